```python
import jax, jax.numpy as jnp
from jax import lax
import numpy as np

D_MODEL = 2048
BATCH = 4
SEQ = 2048
DEPTH = 2
DEC_BATCH = 128
DEC_SEQ = 4
PAST_LEN = 16384
PAGE_SIZE = 128

N_MIXERS = 2
N_LAYERS_A = (DEPTH + 1) // 2
N_LAYERS_B = DEPTH // 2
D_PLE = 256
D_FF = 5632
CHUNK_A = 128
D_A = D_MODEL
GROUPS_A = 16
GROUP_DIM_A = D_A // GROUPS_A
D_B = D_MODEL
HEADS_B = D_B // 128
DK_B = D_B // HEADS_B
DV_B = D_B // HEADS_B
CHUNK_B = 64
EPS = 1e-6

kernel_name = "macaron_gmlp_hgrn2_hybrid_step"


def rms_norm(x, g):
    xf = x.astype(jnp.float32)
    y = xf * lax.rsqrt(jnp.mean(xf * xf, axis=-1, keepdims=True) + EPS)
    return (y * g.astype(jnp.float32)).astype(x.dtype)


def layer_norm(x, g, b):
    xf = x.astype(jnp.float32)
    mu = jnp.mean(xf, axis=-1, keepdims=True)
    xc = xf - mu
    var = jnp.mean(xc * xc, axis=-1, keepdims=True)
    y = xc * lax.rsqrt(var + EPS) * g.astype(jnp.float32) + b.astype(jnp.float32)
    return y.astype(x.dtype)


def swiglu_ffn(x, w_in, w_out):
    gate, up = jnp.split(x @ w_in, 2, axis=-1)
    return (jax.nn.silu(gate) * up) @ w_out


def spatial_gating(v, w_s, b_s):
    bn, L, G, GD = v.shape
    if L <= CHUNK_A:
        c = L
        w = w_s[:, :L, :L]
        bias = b_s[:, :L]
        vv = v[:, None]
    else:
        n = -(-L // CHUNK_A)
        pad = n * CHUNK_A - L
        c = CHUNK_A
        w = w_s
        bias = b_s
        vv = jnp.pad(v, ((0, 0), (0, pad), (0, 0), (0, 0))).reshape(bn, n, CHUNK_A, G, GD)
    mask = jnp.tril(jnp.ones((c, c), dtype=bool))
    w = jnp.where(mask[None], w, jnp.zeros_like(w))
    s = jnp.einsum('gts,bnsgd->bntgd', w, vv) + bias.T[None, None, :, :, None]
    return s.reshape(bn, -1, G, GD)[:, :L]


def mixer_gmlp(xn, w_in, ln_g, ln_b, w_s, b_s, w_out):
    bn, L, _ = xn.shape
    z = jax.nn.gelu(xn @ w_in, approximate=False)
    u, v = jnp.split(z, 2, axis=-1)
    v = layer_norm(v, ln_g, ln_b)
    s = spatial_gating(v.reshape(bn, L, GROUPS_A, GROUP_DIM_A), w_s, b_s).reshape(bn, L, D_A)
    return (u * s) @ w_out, v


def hgrn2_recurrence(q, k, logf, v, s0):
    bn, L, H, K = q.shape
    V = v.shape[-1]
    c = CHUNK_B if L % CHUNK_B == 0 else L
    n = L // c

    def blocks(t):
        return t.reshape(bn, n, c, H, t.shape[-1]).transpose(1, 0, 2, 3, 4)

    mask = jnp.tril(jnp.ones((c, c), dtype=bool))[None, :, :, None, None]

    def step(S, inp):
        qc, kc, fc, vc = inp
        b = jnp.cumsum(fc, axis=1)
        inter = jnp.einsum('bthk,bhkv->bthv', qc * jnp.exp(b), S)
        diff = b[:, :, None] - b[:, None, :]
        decay = jnp.exp(jnp.where(mask, diff, -jnp.inf))
        att = jnp.sum(qc[:, :, None] * decay * kc[:, None], axis=-1)
        intra = jnp.einsum('btsh,bshv->bthv', att, vc)
        b_last = b[:, -1]
        k_dec = kc * jnp.exp(b_last[:, None] - b)
        S_new = jnp.exp(b_last)[..., None] * S + jnp.einsum('bshk,bshv->bhkv', k_dec, vc)
        return S_new, inter + intra

    s_final, out = lax.scan(step, s0, (blocks(q), blocks(k), blocks(logf), blocks(v)))
    return out.transpose(1, 0, 2, 3, 4).reshape(bn, L, H, V), s_final


def mixer_hgrn2(xn, s0, w_in, lb, norm_g, w_out):
    bn, L, _ = xn.shape
    q, fz, i, g = jnp.split(xn @ w_in, 4, axis=-1)
    q = jax.nn.silu(q.astype(jnp.float32)).reshape(bn, L, HEADS_B, DK_B)
    f = lb + (1.0 - lb) * jax.nn.sigmoid(fz.astype(jnp.float32))
    logf = jnp.log(f).reshape(bn, L, HEADS_B, DK_B)
    k = (1.0 - f).reshape(bn, L, HEADS_B, DK_B)
    vi = i.astype(jnp.float32).reshape(bn, L, HEADS_B, DV_B)
    o, s_final = hgrn2_recurrence(q, k, logf, vi, s0.astype(jnp.float32))
    o = o * lax.rsqrt(jnp.mean(o * o, axis=-1, keepdims=True) + EPS)
    o = o * norm_g.astype(jnp.float32).reshape(HEADS_B, DV_B)
    o = o.reshape(bn, L, D_B).astype(xn.dtype) * jax.nn.silu(g)
    return o @ w_out, s_final


def trunk(x, p, states_b, ffn1_w_in, ffn1_w_out, ffn2_w_in, ffn2_w_out,
          norm_ffn1, norm_mix, norm_ffn2, norm_ple,
          a_w_in, a_ln_g, a_ln_b, a_w_s, a_b_s, a_w_out,
          b_w_in, b_lb_logits, b_norm_g, b_w_out,
          ple_w_proj, ple_w_gate, final_norm):
    sm = jax.nn.softmax(b_lb_logits.astype(jnp.float32), axis=0)
    lbs = jnp.cumsum(sm, axis=0) - sm[0]
    h = x
    new_v, new_s = [], []
    for li in range(DEPTH):
        h = h + 0.5 * swiglu_ffn(rms_norm(h, norm_ffn1[li]), ffn1_w_in[li], ffn1_w_out[li])
        xn = rms_norm(h, norm_mix[li])
        j = li // N_MIXERS
        if li % N_MIXERS == 0:
            y, v = mixer_gmlp(xn, a_w_in[j], a_ln_g[j], a_ln_b[j], a_w_s[j], a_b_s[j], a_w_out[j])
            new_v.append(v)
        else:
            y, s_fin = mixer_hgrn2(xn, states_b[j], b_w_in[j], lbs[li], b_norm_g[j], b_w_out[j])
            new_s.append(s_fin)
        h = h + y
        h = h + 0.5 * swiglu_ffn(rms_norm(h, norm_ffn2[li]), ffn2_w_in[li], ffn2_w_out[li])
        gate = jax.nn.sigmoid(rms_norm(h, norm_ple[li]) @ ple_w_gate[li])
        h = h + (p[li] @ ple_w_proj[li]) * gate
    return rms_norm(h, final_norm), jnp.stack(new_v), jnp.stack(new_s)


def setup_inputs(seed: int = 0) -> dict:
    key = jax.random.key(seed)
    ks = iter(jax.random.split(key, 32))

    def nrm(shape, scale):
        return jax.random.normal(next(ks), shape, dtype=jnp.float32) * scale

    def gain(shape):
        return 1.0 + nrm(shape, 0.1)

    d = D_MODEL
    return {
        "x_prompt": nrm((BATCH, SEQ, d), 1.0),
        "x_sample": nrm((DEC_BATCH, DEC_SEQ, d), 1.0),
        "state_hgrn": nrm((N_LAYERS_B, DEC_BATCH, HEADS_B, DK_B, DV_B), 0.5),
        "p_prompt": nrm((DEPTH, BATCH, SEQ, D_PLE), 1.0),
        "p_sample": nrm((DEPTH, DEC_BATCH, DEC_SEQ, D_PLE), 1.0),
        "ffn1_w_in": nrm((DEPTH, d, 2 * D_FF), d ** -0.5),
        "ffn1_w_out": nrm((DEPTH, D_FF, d), D_FF ** -0.5),
        "ffn2_w_in": nrm((DEPTH, d, 2 * D_FF), d ** -0.5),
        "ffn2_w_out": nrm((DEPTH, D_FF, d), D_FF ** -0.5),
        "norm_ffn1": gain((DEPTH, d)),
        "norm_mix": gain((DEPTH, d)),
        "norm_ffn2": gain((DEPTH, d)),
        "norm_ple": gain((DEPTH, d)),
        "a_w_in": nrm((N_LAYERS_A, d, 2 * D_A), d ** -0.5),
        "a_ln_g": gain((N_LAYERS_A, D_A)),
        "a_ln_b": nrm((N_LAYERS_A, D_A), 0.02),
        "a_w_s": nrm((N_LAYERS_A, GROUPS_A, CHUNK_A, CHUNK_A), CHUNK_A ** -0.5),
        "a_b_s": gain((N_LAYERS_A, GROUPS_A, CHUNK_A)),
        "a_w_out": nrm((N_LAYERS_A, D_A, d), D_A ** -0.5),
        "b_w_in": nrm((N_LAYERS_B, d, 4 * D_B), d ** -0.5),
        "b_lb_logits": nrm((DEPTH, D_B), 0.5),
        "b_norm_g": gain((N_LAYERS_B, D_B)),
        "b_w_out": nrm((N_LAYERS_B, D_B, d), D_B ** -0.5),
        "ple_w_proj": nrm((DEPTH, D_PLE, d), D_PLE ** -0.5),
        "ple_w_gate": nrm((DEPTH, d, d), d ** -0.5),
        "final_norm": gain((d,)),
    }


def reference(x_prompt, x_sample, state_hgrn, p_prompt, p_sample,
              ffn1_w_in, ffn1_w_out, ffn2_w_in, ffn2_w_out,
              norm_ffn1, norm_mix, norm_ffn2, norm_ple,
              a_w_in, a_ln_g, a_ln_b, a_w_s, a_b_s, a_w_out,
              b_w_in, b_lb_logits, b_norm_g, b_w_out,
              ple_w_proj, ple_w_gate, final_norm):
    weights = (ffn1_w_in, ffn1_w_out, ffn2_w_in, ffn2_w_out,
               norm_ffn1, norm_mix, norm_ffn2, norm_ple,
               a_w_in, a_ln_g, a_ln_b, a_w_s, a_b_s, a_w_out,
               b_w_in, b_lb_logits, b_norm_g, b_w_out,
               ple_w_proj, ple_w_gate, final_norm)
    bsz = x_prompt.shape[0]
    s0_prompt = jnp.zeros((N_LAYERS_B, bsz, HEADS_B, DK_B, DV_B), dtype=jnp.float32)
    y_prompt, _, state_hgrn_prompt = trunk(x_prompt, p_prompt, s0_prompt, *weights)
    y_sample, state_gmlp_v_sample, state_hgrn_sample = trunk(x_sample, p_sample, state_hgrn, *weights)
    return (y_prompt, y_sample, state_hgrn_prompt, state_hgrn_sample, state_gmlp_v_sample)
```

```python
import functools

import jax
import jax.numpy as jnp
from jax import lax
from jax.experimental import pallas as pl
from jax.experimental.pallas import tpu as pltpu

F32 = jnp.float32
BF16 = jnp.bfloat16
EPS = 1e-6

LANES = 128
SUBLANES = 8
VMEM_LIMIT_BYTES = 56 * 1024 * 1024

TM = 512
TF = 512
TN = 512
HGRN_BLOCK = 128
CHUNK_A = 128
SAMPLE_BATCH_TILE = 4


def _cparams(sem):
    return pltpu.CompilerParams(dimension_semantics=sem, vmem_limit_bytes=VMEM_LIMIT_BYTES)


def _rms(x, g):
    ms = jnp.mean(x * x, axis=-1, keepdims=True)
    return x * lax.rsqrt(ms + EPS) * g


def _resident(shape, index_map):
    return pl.BlockSpec(shape, index_map, pipeline_mode=pl.Buffered(1))


def _ffn_kernel(x_ref, g_ref, wg_ref, wu_ref, wo_ref, o_ref, xn_ref, acc_ref, *, nf):
    j = pl.program_id(1)

    @pl.when(j == 0)
    def _():
        xn_ref[...] = _rms(x_ref[...], g_ref[...]).astype(BF16)
        acc_ref[...] = jnp.zeros_like(acc_ref)

    xn = xn_ref[...]
    gate = jnp.dot(xn, wg_ref[...], preferred_element_type=F32)
    up = jnp.dot(xn, wu_ref[...], preferred_element_type=F32)
    act = (gate * jax.nn.sigmoid(gate) * up).astype(BF16)
    acc_ref[...] += jnp.dot(act, wo_ref[...], preferred_element_type=F32)

    @pl.when(j == nf - 1)
    def _():
        o_ref[...] = x_ref[...] + 0.5 * acc_ref[...]


def _ffn(h, norm_g, w_in, w_out, li):
    t, d = h.shape
    d_ff = w_out.shape[1]
    nf = d_ff // TF
    return pl.pallas_call(
        functools.partial(_ffn_kernel, nf=nf),
        grid=(t // TM, nf),
        in_specs=[
            pl.BlockSpec((TM, d), lambda i, j: (i, 0)),
            pl.BlockSpec((None, 1, d), lambda i, j: (li, 0, 0)),
            pl.BlockSpec((None, d, TF), lambda i, j: (li, 0, j)),
            pl.BlockSpec((None, d, TF), lambda i, j: (li, 0, j + nf)),
            pl.BlockSpec((None, TF, d), lambda i, j: (li, j, 0)),
        ],
        out_specs=pl.BlockSpec((TM, d), lambda i, j: (i, 0)),
        out_shape=jax.ShapeDtypeStruct((t, d), F32),
        scratch_shapes=[pltpu.VMEM((TM, d), BF16), pltpu.VMEM((TM, d), F32)],
        compiler_params=_cparams(("parallel", "arbitrary")),
        name=f"ffn_l{li}",
    )(h, norm_g, w_in, w_in, w_out)


def _ple_kernel(h_ref, p_ref, g_ref, wgate_ref, wproj_ref, fn_ref, o_ref, *, final):
    h = h_ref[...]
    hn = _rms(h, g_ref[...]).astype(BF16)
    gate = jax.nn.sigmoid(jnp.dot(hn, wgate_ref[...], preferred_element_type=F32))
    proj = jnp.dot(p_ref[...].astype(BF16), wproj_ref[...], preferred_element_type=F32)
    out = h + proj * gate
    if final:
        out = _rms(out, fn_ref[...])
    o_ref[...] = out


def _ple(h, p, norm_g, w_gate, w_proj, final_g, li, final):
    t, d = h.shape
    dp = p.shape[-1]
    return pl.pallas_call(
        functools.partial(_ple_kernel, final=final),
        grid=(t // TM,),
        in_specs=[
            pl.BlockSpec((TM, d), lambda i: (i, 0)),
            pl.BlockSpec((None, TM, dp), lambda i: (li, i, 0)),
            _resident((None, 1, d), lambda i: (li, 0, 0)),
            _resident((None, d, d), lambda i: (li, 0, 0)),
            _resident((None, dp, d), lambda i: (li, 0, 0)),
            _resident((1, d), lambda i: (0, 0)),
        ],
        out_specs=pl.BlockSpec((TM, d), lambda i: (i, 0)),
        out_shape=jax.ShapeDtypeStruct((t, d), F32),
        compiler_params=_cparams(("parallel",)),
        name=f"ple_l{li}",
    )(h, p, norm_g, w_gate, w_proj, final_g)


def _gelu(z):
    return 0.5 * z * (1.0 + lax.erf(z * (2.0 ** -0.5)))


def _gmlp_proj_kernel(x_ref, g_ref, w_ref, lng_ref, lnb_ref, o_ref, xn_ref, *, layer_norm):
    @pl.when(pl.program_id(1) == 0)
    def _():
        xn_ref[...] = _rms(x_ref[...], g_ref[...]).astype(BF16)

    z = _gelu(jnp.dot(xn_ref[...], w_ref[...], preferred_element_type=F32))
    if layer_norm:
        mu = jnp.mean(z, axis=-1, keepdims=True)
        zc = z - mu
        var = jnp.mean(zc * zc, axis=-1, keepdims=True)
        z = zc * lax.rsqrt(var + EPS) * lng_ref[...] + lnb_ref[...]
    o_ref[...] = z.astype(o_ref.dtype)


def _gmlp_proj(h, norm_g, w_in, ln_g, ln_b, li, j_layer, *, half, layer_norm, out_dtype):
    t, d = h.shape
    d_a = w_in.shape[-1] // 2
    tn = d_a if layer_norm else TN
    nn = d_a // tn
    off = half * nn
    return pl.pallas_call(
        functools.partial(_gmlp_proj_kernel, layer_norm=layer_norm),
        grid=(t // TM, nn),
        in_specs=[
            pl.BlockSpec((TM, d), lambda i, j: (i, 0)),
            pl.BlockSpec((None, 1, d), lambda i, j: (li, 0, 0)),
            pl.BlockSpec((None, d, tn), lambda i, j: (j_layer, 0, j + off)),
            pl.BlockSpec((None, 1, tn), lambda i, j: (j_layer, 0, j)),
            pl.BlockSpec((None, 1, tn), lambda i, j: (j_layer, 0, j)),
        ],
        out_specs=pl.BlockSpec((TM, tn), lambda i, j: (i, j)),
        out_shape=jax.ShapeDtypeStruct((t, d_a), out_dtype),
        scratch_shapes=[pltpu.VMEM((TM, d), BF16)],
        compiler_params=_cparams(("parallel", "arbitrary")),
        name=f"gmlp_proj_l{li}_{'v' if half else 'u'}",
    )(h, norm_g, w_in, ln_g, ln_b)


def _gmlp_mix_kernel(h_ref, u_ref, v_ref, wmix_ref, bias_ref, wo_ref, o_ref, t_ref, *, groups):
    for c in range(TM // CHUNK_A):
        rows = slice(c * CHUNK_A, (c + 1) * CHUNK_A)
        for g in range(groups):
            cols = slice(g * LANES, (g + 1) * LANES)
            s = jnp.dot(wmix_ref[g], v_ref[rows, cols].astype(BF16), preferred_element_type=F32)
            s = s + bias_ref[:, cols]
            t_ref[rows, cols] = (u_ref[rows, cols].astype(F32) * s).astype(BF16)
    o_ref[...] = h_ref[...] + jnp.dot(t_ref[...], wo_ref[...], preferred_element_type=F32)


def _gmlp_mix(h, u, v, wmix, bias, w_out, j_layer, n_prompt_tiles):
    t, d = h.shape
    d_a = u.shape[1]
    groups = d_a // LANES
    return pl.pallas_call(
        functools.partial(_gmlp_mix_kernel, groups=groups),
        grid=(t // TM,),
        in_specs=[
            pl.BlockSpec((TM, d), lambda i: (i, 0)),
            pl.BlockSpec((TM, d_a), lambda i: (i, 0)),
            pl.BlockSpec((TM, d_a), lambda i: (i, 0)),
            pl.BlockSpec((None, groups, CHUNK_A, CHUNK_A), lambda i: (i // n_prompt_tiles, 0, 0, 0)),
            pl.BlockSpec((None, CHUNK_A, d_a), lambda i: (i // n_prompt_tiles, 0, 0)),
            _resident((None, d_a, d), lambda i: (j_layer, 0, 0)),
        ],
        out_specs=pl.BlockSpec((TM, d), lambda i: (i, 0)),
        out_shape=jax.ShapeDtypeStruct((t, d), F32),
        scratch_shapes=[pltpu.VMEM((TM, d_a), BF16)],
        compiler_params=_cparams(("parallel",)),
        name="gmlp_mix",
    )(h, u, v, wmix, bias, w_out)


def _hgrn_proj_kernel(x_ref, g_ref, w_ref, lbl_ref, o_ref, xn_ref, *, li, per_part):
    j = pl.program_id(1)

    @pl.when(j == 0)
    def _():
        xn_ref[...] = _rms(x_ref[...], g_ref[...]).astype(BF16)

    z = jnp.dot(xn_ref[...], w_ref[...], preferred_element_type=F32)
    sig = jax.nn.sigmoid(z)
    logits = lbl_ref[...]
    ex = jnp.exp(logits - jnp.max(logits, axis=0, keepdims=True))
    sm = ex / jnp.sum(ex, axis=0, keepdims=True)
    lb = jnp.sum(sm[: li + 1], axis=0, keepdims=True) - sm[0:1]
    part = j // per_part
    res = jnp.where(part == 1, lb + (1.0 - lb) * sig, jnp.where(part == 2, z, z * sig))
    for c in range(TN // LANES):
        o_ref[c] = res[:, c * LANES:(c + 1) * LANES]


def _hgrn_proj(h, norm_g, w_in, lb_logits, li, j_layer):
    t, d = h.shape
    n = w_in.shape[-1]
    d_b = n // 4
    per_part = d_b // TN
    return pl.pallas_call(
        functools.partial(_hgrn_proj_kernel, li=li, per_part=per_part),
        grid=(t // TM, n // TN),
        in_specs=[
            pl.BlockSpec((TM, d), lambda i, j: (i, 0)),
            pl.BlockSpec((None, 1, d), lambda i, j: (li, 0, 0)),
            pl.BlockSpec((None, d, TN), lambda i, j: (j_layer, 0, j)),
            pl.BlockSpec((lb_logits.shape[0], TN), lambda i, j: (0, j % per_part)),
        ],
        out_specs=pl.BlockSpec((TN // LANES, TM, LANES), lambda i, j: (j, i, 0)),
        out_shape=jax.ShapeDtypeStruct((n // LANES, t, LANES), F32),
        scratch_shapes=[pltpu.VMEM((TM, d), BF16)],
        compiler_params=_cparams(("parallel", "arbitrary")),
        name="hgrn_proj",
    )(h, norm_g, w_in, lb_logits)


def _cumsum_rows(x, row):
    n = x.shape[0]
    shift = 1
    while shift < n:
        x = x + jnp.where(row >= shift, pltpu.roll(x, shift, 0), 0.0)
        shift *= 2
    return x


def _dot_nt(a, b):
    return lax.dot_general(a, b, (((1,), (1,)), ((), ())), preferred_element_type=F32)


def _hgrn_norm_gate(o, ng, gate):
    ms = jnp.mean(o * o, axis=-1, keepdims=True)
    return (o * lax.rsqrt(ms + EPS) * ng * gate).astype(BF16)


def _hgrn_prompt_kernel(q_ref, f_ref, i_ref, g_ref, ng_ref, og_ref, sfin_ref, st_ref, *, nblk):
    c = HGRN_BLOCK
    st_ref[...] = jnp.zeros_like(st_ref)
    row = lax.broadcasted_iota(jnp.int32, (c, LANES), 0)
    col = lax.broadcasted_iota(jnp.int32, (c, c), 1)
    sub = lax.broadcasted_iota(jnp.int32, (c // SUBLANES, SUBLANES, 1), 1)
    ng = ng_ref[...]

    def body(blk, carry):
        rows = pl.ds(pl.multiple_of(blk * c, c), c)
        q = q_ref[rows, :]
        f = f_ref[rows, :]
        v = i_ref[rows, :]
        k = 1.0 - f
        b = _cumsum_rows(jnp.log(f), row)
        st = st_ref[...]
        vb = v.astype(BF16)

        o = _dot_nt((q * jnp.exp(b)).astype(BF16), st.astype(BF16))

        att = None
        half = c // 2
        while half >= SUBLANES:
            par = 2 * half
            pieces = [jnp.broadcast_to(b[p * par + half - 1:p * par + half, :], (par, LANES))
                      for p in range(c // par)]
            beta = pieces[0] if len(pieces) == 1 else jnp.concatenate(pieces, axis=0)
            d = b - beta
            e = jnp.exp(jnp.minimum(d, -d))
            second = (row & half) != 0
            qt = jnp.where(second, q * e, 0.0).astype(BF16)
            kt = jnp.where(second, 0.0, k * e).astype(BF16)
            a = _dot_nt(qt, kt)
            if par < c:
                a = jnp.where((row // par) == (col // par), a, 0.0)
            att = a if att is None else att + a
            half //= 2
        o = o + jnp.dot(att.astype(BF16), vb, preferred_element_type=F32)

        shape3 = (c // SUBLANES, SUBLANES, LANES)
        q3, k3, b3, v3 = (x.reshape(shape3) for x in (q, k, b, v))
        od = jnp.zeros(shape3, F32)
        for s in range(SUBLANES):
            e = jnp.exp(jnp.minimum(b3 - b3[:, s:s + 1, :], 0.0))
            w = jnp.sum(q3 * e * k3[:, s:s + 1, :], axis=-1, keepdims=True)
            od = od + jnp.where(sub >= s, w, 0.0) * v3[:, s:s + 1, :]
        o = o + od.reshape(c, LANES)

        og_ref[rows, :] = _hgrn_norm_gate(o, ng, g_ref[rows, :])

        b_last = b[c - 1:c, :]
        kd = (k * jnp.exp(b_last - b)).astype(BF16)
        st_ref[...] = st * jnp.exp(b_last) + jnp.dot(v.T.astype(BF16), kd, preferred_element_type=F32)
        return carry

    lax.fori_loop(0, nblk, body, 0)
    sfin_ref[...] = st_ref[...].T


def _hgrn_prompt(qfig, norm_g, batch, seq, heads, t):
    nblk = seq // HGRN_BLOCK
    part = lambda p: pl.BlockSpec((None, seq, LANES), lambda s, h, p=p: (p * heads + h, s, 0))
    return pl.pallas_call(
        functools.partial(_hgrn_prompt_kernel, nblk=nblk),
        grid=(batch, heads),
        in_specs=[part(0), part(1), part(2), part(3),
                  pl.BlockSpec((None, 1, LANES), lambda s, h: (h, 0, 0))],
        out_specs=[pl.BlockSpec((seq, LANES), lambda s, h: (s, h)),
                   pl.BlockSpec((None, None, LANES, LANES), lambda s, h: (s, h, 0, 0))],
        out_shape=[jax.ShapeDtypeStruct((t, heads * LANES), BF16),
                   jax.ShapeDtypeStruct((batch, heads, LANES, LANES), F32)],
        scratch_shapes=[pltpu.VMEM((LANES, LANES), F32)],
        compiler_params=_cparams(("parallel", "parallel")),
        name="hgrn_prompt",
    )(qfig, qfig, qfig, qfig, norm_g)


def _hgrn_sample_kernel(q_ref, f_ref, i_ref, g_ref, ng_ref, s0_ref, og_in_ref, og_ref, s1_ref,
                        *, heads, dec_seq):
    del og_in_ref
    rows_per_tile = SUBLANES
    per_tile = rows_per_tile // dec_seq
    row = lax.broadcasted_iota(jnp.int32, (rows_per_tile, LANES), 0)
    pos = row % dec_seq
    first = row < dec_seq
    eye_c = lax.broadcasted_iota(jnp.int32, (LANES, LANES), 1)
    pad = jnp.zeros((LANES - 2 * rows_per_tile, LANES), F32)
    zrow = jnp.zeros((rows_per_tile - per_tile, LANES), F32)

    def pick(x, s):
        return jnp.where(first, x[s:s + 1, :], x[dec_seq + s:dec_seq + s + 1, :])

    for tile in range(SAMPLE_BATCH_TILE // per_tile):
        rows = slice(tile * rows_per_tile, (tile + 1) * rows_per_tile)
        for h in range(heads):
            q = q_ref[h, rows, :]
            f = f_ref[h, rows, :]
            v = i_ref[h, rows, :]
            k = 1.0 - f
            b = jnp.log(f)
            shift = 1
            while shift < dec_seq:
                b = b + jnp.where(pos >= shift, pltpu.roll(b, shift, 0), 0.0)
                shift *= 2
            qe = (q * jnp.exp(b)).astype(BF16)
            s_a = s0_ref[tile * per_tile, h]
            s_b = s0_ref[tile * per_tile + 1, h]
            o = jnp.where(first,
                          jnp.dot(qe, s_a.astype(BF16), preferred_element_type=F32),
                          jnp.dot(qe, s_b.astype(BF16), preferred_element_type=F32))
            for s in range(dec_seq):
                e = jnp.exp(jnp.minimum(b - pick(b, s), 0.0))
                w = jnp.sum(q * e * pick(k, s), axis=-1, keepdims=True)
                o = o + jnp.where(pos >= s, w, 0.0) * pick(v, s)
            cols = slice(h * LANES, (h + 1) * LANES)
            og_ref[rows, cols] = _hgrn_norm_gate(o, ng_ref[h], g_ref[h, rows, :])

            b_last = pick(b, dec_seq - 1)
            kd = k * jnp.exp(b_last - b)
            e_rows = jnp.concatenate([jnp.exp(b[dec_seq - 1:dec_seq, :]),
                                      jnp.exp(b[2 * dec_seq - 1:2 * dec_seq, :]), zrow], axis=0)
            cols_t = jnp.concatenate([kd, e_rows, pad], axis=0).T
            kd_t = cols_t.astype(BF16)
            v_a = jnp.concatenate([jnp.where(first, v, 0.0), jnp.zeros((LANES - rows_per_tile, LANES), F32)], axis=0)
            v_b = jnp.concatenate([jnp.where(first, 0.0, v), jnp.zeros((LANES - rows_per_tile, LANES), F32)], axis=0)
            e_a = jnp.sum(jnp.where(eye_c == rows_per_tile, cols_t, 0.0), axis=-1, keepdims=True)
            e_b = jnp.sum(jnp.where(eye_c == rows_per_tile + 1, cols_t, 0.0), axis=-1, keepdims=True)
            s1_ref[tile * per_tile, h] = s_a * e_a + jnp.dot(kd_t, v_a.astype(BF16), preferred_element_type=F32)
            s1_ref[tile * per_tile + 1, h] = s_b * e_b + jnp.dot(kd_t, v_b.astype(BF16), preferred_element_type=F32)


def _hgrn_sample(qfig, norm_g, s0, og, n_prompt, dec_batch, dec_seq, heads):
    rows = SAMPLE_BATCH_TILE * dec_seq
    first_blk = n_prompt // rows
    part = lambda p: pl.BlockSpec((heads, rows, LANES), lambda i, p=p: (p, first_blk + i, 0))
    return pl.pallas_call(
        functools.partial(_hgrn_sample_kernel, heads=heads, dec_seq=dec_seq),
        grid=(dec_batch // SAMPLE_BATCH_TILE,),
        in_specs=[part(0), part(1), part(2), part(3),
                  _resident((heads, 1, LANES), lambda i: (0, 0, 0)),
                  pl.BlockSpec((SAMPLE_BATCH_TILE, heads, LANES, LANES), lambda i: (i, 0, 0, 0)),
                  pl.BlockSpec(memory_space=pl.ANY)],
        out_specs=[pl.BlockSpec((rows, heads * LANES), lambda i: (first_blk + i, 0)),
                   pl.BlockSpec((SAMPLE_BATCH_TILE, heads, LANES, LANES), lambda i: (i, 0, 0, 0))],
        out_shape=[jax.ShapeDtypeStruct(og.shape, og.dtype),
                   jax.ShapeDtypeStruct(s0.shape, F32)],
        input_output_aliases={6: 0},
        compiler_params=_cparams(("parallel",)),
        name="hgrn_sample",
    )(qfig, qfig, qfig, qfig, norm_g, s0, og)


def _out_proj_kernel(h_ref, x_ref, w_ref, o_ref):
    o_ref[...] = h_ref[...] + jnp.dot(x_ref[...], w_ref[...], preferred_element_type=F32)


def _out_proj(h, x, w, j_layer):
    t, d = h.shape
    kdim = x.shape[1]
    return pl.pallas_call(
        _out_proj_kernel,
        grid=(t // TM,),
        in_specs=[
            pl.BlockSpec((TM, d), lambda i: (i, 0)),
            pl.BlockSpec((TM, kdim), lambda i: (i, 0)),
            _resident((None, kdim, d), lambda i: (j_layer, 0, 0)),
        ],
        out_specs=pl.BlockSpec((TM, d), lambda i: (i, 0)),
        out_shape=jax.ShapeDtypeStruct((t, d), F32),
        compiler_params=_cparams(("parallel",)),
        name="hgrn_out_proj",
    )(h, x, w)


def kernel(x_prompt, x_sample, state_hgrn, p_prompt, p_sample, ffn1_w_in, ffn1_w_out, ffn2_w_in, ffn2_w_out, norm_ffn1, norm_mix, norm_ffn2, norm_ple, a_w_in, a_ln_g, a_ln_b, a_w_s, a_b_s, a_w_out, b_w_in, b_lb_logits, b_norm_g, b_w_out, ple_w_proj, ple_w_gate, final_norm):
    batch, seq, d = x_prompt.shape
    dec_batch, dec_seq, _ = x_sample.shape
    depth = ffn1_w_in.shape[0]
    n_prompt = batch * seq
    n_sample = dec_batch * dec_seq
    t = n_prompt + n_sample
    heads = b_norm_g.shape[-1] // LANES
    groups = a_w_s.shape[1]
    assert n_prompt % TM == 0 and n_sample % TM == 0 and seq % HGRN_BLOCK == 0
    assert CHUNK_A % dec_seq == 0 and SUBLANES % dec_seq == 0 and dec_batch % SAMPLE_BATCH_TILE == 0
    assert a_w_s.shape[-1] == CHUNK_A and seq % CHUNK_A == 0

    h = jnp.concatenate([x_prompt.reshape(n_prompt, d), x_sample.reshape(n_sample, d)], axis=0)
    p = jnp.concatenate([p_prompt.reshape(depth, n_prompt, -1), p_sample.reshape(depth, n_sample, -1)], axis=1)

    bf = lambda w: w.astype(BF16)
    vec = lambda g: g.reshape(g.shape[0], 1, g.shape[-1])
    ffn1_w_in, ffn1_w_out, ffn2_w_in, ffn2_w_out = bf(ffn1_w_in), bf(ffn1_w_out), bf(ffn2_w_in), bf(ffn2_w_out)
    a_w_in_b, a_w_out_b, b_w_in_b, b_w_out_b = bf(a_w_in), bf(a_w_out), bf(b_w_in), bf(b_w_out)
    ple_w_proj_b, ple_w_gate_b = bf(ple_w_proj), bf(ple_w_gate)
    n_ffn1, n_mix, n_ffn2, n_ple = vec(norm_ffn1), vec(norm_mix), vec(norm_ffn2), vec(norm_ple)
    ln_g, ln_b = vec(a_ln_g), vec(a_ln_b)
    final_g = final_norm.reshape(1, d)

    reps = CHUNK_A // dec_seq
    tri = jnp.tril(jnp.ones((CHUNK_A, CHUNK_A), bool))
    pos_c = jnp.arange(CHUNK_A) // dec_seq
    blockdiag = pos_c[:, None] == pos_c[None, :]
    w_prompt = jnp.where(tri, a_w_s, 0.0)
    w_sample = jnp.where(tri & blockdiag, jnp.tile(a_w_s[:, :, :dec_seq, :dec_seq], (1, 1, reps, reps)), 0.0)
    wmix = jnp.stack([w_prompt, w_sample], axis=1).astype(BF16)
    b_prompt = jnp.swapaxes(a_b_s, 1, 2)
    b_sample = jnp.tile(b_prompt[:, :dec_seq], (1, reps, 1))
    bias = jnp.repeat(jnp.stack([b_prompt, b_sample], axis=1), LANES, axis=-1)

    new_v, s_prompt, s_sample = [], [], []
    for li in range(depth):
        h = _ffn(h, n_ffn1, ffn1_w_in, ffn1_w_out, li)
        j = li // 2
        if li % 2 == 0:
            u = _gmlp_proj(h, n_mix, a_w_in_b, ln_g, ln_b, li, j, half=0, layer_norm=False, out_dtype=BF16)
            v = _gmlp_proj(h, n_mix, a_w_in_b, ln_g, ln_b, li, j, half=1, layer_norm=True, out_dtype=F32)
            new_v.append(v[n_prompt:].reshape(dec_batch, dec_seq, -1))
            h = _gmlp_mix(h, u, v, wmix[j], bias[j], a_w_out_b, j, n_prompt // TM)
        else:
            qfig = _hgrn_proj(h, n_mix, b_w_in_b, b_lb_logits, li, j)
            ng = b_norm_g[j].reshape(heads, 1, LANES)
            og, s_fin = _hgrn_prompt(qfig, ng, batch, seq, heads, t)
            og, s_new = _hgrn_sample(qfig, ng, state_hgrn[j], og, n_prompt, dec_batch, dec_seq, heads)
            s_prompt.append(s_fin)
            s_sample.append(s_new)
            h = _out_proj(h, og, b_w_out_b, j)
        h = _ffn(h, n_ffn2, ffn2_w_in, ffn2_w_out, li)
        h = _ple(h, p, n_ple, ple_w_gate_b, ple_w_proj_b, final_g, li, final=(li == depth - 1))

    y_prompt = h[:n_prompt].reshape(batch, seq, d)
    y_sample = h[n_prompt:].reshape(dec_batch, dec_seq, d)
    return (y_prompt, y_sample, jnp.stack(s_prompt), jnp.stack(s_sample), jnp.stack(new_v))
```

```python
import functools

import jax
import jax.numpy as jnp
from jax import lax
from jax.experimental import pallas as pl
from jax.experimental.pallas import tpu as pltpu

F32 = jnp.float32
BF16 = jnp.bfloat16
EPS = 1e-6

LANES = 128
SUBLANES = 8
VMEM_LIMIT_BYTES = 56 * 1024 * 1024

TM = 512
TM_STREAM = 1088
TF = 256
TN = 512
HGRN_BLOCK = 128
HGRN_HEADS_PER_STEP = 4
CHUNK_A = 128
SAMPLE_BATCH_TILE = 4


def _cparams(sem):
    return pltpu.CompilerParams(dimension_semantics=sem, vmem_limit_bytes=VMEM_LIMIT_BYTES)


def _rms(x, g):
    ms = jnp.mean(x * x, axis=-1, keepdims=True)
    return x * lax.rsqrt(ms + EPS) * g


def _resident(shape, index_map):
    return pl.BlockSpec(shape, index_map, pipeline_mode=pl.Buffered(1))


def _dot(a, b):
    return jnp.dot(a, b, preferred_element_type=F32)


def _dot_nt(a, b):
    return lax.dot_general(a, b, (((1,), (1,)), ((), ())), preferred_element_type=F32)


def _dot_tn(a, b):
    return lax.dot_general(a, b, (((0,), (0,)), ((), ())), preferred_element_type=F32)


def _ffn_kernel(x_ref, g_ref, wg_ref, wu_ref, wo_ref, o_ref, xn_ref):
    @pl.when(pl.program_id(1) == 0)
    def _():
        x = x_ref[...]
        xn_ref[...] = _rms(x, g_ref[...]).astype(BF16)
        o_ref[...] = x

    xn = xn_ref[...]
    gate = _dot(xn, wg_ref[...].astype(BF16))
    up = _dot(xn, wu_ref[...].astype(BF16))
    act = (0.5 * gate * jax.nn.sigmoid(gate) * up).astype(BF16)
    o_ref[...] += _dot(act, wo_ref[...].astype(BF16))


def _ffn(h, norm_g, w_in, w_out, li):
    t, d = h.shape
    d_ff = w_out.shape[1]
    nf = d_ff // TF
    return pl.pallas_call(
        _ffn_kernel,
        grid=(t // TM_STREAM, nf),
        in_specs=[
            pl.BlockSpec((TM_STREAM, d), lambda i, j: (i, 0), pipeline_mode=pl.Buffered(1)),
            pl.BlockSpec((None, 1, d), lambda i, j: (li, 0, 0)),
            pl.BlockSpec((None, d, TF), lambda i, j: (li, 0, j)),
            pl.BlockSpec((None, d, TF), lambda i, j: (li, 0, j + nf)),
            pl.BlockSpec((None, TF, d), lambda i, j: (li, j, 0)),
        ],
        out_specs=pl.BlockSpec((TM_STREAM, d), lambda i, j: (i, 0)),
        out_shape=jax.ShapeDtypeStruct((t, d), F32),
        scratch_shapes=[pltpu.VMEM((TM_STREAM, d), BF16)],
        compiler_params=_cparams(("parallel", "arbitrary")),
        name=f"ffn_l{li}",
    )(h, norm_g, w_in, w_in, w_out)


def _ple_kernel(h_ref, p_ref, g_ref, wgate_ref, wproj_ref, fn_ref, o_ref, *, final):
    h = h_ref[...]
    hn = _rms(h, g_ref[...]).astype(BF16)
    gate = jax.nn.sigmoid(_dot(hn, wgate_ref[...]))
    proj = _dot(p_ref[...].astype(BF16), wproj_ref[...])
    out = h + proj * gate
    if final:
        out = _rms(out, fn_ref[...])
    o_ref[...] = out


def _ple(h, p, norm_g, w_gate, w_proj, final_g, li, final):
    t, d = h.shape
    dp = p.shape[-1]
    return pl.pallas_call(
        functools.partial(_ple_kernel, final=final),
        grid=(t // TM,),
        in_specs=[
            pl.BlockSpec((TM, d), lambda i: (i, 0)),
            pl.BlockSpec((None, TM, dp), lambda i: (li, i, 0)),
            _resident((None, 1, d), lambda i: (li, 0, 0)),
            _resident((None, d, d), lambda i: (li, 0, 0)),
            _resident((None, dp, d), lambda i: (li, 0, 0)),
            _resident((1, d), lambda i: (0, 0)),
        ],
        out_specs=pl.BlockSpec((TM, d), lambda i: (i, 0)),
        out_shape=jax.ShapeDtypeStruct((t, d), F32),
        compiler_params=_cparams(("parallel",)),
        name=f"ple_l{li}",
    )(h, p, norm_g, w_gate, w_proj, final_g)


def _gelu(z):
    return 0.5 * z * (1.0 + lax.erf(z * (2.0 ** -0.5)))


def _gmlp_proj_kernel(x_ref, g_ref, w_ref, lng_ref, lnb_ref, o_ref, xn_ref, *, layer_norm):
    @pl.when(pl.program_id(1) == 0)
    def _():
        xn_ref[...] = _rms(x_ref[...], g_ref[...]).astype(BF16)

    z = _gelu(_dot(xn_ref[...], w_ref[...]))
    if layer_norm:
        mu = jnp.mean(z, axis=-1, keepdims=True)
        zc = z - mu
        var = jnp.mean(zc * zc, axis=-1, keepdims=True)
        z = zc * lax.rsqrt(var + EPS) * lng_ref[...] + lnb_ref[...]
    o_ref[...] = z.astype(o_ref.dtype)


def _gmlp_proj(h, norm_g, w_in, ln_g, ln_b, li, j_layer, *, half, layer_norm, out_dtype):
    t, d = h.shape
    d_a = w_in.shape[-1] // 2
    tn = d_a if layer_norm else TN
    nn = d_a // tn
    off = half * nn
    return pl.pallas_call(
        functools.partial(_gmlp_proj_kernel, layer_norm=layer_norm),
        grid=(t // TM, nn),
        in_specs=[
            pl.BlockSpec((TM, d), lambda i, j: (i, 0)),
            pl.BlockSpec((None, 1, d), lambda i, j: (li, 0, 0)),
            pl.BlockSpec((None, d, tn), lambda i, j: (j_layer, 0, j + off)),
            pl.BlockSpec((None, 1, tn), lambda i, j: (j_layer, 0, j)),
            pl.BlockSpec((None, 1, tn), lambda i, j: (j_layer, 0, j)),
        ],
        out_specs=pl.BlockSpec((TM, tn), lambda i, j: (i, j)),
        out_shape=jax.ShapeDtypeStruct((t, d_a), out_dtype),
        scratch_shapes=[pltpu.VMEM((TM, d), BF16)],
        compiler_params=_cparams(("parallel", "arbitrary")),
        name=f"gmlp_proj_l{li}_{'v' if half else 'u'}",
    )(h, norm_g, w_in, ln_g, ln_b)


def _gmlp_mix_kernel(h_ref, u_ref, v_ref, wmix_ref, bias_ref, wo_ref, o_ref, t_ref, *, groups):
    for c in range(TM // CHUNK_A):
        rows = slice(c * CHUNK_A, (c + 1) * CHUNK_A)
        for g in range(groups):
            cols = slice(g * LANES, (g + 1) * LANES)
            s = _dot(wmix_ref[g], v_ref[rows, cols].astype(BF16)) + bias_ref[:, cols]
            t_ref[rows, cols] = (u_ref[rows, cols].astype(F32) * s).astype(BF16)
    o_ref[...] = h_ref[...] + _dot(t_ref[...], wo_ref[...])


def _gmlp_mix(h, u, v, wmix, bias, w_out, j_layer, n_prompt_tiles):
    t, d = h.shape
    d_a = u.shape[1]
    groups = d_a // LANES
    return pl.pallas_call(
        functools.partial(_gmlp_mix_kernel, groups=groups),
        grid=(t // TM,),
        in_specs=[
            pl.BlockSpec((TM, d), lambda i: (i, 0)),
            pl.BlockSpec((TM, d_a), lambda i: (i, 0)),
            pl.BlockSpec((TM, d_a), lambda i: (i, 0)),
            pl.BlockSpec((None, groups, CHUNK_A, CHUNK_A), lambda i: (i // n_prompt_tiles, 0, 0, 0)),
            pl.BlockSpec((None, CHUNK_A, d_a), lambda i: (i // n_prompt_tiles, 0, 0)),
            _resident((None, d_a, d), lambda i: (j_layer, 0, 0)),
        ],
        out_specs=pl.BlockSpec((TM, d), lambda i: (i, 0)),
        out_shape=jax.ShapeDtypeStruct((t, d), F32),
        scratch_shapes=[pltpu.VMEM((TM, d_a), BF16)],
        compiler_params=_cparams(("parallel",)),
        name="gmlp_mix",
    )(h, u, v, wmix, bias, w_out)


def _hgrn_proj_kernel(x_ref, g_ref, w_ref, lbl_ref, o_ref, xn_ref, *, li, forget, per_part):
    j = pl.program_id(1)

    @pl.when(j == 0)
    def _():
        xn_ref[...] = _rms(x_ref[...], g_ref[...]).astype(BF16)

    z = _dot(xn_ref[...], w_ref[...].astype(BF16))
    sig = jax.nn.sigmoid(z)
    if forget:
        logits = lbl_ref[...]
        ex = jnp.exp(logits - jnp.max(logits, axis=0, keepdims=True))
        sm = ex / jnp.sum(ex, axis=0, keepdims=True)
        lb = jnp.sum(sm[: li + 1], axis=0, keepdims=True) - sm[0:1]
        res = lb + (1.0 - lb) * sig
    else:
        res = jnp.where(j // per_part == 1, z, z * sig)
    o_ref[...] = res.astype(o_ref.dtype)


def _hgrn_proj(h, norm_g, w_in, lb_logits, li, j_layer, *, forget):
    t, d = h.shape
    d_b = w_in.shape[-1] // 4
    per_part = d_b // TN
    if forget:
        n_out, out_dtype = d_b, F32
        wcol = lambda j: j + per_part
    else:
        n_out, out_dtype = 3 * d_b, BF16
        wcol = lambda j: j + jnp.where(j >= per_part, per_part, 0)
    return pl.pallas_call(
        functools.partial(_hgrn_proj_kernel, li=li, forget=forget, per_part=per_part),
        grid=(t // TM_STREAM, n_out // TN),
        in_specs=[
            pl.BlockSpec((TM_STREAM, d), lambda i, j: (i, 0)),
            pl.BlockSpec((None, 1, d), lambda i, j: (li, 0, 0)),
            pl.BlockSpec((None, d, TN), lambda i, j: (j_layer, 0, wcol(j))),
            pl.BlockSpec((lb_logits.shape[0], TN), lambda i, j: (0, j % per_part)),
        ],
        out_specs=pl.BlockSpec((TM_STREAM, TN), lambda i, j: (i, j)),
        out_shape=jax.ShapeDtypeStruct((t, n_out), out_dtype),
        scratch_shapes=[pltpu.VMEM((TM_STREAM, d), BF16)],
        compiler_params=_cparams(("parallel", "arbitrary")),
        name="hgrn_proj_f" if forget else "hgrn_proj_qig",
    )(h, norm_g, w_in, lb_logits)


def _hgrn_halves():
    halves, half = [], HGRN_BLOCK // 2
    while half >= 1:
        halves.append(half)
        half //= 2
    return tuple(halves)


def _hgrn_norm_gate(o, ng, gate):
    ms = jnp.mean(o * o, axis=-1, keepdims=True)
    return (o * lax.rsqrt(ms + EPS) * ng * gate).astype(BF16)


def _hgrn_prompt_kernel(f_ref, q_ref, i_ref, g_ref, ng_ref, og_ref, sfin_ref,
                        st_ref, tril_ref, sgn_ref, mask_ref, *, nblk, hps):
    c = HGRN_BLOCK
    w = hps * LANES
    halves = _hgrn_halves()
    nl = len(halves)

    @pl.when((pl.program_id(0) == 0) & (pl.program_id(1) == 0))
    def _():
        r = lax.broadcasted_iota(jnp.int32, (c, c), 0)
        s = lax.broadcasted_iota(jnp.int32, (c, c), 1)
        tril_ref[...] = jnp.where(s <= r, 1.0, 0.0).astype(BF16)
        for lvl, half in enumerate(halves):
            par = 2 * half
            second = (r & half) != 0
            sgn_ref[lvl] = jnp.where(second, 1.0, -1.0)
            valid = second & ((s & half) == 0) & ((r // par) == (s // par))
            mask_ref[lvl] = jnp.where(valid, 1.0, 0.0)
        mask_ref[nl] = jnp.where(r == s, 1.0, 0.0)

    st_ref[...] = jnp.zeros_like(st_ref)
    low_sub = lax.broadcasted_iota(jnp.int32, (c // SUBLANES, SUBLANES, w), 1) < SUBLANES // 2

    def body(blk, carry):
        rows = pl.ds(pl.multiple_of(blk * c, c), c)
        f = f_ref[rows, :]
        qb = q_ref[rows, :]
        vb = i_ref[rows, :]
        q = qb.astype(F32)
        k = 1.0 - f
        kb = k.astype(BF16)

        x = jnp.log(f)
        hi = x.astype(BF16)
        r1 = x - hi.astype(F32)
        mid = r1.astype(BF16)
        lo = (r1 - mid.astype(F32)).astype(BF16)
        tril = tril_ref[...]
        b = _dot(tril, hi) + _dot(tril, mid) + _dot(tril, lo)
        b3 = b.reshape(c // SUBLANES, SUBLANES, w)

        att = [None] * hps

        def add_level(lvl, qt, kt):
            for hd in range(hps):
                cols = slice(hd * LANES, (hd + 1) * LANES)
                a = _dot_nt(qt[:, cols], kt[:, cols]) * mask_ref[lvl]
                att[hd] = a if att[hd] is None else att[hd] + a

        for lvl, half in enumerate(halves):
            par = 2 * half
            if half == 1:
                add_level(lvl, (q * f).astype(BF16), kb)
                continue
            if half >= SUBLANES:
                pieces = [jnp.broadcast_to(b[p * par + half - 1:p * par + half, :], (par, w))
                          for p in range(c // par)]
                beta = pieces[0] if len(pieces) == 1 else jnp.concatenate(pieces, axis=0)
            elif half == SUBLANES // 2:
                beta = jnp.broadcast_to(b3[:, half - 1:half, :], b3.shape).reshape(c, w)
            else:
                beta = jnp.where(low_sub, b3[:, half - 1:half, :],
                                 b3[:, par + half - 1:par + half, :]).reshape(c, w)
            sgn = jnp.concatenate([sgn_ref[lvl]] * hps, axis=1)
            e = jnp.exp((b - beta) * sgn)
            add_level(lvl, (q * e).astype(BF16), (k * e).astype(BF16))
        add_level(nl, qb, kb)

        qe = (q * jnp.exp(b)).astype(BF16)
        b_last = b[c - 1:c, :]
        kd = (k * jnp.exp(b_last - b)).astype(BF16)
        e_last = jnp.exp(b_last)
        gate = g_ref[rows, :].astype(F32)
        ng = ng_ref[...]
        for hd in range(hps):
            cols = slice(hd * LANES, (hd + 1) * LANES)
            st = st_ref[hd]
            o = _dot_nt(qe[:, cols], st.astype(BF16)) + _dot(att[hd].astype(BF16), vb[:, cols])
            og_ref[rows, cols] = _hgrn_norm_gate(o, ng[:, cols], gate[:, cols])
            st_ref[hd] = st * e_last[:, cols] + _dot_tn(vb[:, cols], kd[:, cols])
        return carry

    lax.fori_loop(0, nblk, body, 0, unroll=2)
    for hd in range(hps):
        sfin_ref[hd] = st_ref[hd].T


def _hgrn_prompt(f, qig, norm_g, batch, seq, heads, t):
    hps = HGRN_HEADS_PER_STEP
    w = hps * LANES
    ngrp = heads // hps
    nblk = seq // HGRN_BLOCK
    nl = len(_hgrn_halves())
    part = lambda p: pl.BlockSpec((seq, w), lambda s, h, p=p: (s, p * ngrp + h))
    return pl.pallas_call(
        functools.partial(_hgrn_prompt_kernel, nblk=nblk, hps=hps),
        grid=(batch, ngrp),
        in_specs=[part(0), part(0), part(1), part(2),
                  pl.BlockSpec((1, w), lambda s, h: (0, h))],
        out_specs=[pl.BlockSpec((seq, w), lambda s, h: (s, h)),
                   pl.BlockSpec((None, hps, LANES, LANES), lambda s, h: (s, h, 0, 0))],
        out_shape=[jax.ShapeDtypeStruct((t, heads * LANES), BF16),
                   jax.ShapeDtypeStruct((batch, heads, LANES, LANES), F32)],
        scratch_shapes=[pltpu.VMEM((hps, LANES, LANES), F32),
                        pltpu.VMEM((HGRN_BLOCK, HGRN_BLOCK), BF16),
                        pltpu.VMEM((nl, HGRN_BLOCK, LANES), F32),
                        pltpu.VMEM((nl + 1, HGRN_BLOCK, HGRN_BLOCK), F32)],
        compiler_params=_cparams(("arbitrary", "arbitrary")),
        name="hgrn_prompt",
    )(f, qig, qig, qig, norm_g)


def _hgrn_sample_kernel(f_ref, q_ref, i_ref, g_ref, ng_ref, s0_ref, og_in_ref, og_ref, s1_ref,
                        *, heads, dec_seq):
    del og_in_ref
    rows_per_tile = SUBLANES
    per_tile = rows_per_tile // dec_seq
    row = lax.broadcasted_iota(jnp.int32, (rows_per_tile, LANES), 0)
    pos = row % dec_seq
    first = row < dec_seq
    eye_c = lax.broadcasted_iota(jnp.int32, (LANES, LANES), 1)
    pad = jnp.zeros((LANES - 2 * rows_per_tile, LANES), F32)
    vpad = jnp.zeros((LANES - rows_per_tile, LANES), F32)
    zrow = jnp.zeros((rows_per_tile - per_tile, LANES), F32)
    f_all = f_ref[...]
    q_all = q_ref[...].astype(F32)
    v_all = i_ref[...].astype(F32)
    g_all = g_ref[...].astype(F32)
    ng = ng_ref[...]

    def pick(x, s):
        return jnp.where(first, x[s:s + 1, :], x[dec_seq + s:dec_seq + s + 1, :])

    for tile in range(SAMPLE_BATCH_TILE // per_tile):
        rows = slice(tile * rows_per_tile, (tile + 1) * rows_per_tile)
        for h in range(heads):
            cols = slice(h * LANES, (h + 1) * LANES)
            q = q_all[rows, cols]
            f = f_all[rows, cols]
            v = v_all[rows, cols]
            k = 1.0 - f
            b = jnp.log(f)
            shift = 1
            while shift < dec_seq:
                b = b + jnp.where(pos >= shift, pltpu.roll(b, shift, 0), 0.0)
                shift *= 2
            qe = (q * jnp.exp(b)).astype(BF16)
            s_a = s0_ref[tile * per_tile, h]
            s_b = s0_ref[tile * per_tile + 1, h]
            o = jnp.where(first, _dot(qe, s_a.astype(BF16)), _dot(qe, s_b.astype(BF16)))
            for s in range(dec_seq):
                e = jnp.exp(jnp.minimum(b - pick(b, s), 0.0))
                wgt = jnp.sum(q * e * pick(k, s), axis=-1, keepdims=True)
                o = o + jnp.where(pos >= s, wgt, 0.0) * pick(v, s)
            og_ref[rows, cols] = _hgrn_norm_gate(o, ng[:, cols], g_all[rows, cols])

            b_last = pick(b, dec_seq - 1)
            kd = k * jnp.exp(b_last - b)
            e_rows = jnp.concatenate([jnp.exp(b[dec_seq - 1:dec_seq, :]),
                                      jnp.exp(b[2 * dec_seq - 1:2 * dec_seq, :]), zrow], axis=0)
            cols_t = jnp.concatenate([kd, e_rows, pad], axis=0).T
            kd_t = cols_t.astype(BF16)
            v_a = jnp.concatenate([jnp.where(first, v, 0.0), vpad], axis=0).astype(BF16)
            v_b = jnp.concatenate([jnp.where(first, 0.0, v), vpad], axis=0).astype(BF16)
            e_a = jnp.sum(jnp.where(eye_c == rows_per_tile, cols_t, 0.0), axis=-1, keepdims=True)
            e_b = jnp.sum(jnp.where(eye_c == rows_per_tile + 1, cols_t, 0.0), axis=-1, keepdims=True)
            s1_ref[tile * per_tile, h] = s_a * e_a + _dot(kd_t, v_a)
            s1_ref[tile * per_tile + 1, h] = s_b * e_b + _dot(kd_t, v_b)


def _hgrn_sample(f, qig, norm_g, s0, og, n_prompt, dec_batch, dec_seq, heads):
    rows = SAMPLE_BATCH_TILE * dec_seq
    first_blk = n_prompt // rows
    d_b = heads * LANES
    part = lambda p: pl.BlockSpec((rows, d_b), lambda i, p=p: (first_blk + i, p))
    return pl.pallas_call(
        functools.partial(_hgrn_sample_kernel, heads=heads, dec_seq=dec_seq),
        grid=(dec_batch // SAMPLE_BATCH_TILE,),
        in_specs=[part(0), part(0), part(1), part(2),
                  _resident((1, d_b), lambda i: (0, 0)),
                  pl.BlockSpec((SAMPLE_BATCH_TILE, heads, LANES, LANES), lambda i: (i, 0, 0, 0)),
                  pl.BlockSpec(memory_space=pl.ANY)],
        out_specs=[pl.BlockSpec((rows, d_b), lambda i: (first_blk + i, 0)),
                   pl.BlockSpec((SAMPLE_BATCH_TILE, heads, LANES, LANES), lambda i: (i, 0, 0, 0))],
        out_shape=[jax.ShapeDtypeStruct(og.shape, og.dtype),
                   jax.ShapeDtypeStruct(s0.shape, F32)],
        input_output_aliases={6: 0},
        compiler_params=_cparams(("parallel",)),
        name="hgrn_sample",
    )(f, qig, qig, qig, norm_g, s0, og)


def _out_proj_kernel(h_ref, x_ref, w_ref, o_ref):
    o_ref[...] = h_ref[...] + _dot(x_ref[...], w_ref[...])


def _out_proj(h, x, w, j_layer):
    t, d = h.shape
    kdim = x.shape[1]
    return pl.pallas_call(
        _out_proj_kernel,
        grid=(t // TM,),
        in_specs=[
            pl.BlockSpec((TM, d), lambda i: (i, 0)),
            pl.BlockSpec((TM, kdim), lambda i: (i, 0)),
            _resident((None, kdim, d), lambda i: (j_layer, 0, 0)),
        ],
        out_specs=pl.BlockSpec((TM, d), lambda i: (i, 0)),
        out_shape=jax.ShapeDtypeStruct((t, d), F32),
        compiler_params=_cparams(("parallel",)),
        name="hgrn_out_proj",
    )(h, x, w)


def kernel(x_prompt, x_sample, state_hgrn, p_prompt, p_sample, ffn1_w_in, ffn1_w_out, ffn2_w_in, ffn2_w_out, norm_ffn1, norm_mix, norm_ffn2, norm_ple, a_w_in, a_ln_g, a_ln_b, a_w_s, a_b_s, a_w_out, b_w_in, b_lb_logits, b_norm_g, b_w_out, ple_w_proj, ple_w_gate, final_norm):
    batch, seq, d = x_prompt.shape
    dec_batch, dec_seq, _ = x_sample.shape
    depth = ffn1_w_in.shape[0]
    n_prompt = batch * seq
    n_sample = dec_batch * dec_seq
    t = n_prompt + n_sample
    heads = b_norm_g.shape[-1] // LANES
    assert n_prompt % TM == 0 and n_sample % TM == 0 and t % TM_STREAM == 0
    assert seq % HGRN_BLOCK == 0 and heads % HGRN_HEADS_PER_STEP == 0
    assert CHUNK_A % dec_seq == 0 and SUBLANES % dec_seq == 0 and dec_batch % SAMPLE_BATCH_TILE == 0
    assert a_w_s.shape[-1] == CHUNK_A and seq % CHUNK_A == 0

    h = jnp.concatenate([x_prompt.reshape(n_prompt, d), x_sample.reshape(n_sample, d)], axis=0)
    p = jnp.concatenate([p_prompt.reshape(depth, n_prompt, -1), p_sample.reshape(depth, n_sample, -1)], axis=1)

    bf = lambda w: w.astype(BF16)
    vec = lambda g: g.reshape(g.shape[0], 1, g.shape[-1])
    a_w_in_b, a_w_out_b, b_w_out_b = bf(a_w_in), bf(a_w_out), bf(b_w_out)
    ple_w_proj_b, ple_w_gate_b = bf(ple_w_proj), bf(ple_w_gate)
    n_ffn1, n_mix, n_ffn2, n_ple = vec(norm_ffn1), vec(norm_mix), vec(norm_ffn2), vec(norm_ple)
    ln_g, ln_b = vec(a_ln_g), vec(a_ln_b)
    final_g = final_norm.reshape(1, d)

    reps = CHUNK_A // dec_seq
    tri = jnp.tril(jnp.ones((CHUNK_A, CHUNK_A), bool))
    pos_c = jnp.arange(CHUNK_A) // dec_seq
    blockdiag = pos_c[:, None] == pos_c[None, :]
    w_prompt = jnp.where(tri, a_w_s, 0.0)
    w_sample = jnp.where(tri & blockdiag, jnp.tile(a_w_s[:, :, :dec_seq, :dec_seq], (1, 1, reps, reps)), 0.0)
    wmix = jnp.stack([w_prompt, w_sample], axis=1).astype(BF16)
    b_prompt = jnp.swapaxes(a_b_s, 1, 2)
    b_sample = jnp.tile(b_prompt[:, :dec_seq], (1, reps, 1))
    bias = jnp.repeat(jnp.stack([b_prompt, b_sample], axis=1), LANES, axis=-1)

    new_v, s_prompt, s_sample = [], [], []
    for li in range(depth):
        h = _ffn(h, n_ffn1, ffn1_w_in, ffn1_w_out, li)
        j = li // 2
        if li % 2 == 0:
            u = _gmlp_proj(h, n_mix, a_w_in_b, ln_g, ln_b, li, j, half=0, layer_norm=False, out_dtype=BF16)
            v = _gmlp_proj(h, n_mix, a_w_in_b, ln_g, ln_b, li, j, half=1, layer_norm=True, out_dtype=F32)
            new_v.append(v[n_prompt:].reshape(dec_batch, dec_seq, -1))
            h = _gmlp_mix(h, u, v, wmix[j], bias[j], a_w_out_b, j, n_prompt // TM)
        else:
            f = _hgrn_proj(h, n_mix, b_w_in, b_lb_logits, li, j, forget=True)
            qig = _hgrn_proj(h, n_mix, b_w_in, b_lb_logits, li, j, forget=False)
            ng = b_norm_g[j].reshape(1, heads * LANES)
            og, s_fin = _hgrn_prompt(f, qig, ng, batch, seq, heads, t)
            og, s_new = _hgrn_sample(f, qig, ng, state_hgrn[j], og, n_prompt, dec_batch, dec_seq, heads)
            s_prompt.append(s_fin)
            s_sample.append(s_new)
            h = _out_proj(h, og, b_w_out_b, j)
        h = _ffn(h, n_ffn2, ffn2_w_in, ffn2_w_out, li)
        h = _ple(h, p, n_ple, ple_w_gate_b, ple_w_proj_b, final_g, li, final=(li == depth - 1))

    y_prompt = h[:n_prompt].reshape(batch, seq, d)
    y_sample = h[n_prompt:].reshape(dec_batch, dec_seq, d)
    return (y_prompt, y_sample, jnp.stack(s_prompt), jnp.stack(s_sample), jnp.stack(new_v))
```

```python
import functools

import jax
import jax.numpy as jnp
from jax import lax
from jax.experimental import pallas as pl
from jax.experimental.pallas import tpu as pltpu

F32 = jnp.float32
BF16 = jnp.bfloat16
EPS = 1e-6

LANES = 128
SUBLANES = 8
VMEM_LIMIT_BYTES = 56 * 1024 * 1024

TM = 512
TM_STREAM = 1088
TF = 512
TN = 512
HGRN_BLOCK = 128
HGRN_HEADS_PER_STEP = 4
CHUNK_A = 128
SAMPLE_BATCH_TILE = 4


def _cparams(sem):
    return pltpu.CompilerParams(dimension_semantics=sem, vmem_limit_bytes=VMEM_LIMIT_BYTES)


def _rms(x, g):
    ms = jnp.mean(x * x, axis=-1, keepdims=True)
    return x * lax.rsqrt(ms + EPS) * g


def _resident(shape, index_map):
    return pl.BlockSpec(shape, index_map, pipeline_mode=pl.Buffered(1))


def _dot(a, b):
    return jnp.dot(a, b, preferred_element_type=F32)


def _dot_nt(a, b):
    return lax.dot_general(a, b, (((1,), (1,)), ((), ())), preferred_element_type=F32)


def _dot_tn(a, b):
    return lax.dot_general(a, b, (((0,), (0,)), ((), ())), preferred_element_type=F32)


def _ffn_kernel(xa_ref, xb_ref, g_ref, wg_ref, wu_ref, wo_ref, o_ref, xn_ref, sem, *, n_a, n_b):
    tm = o_ref.shape[0]
    full_a, rem_a = divmod(n_a, tm)

    @pl.when(pl.program_id(1) == 0)
    def _():
        i = pl.program_id(0)

        @pl.when(i < full_a)
        def _():
            cp = pltpu.make_async_copy(xa_ref.at[pl.ds(i * tm, tm)], o_ref, sem.at[0])
            cp.start()
            cp.wait()

        if n_b:
            @pl.when(i == full_a)
            def _():
                copies = []
                if rem_a:
                    copies.append(pltpu.make_async_copy(
                        xa_ref.at[pl.ds(full_a * tm, rem_a)], o_ref.at[pl.ds(0, rem_a)], sem.at[0]))
                copies.append(pltpu.make_async_copy(xb_ref, o_ref.at[pl.ds(rem_a, n_b)], sem.at[1]))
                for cp in copies:
                    cp.start()
                for cp in copies:
                    cp.wait()

        xn_ref[...] = _rms(o_ref[...], g_ref[...]).astype(BF16)

    xn = xn_ref[...]
    gate = _dot(xn, wg_ref[...].astype(BF16))
    up = _dot(xn, wu_ref[...].astype(BF16))
    act = (0.5 * gate * jax.nn.sigmoid(gate) * up).astype(BF16)
    o_ref[...] += _dot(act, wo_ref[...].astype(BF16))


def _ffn(xa, xb, norm_g, w_in, w_out, li):
    n_a, d = xa.shape
    n_b = 0 if xb is None else xb.shape[0]
    t = n_a + n_b
    assert t % TM_STREAM == 0 and (n_b == 0 or n_a % TM_STREAM + n_b == TM_STREAM)
    d_ff = w_out.shape[1]
    nf = d_ff // TF
    return pl.pallas_call(
        functools.partial(_ffn_kernel, n_a=n_a, n_b=n_b),
        grid=(t // TM_STREAM, nf),
        in_specs=[
            pl.BlockSpec(memory_space=pl.ANY),
            pl.BlockSpec(memory_space=pl.ANY),
            pl.BlockSpec((None, 1, d), lambda i, j: (li, 0, 0)),
            pl.BlockSpec((None, d, TF), lambda i, j: (li, 0, j)),
            pl.BlockSpec((None, d, TF), lambda i, j: (li, 0, j + nf)),
            pl.BlockSpec((None, TF, d), lambda i, j: (li, j, 0)),
        ],
        out_specs=pl.BlockSpec((TM_STREAM, d), lambda i, j: (i, 0)),
        out_shape=jax.ShapeDtypeStruct((t, d), F32),
        scratch_shapes=[pltpu.VMEM((TM_STREAM, d), BF16), pltpu.SemaphoreType.DMA((2,))],
        compiler_params=_cparams(("arbitrary", "arbitrary")),
        name=f"ffn_l{li}",
    )(xa, xa if xb is None else xb, norm_g, w_in, w_in, w_out)


def _ple_kernel(h_ref, pa_ref, pb_ref, g_ref, wgate_ref, wproj_ref, fn_ref, *o_refs, final, na):
    i = pl.program_id(0)
    h = h_ref[...]
    hn = _rms(h, g_ref[...]).astype(BF16)
    gate = jax.nn.sigmoid(_dot(hn, wgate_ref[...]))
    p = jnp.where(i < na, pa_ref[...], pb_ref[...])
    proj = _dot(p.astype(BF16), wproj_ref[...])
    out = h + proj * gate
    if not final:
        o_refs[0][...] = out
        return
    out = _rms(out, fn_ref[...])

    @pl.when(i < na)
    def _():
        o_refs[0][...] = out

    @pl.when(i >= na)
    def _():
        o_refs[1][...] = out


def _ple(h, pa, pb, norm_g, w_gate, w_proj, final_g, li, final):
    t, d = h.shape
    dp = pa.shape[-1]
    na = pa.shape[1] // TM
    in_a = lambda i: jnp.minimum(i, na - 1)
    in_b = lambda i: jnp.maximum(i - na, 0)
    if final:
        out_specs = [pl.BlockSpec((TM, d), lambda i: (in_a(i), 0)), pl.BlockSpec((TM, d), lambda i: (in_b(i), 0))]
        out_shape = [jax.ShapeDtypeStruct((pa.shape[1], d), F32), jax.ShapeDtypeStruct((pb.shape[1], d), F32)]
    else:
        out_specs = pl.BlockSpec((TM, d), lambda i: (i, 0))
        out_shape = jax.ShapeDtypeStruct((t, d), F32)
    return pl.pallas_call(
        functools.partial(_ple_kernel, final=final, na=na),
        grid=(t // TM,),
        in_specs=[
            pl.BlockSpec((TM, d), lambda i: (i, 0)),
            pl.BlockSpec((None, TM, dp), lambda i: (li, in_a(i), 0)),
            pl.BlockSpec((None, TM, dp), lambda i: (li, in_b(i), 0)),
            _resident((None, 1, d), lambda i: (li, 0, 0)),
            _resident((None, d, d), lambda i: (li, 0, 0)),
            _resident((None, dp, d), lambda i: (li, 0, 0)),
            _resident((1, d), lambda i: (0, 0)),
        ],
        out_specs=out_specs,
        out_shape=out_shape,
        compiler_params=_cparams(("arbitrary",)),
        name=f"ple_l{li}",
    )(h, pa, pb, norm_g, w_gate, w_proj, final_g)


def _gelu(z):
    return 0.5 * z * (1.0 + lax.erf(z * (2.0 ** -0.5)))


def _gmlp_proj_kernel(x_ref, g_ref, w_ref, lng_ref, lnb_ref, o_ref, xn_ref, *, layer_norm):
    @pl.when(pl.program_id(1) == 0)
    def _():
        xn_ref[...] = _rms(x_ref[...], g_ref[...]).astype(BF16)

    z = _gelu(_dot(xn_ref[...], w_ref[...]))
    if layer_norm:
        mu = jnp.mean(z, axis=-1, keepdims=True)
        zc = z - mu
        var = jnp.mean(zc * zc, axis=-1, keepdims=True)
        z = zc * lax.rsqrt(var + EPS) * lng_ref[...] + lnb_ref[...]
    o_ref[...] = z.astype(o_ref.dtype)


def _gmlp_proj(h, norm_g, w_in, ln_g, ln_b, li, j_layer, *, half, layer_norm, out_dtype):
    t, d = h.shape
    d_a = w_in.shape[-1] // 2
    tn = d_a
    nn = d_a // tn
    off = half * nn
    return pl.pallas_call(
        functools.partial(_gmlp_proj_kernel, layer_norm=layer_norm),
        grid=(t // TM, nn),
        in_specs=[
            pl.BlockSpec((TM, d), lambda i, j: (i, 0)),
            pl.BlockSpec((None, 1, d), lambda i, j: (li, 0, 0)),
            pl.BlockSpec((None, d, tn), lambda i, j: (j_layer, 0, j + off)),
            pl.BlockSpec((None, 1, tn), lambda i, j: (j_layer, 0, j)),
            pl.BlockSpec((None, 1, tn), lambda i, j: (j_layer, 0, j)),
        ],
        out_specs=pl.BlockSpec((TM, tn), lambda i, j: (i, j)),
        out_shape=jax.ShapeDtypeStruct((t, d_a), out_dtype),
        scratch_shapes=[pltpu.VMEM((TM, d), BF16)],
        compiler_params=_cparams(("parallel", "arbitrary")),
        name=f"gmlp_proj_l{li}_{'v' if half else 'u'}",
    )(h, norm_g, w_in, ln_g, ln_b)


def _gmlp_mix_kernel(h_ref, u_ref, v_ref, wmix_ref, bias_ref, wo_ref, o_ref, t_ref, *, groups):
    for c in range(TM // CHUNK_A):
        rows = slice(c * CHUNK_A, (c + 1) * CHUNK_A)
        for g in range(groups):
            cols = slice(g * LANES, (g + 1) * LANES)
            s = _dot(wmix_ref[g], v_ref[rows, cols].astype(BF16)) + bias_ref[:, cols]
            t_ref[rows, cols] = (u_ref[rows, cols].astype(F32) * s).astype(BF16)
    o_ref[...] = h_ref[...] + _dot(t_ref[...], wo_ref[...])


def _gmlp_mix(h, u, v, wmix, bias, w_out, j_layer, n_prompt_tiles):
    t, d = h.shape
    d_a = u.shape[1]
    groups = d_a // LANES
    return pl.pallas_call(
        functools.partial(_gmlp_mix_kernel, groups=groups),
        grid=(t // TM,),
        in_specs=[
            pl.BlockSpec((TM, d), lambda i: (i, 0)),
            pl.BlockSpec((TM, d_a), lambda i: (i, 0)),
            pl.BlockSpec((TM, d_a), lambda i: (i, 0)),
            pl.BlockSpec((None, groups, CHUNK_A, CHUNK_A), lambda i: (i // n_prompt_tiles, 0, 0, 0)),
            pl.BlockSpec((None, CHUNK_A, d_a), lambda i: (i // n_prompt_tiles, 0, 0)),
            _resident((None, d_a, d), lambda i: (j_layer, 0, 0)),
        ],
        out_specs=pl.BlockSpec((TM, d), lambda i: (i, 0)),
        out_shape=jax.ShapeDtypeStruct((t, d), F32),
        scratch_shapes=[pltpu.VMEM((TM, d_a), BF16)],
        compiler_params=_cparams(("parallel",)),
        name="gmlp_mix",
    )(h, u, v, wmix, bias, w_out)


def _hgrn_proj_kernel(x_ref, g_ref, w_ref, lbl_ref, f_ref, qig_ref, xn_ref, *, li, per_part):
    j = pl.program_id(1)
    part = j // per_part

    @pl.when(j == 0)
    def _():
        xn_ref[...] = _rms(x_ref[...], g_ref[...]).astype(BF16)

    z = _dot(xn_ref[...], w_ref[...].astype(BF16))
    sig = jax.nn.sigmoid(z)

    @pl.when(part == 1)
    def _():
        logits = lbl_ref[...]
        ex = jnp.exp(logits - jnp.max(logits, axis=0, keepdims=True))
        sm = ex / jnp.sum(ex, axis=0, keepdims=True)
        lb = jnp.sum(sm[: li + 1], axis=0, keepdims=True) - sm[0:1]
        f_ref[...] = lb + (1.0 - lb) * sig

    @pl.when(part != 1)
    def _():
        qig_ref[...] = jnp.where(part == 2, z, z * sig).astype(BF16)


def _hgrn_proj(h, norm_g, w_in, lb_logits, li, j_layer):
    t, d = h.shape
    d_b = w_in.shape[-1] // 4
    per_part = d_b // TN
    f_col = lambda j: jnp.clip(j - per_part, 0, per_part - 1)
    qig_col = lambda j: jnp.where(j < per_part, j, jnp.maximum(j - per_part, per_part - 1))
    return pl.pallas_call(
        functools.partial(_hgrn_proj_kernel, li=li, per_part=per_part),
        grid=(t // TM_STREAM, 4 * per_part),
        in_specs=[
            pl.BlockSpec((TM_STREAM, d), lambda i, j: (i, 0)),
            pl.BlockSpec((None, 1, d), lambda i, j: (li, 0, 0)),
            pl.BlockSpec((None, d, TN), lambda i, j: (j_layer, 0, j)),
            pl.BlockSpec((lb_logits.shape[0], TN), lambda i, j: (0, j % per_part)),
        ],
        out_specs=[pl.BlockSpec((TM_STREAM, TN), lambda i, j: (i, f_col(j))),
                   pl.BlockSpec((TM_STREAM, TN), lambda i, j: (i, qig_col(j)))],
        out_shape=[jax.ShapeDtypeStruct((t, d_b), F32), jax.ShapeDtypeStruct((t, 3 * d_b), BF16)],
        scratch_shapes=[pltpu.VMEM((TM_STREAM, d), BF16)],
        compiler_params=_cparams(("arbitrary", "arbitrary")),
        name="hgrn_proj",
    )(h, norm_g, w_in, lb_logits)


def _hgrn_halves():
    halves, half = [], HGRN_BLOCK // 2
    while half >= 1:
        halves.append(half)
        half //= 2
    return tuple(halves)


def _hgrn_norm_gate(o, ng, gate):
    ms = jnp.mean(o * o, axis=-1, keepdims=True)
    return (o * lax.rsqrt(ms + EPS) * ng * gate).astype(BF16)


def _hgrn_prompt_kernel(f_ref, q_ref, i_ref, g_ref, ng_ref, og_ref, sfin_ref,
                        st_ref, tril_ref, sgn_ref, mask_ref, *, nblk, hps):
    c = HGRN_BLOCK
    w = hps * LANES
    halves = _hgrn_halves()
    nl = len(halves)

    @pl.when((pl.program_id(0) == 0) & (pl.program_id(1) == 0))
    def _():
        r = lax.broadcasted_iota(jnp.int32, (c, c), 0)
        s = lax.broadcasted_iota(jnp.int32, (c, c), 1)
        tril_ref[...] = jnp.where(s <= r, 1.0, 0.0).astype(BF16)
        for lvl, half in enumerate(halves):
            par = 2 * half
            second = (r & half) != 0
            sgn_ref[lvl] = jnp.where(second, 1.0, -1.0)
            valid = second & ((s & half) == 0) & ((r // par) == (s // par))
            mask_ref[lvl] = jnp.where(valid, 1.0, 0.0)
        mask_ref[nl] = jnp.where(r == s, 1.0, 0.0)

    st_ref[...] = jnp.zeros_like(st_ref)
    low_sub = lax.broadcasted_iota(jnp.int32, (c // SUBLANES, SUBLANES, w), 1) < SUBLANES // 2

    def body(blk, carry):
        rows = pl.ds(pl.multiple_of(blk * c, c), c)
        f = f_ref[rows, :]
        qb = q_ref[rows, :]
        vb = i_ref[rows, :]
        q = qb.astype(F32)
        k = 1.0 - f
        kb = k.astype(BF16)

        x = jnp.log(f)
        hi = x.astype(BF16)
        r1 = x - hi.astype(F32)
        mid = r1.astype(BF16)
        lo = (r1 - mid.astype(F32)).astype(BF16)
        tril = tril_ref[...]
        b = _dot(tril, hi) + _dot(tril, mid) + _dot(tril, lo)
        b3 = b.reshape(c // SUBLANES, SUBLANES, w)

        att = [None] * hps

        def add_level(lvl, qt, kt):
            for hd in range(hps):
                cols = slice(hd * LANES, (hd + 1) * LANES)
                a = _dot_nt(qt[:, cols], kt[:, cols]) * mask_ref[lvl]
                att[hd] = a if att[hd] is None else att[hd] + a

        for lvl, half in enumerate(halves):
            par = 2 * half
            if half == 1:
                add_level(lvl, (q * f).astype(BF16), kb)
                continue
            if half >= SUBLANES:
                pieces = [jnp.broadcast_to(b[p * par + half - 1:p * par + half, :], (par, w))
                          for p in range(c // par)]
                beta = pieces[0] if len(pieces) == 1 else jnp.concatenate(pieces, axis=0)
            elif half == SUBLANES // 2:
                beta = jnp.broadcast_to(b3[:, half - 1:half, :], b3.shape).reshape(c, w)
            else:
                beta = jnp.where(low_sub, b3[:, half - 1:half, :],
                                 b3[:, par + half - 1:par + half, :]).reshape(c, w)
            sgn = jnp.concatenate([sgn_ref[lvl]] * hps, axis=1)
            e = jnp.exp((b - beta) * sgn)
            add_level(lvl, (q * e).astype(BF16), (k * e).astype(BF16))
        add_level(nl, qb, kb)

        qe = (q * jnp.exp(b)).astype(BF16)
        b_last = b[c - 1:c, :]
        kd = (k * jnp.exp(b_last - b)).astype(BF16)
        e_last = jnp.exp(b_last)
        gate = g_ref[rows, :].astype(F32)
        ng = ng_ref[...]
        for hd in range(hps):
            cols = slice(hd * LANES, (hd + 1) * LANES)
            st = st_ref[hd]
            o = _dot_nt(qe[:, cols], st.astype(BF16)) + _dot(att[hd].astype(BF16), vb[:, cols])
            og_ref[rows, cols] = _hgrn_norm_gate(o, ng[:, cols], gate[:, cols])
            st_ref[hd] = st * e_last[:, cols] + _dot_tn(vb[:, cols], kd[:, cols])
        return carry

    lax.fori_loop(0, nblk, body, 0, unroll=2)
    for hd in range(hps):
        sfin_ref[hd] = st_ref[hd].T


def _hgrn_prompt(f, qig, norm_g, batch, seq, heads):
    hps = HGRN_HEADS_PER_STEP
    w = hps * LANES
    ngrp = heads // hps
    nblk = seq // HGRN_BLOCK
    nl = len(_hgrn_halves())
    part = lambda p: pl.BlockSpec((seq, w), lambda s, h, p=p: (s, p * ngrp + h))
    return pl.pallas_call(
        functools.partial(_hgrn_prompt_kernel, nblk=nblk, hps=hps),
        grid=(batch, ngrp),
        in_specs=[part(0), part(0), part(1), part(2),
                  pl.BlockSpec((1, w), lambda s, h: (0, h))],
        out_specs=[pl.BlockSpec((seq, w), lambda s, h: (s, h)),
                   pl.BlockSpec((None, hps, LANES, LANES), lambda s, h: (s, h, 0, 0))],
        out_shape=[jax.ShapeDtypeStruct((batch * seq, heads * LANES), BF16),
                   jax.ShapeDtypeStruct((batch, heads, LANES, LANES), F32)],
        scratch_shapes=[pltpu.VMEM((hps, LANES, LANES), F32),
                        pltpu.VMEM((HGRN_BLOCK, HGRN_BLOCK), BF16),
                        pltpu.VMEM((nl, HGRN_BLOCK, LANES), F32),
                        pltpu.VMEM((nl + 1, HGRN_BLOCK, HGRN_BLOCK), F32)],
        compiler_params=_cparams(("arbitrary", "arbitrary")),
        name="hgrn_prompt",
    )(f, qig, qig, qig, norm_g)


def _hgrn_sample_kernel(f_ref, q_ref, i_ref, g_ref, ng_ref, s0_ref, og_ref, s1_ref, *, heads, dec_seq):
    rows_per_tile = SUBLANES
    per_tile = rows_per_tile // dec_seq
    row = lax.broadcasted_iota(jnp.int32, (rows_per_tile, LANES), 0)
    pos = row % dec_seq
    first = row < dec_seq
    eye_c = lax.broadcasted_iota(jnp.int32, (LANES, LANES), 1)
    pad = jnp.zeros((LANES - 2 * rows_per_tile, LANES), F32)
    vpad = jnp.zeros((LANES - rows_per_tile, LANES), F32)
    zrow = jnp.zeros((rows_per_tile - per_tile, LANES), F32)
    f_all = f_ref[...]
    q_all = q_ref[...].astype(F32)
    v_all = i_ref[...].astype(F32)
    g_all = g_ref[...].astype(F32)
    ng = ng_ref[...]

    def pick(x, s):
        return jnp.where(first, x[s:s + 1, :], x[dec_seq + s:dec_seq + s + 1, :])

    for tile in range(SAMPLE_BATCH_TILE // per_tile):
        rows = slice(tile * rows_per_tile, (tile + 1) * rows_per_tile)
        for h in range(heads):
            cols = slice(h * LANES, (h + 1) * LANES)
            q = q_all[rows, cols]
            f = f_all[rows, cols]
            v = v_all[rows, cols]
            k = 1.0 - f
            b = jnp.log(f)
            shift = 1
            while shift < dec_seq:
                b = b + jnp.where(pos >= shift, pltpu.roll(b, shift, 0), 0.0)
                shift *= 2
            qe = (q * jnp.exp(b)).astype(BF16)
            s_a = s0_ref[tile * per_tile, h]
            s_b = s0_ref[tile * per_tile + 1, h]
            o = jnp.where(first, _dot(qe, s_a.astype(BF16)), _dot(qe, s_b.astype(BF16)))
            for s in range(dec_seq):
                e = jnp.exp(jnp.minimum(b - pick(b, s), 0.0))
                wgt = jnp.sum(q * e * pick(k, s), axis=-1, keepdims=True)
                o = o + jnp.where(pos >= s, wgt, 0.0) * pick(v, s)
            og_ref[rows, cols] = _hgrn_norm_gate(o, ng[:, cols], g_all[rows, cols])

            b_last = pick(b, dec_seq - 1)
            kd = k * jnp.exp(b_last - b)
            e_rows = jnp.concatenate([jnp.exp(b[dec_seq - 1:dec_seq, :]),
                                      jnp.exp(b[2 * dec_seq - 1:2 * dec_seq, :]), zrow], axis=0)
            cols_t = jnp.concatenate([kd, e_rows, pad], axis=0).T
            kd_t = cols_t.astype(BF16)
            v_a = jnp.concatenate([jnp.where(first, v, 0.0), vpad], axis=0).astype(BF16)
            v_b = jnp.concatenate([jnp.where(first, 0.0, v), vpad], axis=0).astype(BF16)
            e_a = jnp.sum(jnp.where(eye_c == rows_per_tile, cols_t, 0.0), axis=-1, keepdims=True)
            e_b = jnp.sum(jnp.where(eye_c == rows_per_tile + 1, cols_t, 0.0), axis=-1, keepdims=True)
            s1_ref[tile * per_tile, h] = s_a * e_a + _dot(kd_t, v_a)
            s1_ref[tile * per_tile + 1, h] = s_b * e_b + _dot(kd_t, v_b)


def _hgrn_sample(f, qig, norm_g, s0, n_prompt, dec_batch, dec_seq, heads):
    rows = SAMPLE_BATCH_TILE * dec_seq
    first_blk = n_prompt // rows
    d_b = heads * LANES
    part = lambda p: pl.BlockSpec((rows, d_b), lambda i, p=p: (first_blk + i, p))
    return pl.pallas_call(
        functools.partial(_hgrn_sample_kernel, heads=heads, dec_seq=dec_seq),
        grid=(dec_batch // SAMPLE_BATCH_TILE,),
        in_specs=[part(0), part(0), part(1), part(2),
                  _resident((1, d_b), lambda i: (0, 0)),
                  pl.BlockSpec((SAMPLE_BATCH_TILE, heads, LANES, LANES), lambda i: (i, 0, 0, 0))],
        out_specs=[pl.BlockSpec((rows, d_b), lambda i: (i, 0)),
                   pl.BlockSpec((SAMPLE_BATCH_TILE, heads, LANES, LANES), lambda i: (i, 0, 0, 0))],
        out_shape=[jax.ShapeDtypeStruct((dec_batch * dec_seq, d_b), BF16),
                   jax.ShapeDtypeStruct(s0.shape, F32)],
        compiler_params=_cparams(("parallel",)),
        name="hgrn_sample",
    )(f, qig, qig, qig, norm_g, s0)


def _out_proj_kernel(h_ref, xa_ref, xb_ref, w_ref, o_ref, *, na):
    x = jnp.where(pl.program_id(0) < na, xa_ref[...], xb_ref[...])
    o_ref[...] = h_ref[...] + _dot(x, w_ref[...])


def _out_proj(h, xa, xb, w, j_layer):
    t, d = h.shape
    kdim = xa.shape[1]
    na = xa.shape[0] // TM
    return pl.pallas_call(
        functools.partial(_out_proj_kernel, na=na),
        grid=(t // TM,),
        in_specs=[
            pl.BlockSpec((TM, d), lambda i: (i, 0)),
            pl.BlockSpec((TM, kdim), lambda i: (jnp.minimum(i, na - 1), 0)),
            pl.BlockSpec((TM, kdim), lambda i: (jnp.maximum(i - na, 0), 0)),
            _resident((None, kdim, d), lambda i: (j_layer, 0, 0)),
        ],
        out_specs=pl.BlockSpec((TM, d), lambda i: (i, 0)),
        out_shape=jax.ShapeDtypeStruct((t, d), F32),
        compiler_params=_cparams(("parallel",)),
        name="hgrn_out_proj",
    )(h, xa, xb, w)


def kernel(x_prompt, x_sample, state_hgrn, p_prompt, p_sample, ffn1_w_in, ffn1_w_out, ffn2_w_in, ffn2_w_out, norm_ffn1, norm_mix, norm_ffn2, norm_ple, a_w_in, a_ln_g, a_ln_b, a_w_s, a_b_s, a_w_out, b_w_in, b_lb_logits, b_norm_g, b_w_out, ple_w_proj, ple_w_gate, final_norm):
    batch, seq, d = x_prompt.shape
    dec_batch, dec_seq, _ = x_sample.shape
    depth = ffn1_w_in.shape[0]
    n_prompt = batch * seq
    n_sample = dec_batch * dec_seq
    t = n_prompt + n_sample
    heads = b_norm_g.shape[-1] // LANES
    assert n_prompt % TM == 0 and n_sample % TM == 0 and t % TM_STREAM == 0
    assert seq % HGRN_BLOCK == 0 and heads % HGRN_HEADS_PER_STEP == 0
    assert CHUNK_A % dec_seq == 0 and SUBLANES % dec_seq == 0 and dec_batch % SAMPLE_BATCH_TILE == 0
    assert a_w_s.shape[-1] == CHUNK_A and seq % CHUNK_A == 0

    p_a = p_prompt.reshape(depth, n_prompt, -1)
    p_b = p_sample.reshape(depth, n_sample, -1)

    bf = lambda w: w.astype(BF16)
    vec = lambda g: g.reshape(g.shape[0], 1, g.shape[-1])
    a_w_in_b, a_w_out_b, b_w_out_b = bf(a_w_in), bf(a_w_out), bf(b_w_out)
    ple_w_proj_b, ple_w_gate_b = bf(ple_w_proj), bf(ple_w_gate)
    n_ffn1, n_mix, n_ffn2, n_ple = vec(norm_ffn1), vec(norm_mix), vec(norm_ffn2), vec(norm_ple)
    ln_g, ln_b = vec(a_ln_g), vec(a_ln_b)
    final_g = final_norm.reshape(1, d)

    reps = CHUNK_A // dec_seq
    tri = jnp.tril(jnp.ones((CHUNK_A, CHUNK_A), bool))
    pos_c = jnp.arange(CHUNK_A) // dec_seq
    blockdiag = pos_c[:, None] == pos_c[None, :]
    w_prompt = jnp.where(tri, a_w_s, 0.0)
    w_sample = jnp.where(tri & blockdiag, jnp.tile(a_w_s[:, :, :dec_seq, :dec_seq], (1, 1, reps, reps)), 0.0)
    wmix = jnp.stack([w_prompt, w_sample], axis=1).astype(BF16)
    b_prompt = jnp.swapaxes(a_b_s, 1, 2)
    b_sample = jnp.tile(b_prompt[:, :dec_seq], (1, reps, 1))
    bias = jnp.repeat(jnp.stack([b_prompt, b_sample], axis=1), LANES, axis=-1)

    new_v, s_prompt, s_sample = [], [], []
    h = None
    for li in range(depth):
        if li == 0:
            h = _ffn(x_prompt.reshape(n_prompt, d), x_sample.reshape(n_sample, d), n_ffn1, ffn1_w_in, ffn1_w_out, li)
        else:
            h = _ffn(h, None, n_ffn1, ffn1_w_in, ffn1_w_out, li)
        j = li // 2
        if li % 2 == 0:
            u = _gmlp_proj(h, n_mix, a_w_in_b, ln_g, ln_b, li, j, half=0, layer_norm=False, out_dtype=BF16)
            v = _gmlp_proj(h, n_mix, a_w_in_b, ln_g, ln_b, li, j, half=1, layer_norm=True, out_dtype=F32)
            new_v.append(v[n_prompt:].reshape(dec_batch, dec_seq, -1))
            h = _gmlp_mix(h, u, v, wmix[j], bias[j], a_w_out_b, j, n_prompt // TM)
        else:
            f, qig = _hgrn_proj(h, n_mix, b_w_in, b_lb_logits, li, j)
            ng = b_norm_g[j].reshape(1, heads * LANES)
            og_p, s_fin = _hgrn_prompt(f, qig, ng, batch, seq, heads)
            og_s, s_new = _hgrn_sample(f, qig, ng, state_hgrn[j], n_prompt, dec_batch, dec_seq, heads)
            s_prompt.append(s_fin)
            s_sample.append(s_new)
            h = _out_proj(h, og_p, og_s, b_w_out_b, j)
        h = _ffn(h, None, n_ffn2, ffn2_w_in, ffn2_w_out, li)
        h = _ple(h, p_a, p_b, n_ple, ple_w_gate_b, ple_w_proj_b, final_g, li, final=(li == depth - 1))

    y_prompt = h[0].reshape(batch, seq, d)
    y_sample = h[1].reshape(dec_batch, dec_seq, d)
    return (y_prompt, y_sample, jnp.stack(s_prompt), jnp.stack(s_sample), jnp.stack(new_v))
```

```python
import functools

import jax
import jax.numpy as jnp
from jax import lax
from jax.experimental import pallas as pl
from jax.experimental.pallas import tpu as pltpu

F32 = jnp.float32
BF16 = jnp.bfloat16
EPS = 1e-6

LANES = 128
SUBLANES = 8
VMEM_LIMIT_BYTES = 56 * 1024 * 1024

TM = 512
TM_STREAM = 1088
TF = 256
TN = 512
TNQ = 256
HGRN_BLOCK = 128
HGRN_HEADS_PER_STEP = 4
CHUNK_A = 128
SAMPLE_BATCH_TILE = 4


def _cparams(sem):
    return pltpu.CompilerParams(dimension_semantics=sem, vmem_limit_bytes=VMEM_LIMIT_BYTES)


def _rms(x, g):
    ms = jnp.mean(x * x, axis=-1, keepdims=True)
    return x * lax.rsqrt(ms + EPS) * g


def _resident(shape, index_map):
    return pl.BlockSpec(shape, index_map, pipeline_mode=pl.Buffered(1))


def _dot(a, b):
    return jnp.dot(a, b, preferred_element_type=F32)


def _dot_nt(a, b):
    return lax.dot_general(a, b, (((1,), (1,)), ((), ())), preferred_element_type=F32)


def _dot_tn(a, b):
    return lax.dot_general(a, b, (((0,), (0,)), ((), ())), preferred_element_type=F32)


def _ffn_kernel(xa_ref, xb_ref, g_ref, wg_ref, wu_ref, wo_ref, o_ref, xbuf_ref, xn_ref, sem, *, n_a, n_b):
    tm = o_ref.shape[0]
    full_a, rem_a = divmod(n_a, tm)
    n_tiles = (n_a + n_b) // tm

    def tile_copies(tile, act):
        @pl.when(tile < full_a)
        def _():
            act(pltpu.make_async_copy(xa_ref.at[pl.ds(tile * tm, tm)], xbuf_ref, sem.at[0]))

        if n_b:
            @pl.when(tile == full_a)
            def _():
                if rem_a:
                    act(pltpu.make_async_copy(xa_ref.at[pl.ds(full_a * tm, rem_a)],
                                              xbuf_ref.at[pl.ds(0, rem_a)], sem.at[0]))
                act(pltpu.make_async_copy(xb_ref, xbuf_ref.at[pl.ds(rem_a, n_b)], sem.at[1]))

    @pl.when(pl.program_id(1) == 0)
    def _():
        i = pl.program_id(0)

        @pl.when(i == 0)
        def _():
            tile_copies(i, lambda cp: cp.start())

        tile_copies(i, lambda cp: cp.wait())
        x = xbuf_ref[...]
        o_ref[...] = x
        xn_ref[...] = _rms(x, g_ref[...]).astype(BF16)

        @pl.when(i + 1 < n_tiles)
        def _():
            tile_copies(i + 1, lambda cp: cp.start())

    xn = xn_ref[...]
    gate = _dot(xn, wg_ref[...].astype(BF16))
    up = _dot(xn, wu_ref[...].astype(BF16))
    act = (0.5 * gate * jax.nn.sigmoid(gate) * up).astype(BF16)
    o_ref[...] += _dot(act, wo_ref[...].astype(BF16))


def _ffn(xa, xb, norm_g, w_in, w_out, li):
    n_a, d = xa.shape
    n_b = 0 if xb is None else xb.shape[0]
    t = n_a + n_b
    assert t % TM_STREAM == 0 and (n_b == 0 or n_a % TM_STREAM + n_b == TM_STREAM)
    d_ff = w_out.shape[1]
    nf = d_ff // TF
    return pl.pallas_call(
        functools.partial(_ffn_kernel, n_a=n_a, n_b=n_b),
        grid=(t // TM_STREAM, nf),
        in_specs=[
            pl.BlockSpec(memory_space=pl.ANY),
            pl.BlockSpec(memory_space=pl.ANY),
            pl.BlockSpec((None, 1, d), lambda i, j: (li, 0, 0)),
            pl.BlockSpec((None, d, TF), lambda i, j: (li, 0, j)),
            pl.BlockSpec((None, d, TF), lambda i, j: (li, 0, j + nf)),
            pl.BlockSpec((None, TF, d), lambda i, j: (li, j, 0)),
        ],
        out_specs=pl.BlockSpec((TM_STREAM, d), lambda i, j: (i, 0)),
        out_shape=jax.ShapeDtypeStruct((t, d), F32),
        scratch_shapes=[pltpu.VMEM((TM_STREAM, d), F32), pltpu.VMEM((TM_STREAM, d), BF16),
                        pltpu.SemaphoreType.DMA((2,))],
        compiler_params=_cparams(("arbitrary", "arbitrary")),
        name=f"ffn_l{li}",
    )(xa, xa if xb is None else xb, norm_g, w_in, w_in, w_out)


def _ple_kernel(h_ref, pa_ref, pb_ref, g_ref, wgate_ref, wproj_ref, fn_ref, *o_refs, final, na):
    i = pl.program_id(0)
    h = h_ref[...]
    hn = _rms(h, g_ref[...]).astype(BF16)
    gate = jax.nn.sigmoid(_dot(hn, wgate_ref[...]))
    p = jnp.where(i < na, pa_ref[...], pb_ref[...])
    proj = _dot(p.astype(BF16), wproj_ref[...])
    out = h + proj * gate
    if not final:
        o_refs[0][...] = out
        return
    out = _rms(out, fn_ref[...])

    @pl.when(i < na)
    def _():
        o_refs[0][...] = out

    @pl.when(i >= na)
    def _():
        o_refs[1][...] = out


def _ple(h, pa, pb, norm_g, w_gate, w_proj, final_g, li, final):
    t, d = h.shape
    dp = pa.shape[-1]
    na = pa.shape[1] // TM
    in_a = lambda i: jnp.minimum(i, na - 1)
    in_b = lambda i: jnp.maximum(i - na, 0)
    if final:
        out_specs = [pl.BlockSpec((TM, d), lambda i: (in_a(i), 0)), pl.BlockSpec((TM, d), lambda i: (in_b(i), 0))]
        out_shape = [jax.ShapeDtypeStruct((pa.shape[1], d), F32), jax.ShapeDtypeStruct((pb.shape[1], d), F32)]
    else:
        out_specs = pl.BlockSpec((TM, d), lambda i: (i, 0))
        out_shape = jax.ShapeDtypeStruct((t, d), F32)
    return pl.pallas_call(
        functools.partial(_ple_kernel, final=final, na=na),
        grid=(t // TM,),
        in_specs=[
            pl.BlockSpec((TM, d), lambda i: (i, 0)),
            pl.BlockSpec((None, TM, dp), lambda i: (li, in_a(i), 0)),
            pl.BlockSpec((None, TM, dp), lambda i: (li, in_b(i), 0)),
            _resident((None, 1, d), lambda i: (li, 0, 0)),
            _resident((None, d, d), lambda i: (li, 0, 0)),
            _resident((None, dp, d), lambda i: (li, 0, 0)),
            _resident((1, d), lambda i: (0, 0)),
        ],
        out_specs=out_specs,
        out_shape=out_shape,
        compiler_params=_cparams(("arbitrary",)),
        name=f"ple_l{li}",
    )(h, pa, pb, norm_g, w_gate, w_proj, final_g)


def _gelu(z):
    return 0.5 * z * (1.0 + lax.erf(z * (2.0 ** -0.5)))


def _gmlp_proj_kernel(x_ref, g_ref, w_ref, lng_ref, lnb_ref, o_ref, xn_ref, *, layer_norm):
    @pl.when(pl.program_id(1) == 0)
    def _():
        xn_ref[...] = _rms(x_ref[...], g_ref[...]).astype(BF16)

    z = _gelu(_dot(xn_ref[...], w_ref[...]))
    if layer_norm:
        mu = jnp.mean(z, axis=-1, keepdims=True)
        zc = z - mu
        var = jnp.mean(zc * zc, axis=-1, keepdims=True)
        z = zc * lax.rsqrt(var + EPS) * lng_ref[...] + lnb_ref[...]
    o_ref[...] = z.astype(o_ref.dtype)


def _gmlp_proj(h, norm_g, w_in, ln_g, ln_b, li, j_layer, *, half, layer_norm, out_dtype):
    t, d = h.shape
    d_a = w_in.shape[-1] // 2
    tn = d_a
    nn = d_a // tn
    off = half * nn
    return pl.pallas_call(
        functools.partial(_gmlp_proj_kernel, layer_norm=layer_norm),
        grid=(t // TM, nn),
        in_specs=[
            pl.BlockSpec((TM, d), lambda i, j: (i, 0)),
            pl.BlockSpec((None, 1, d), lambda i, j: (li, 0, 0)),
            pl.BlockSpec((None, d, tn), lambda i, j: (j_layer, 0, j + off)),
            pl.BlockSpec((None, 1, tn), lambda i, j: (j_layer, 0, j)),
            pl.BlockSpec((None, 1, tn), lambda i, j: (j_layer, 0, j)),
        ],
        out_specs=pl.BlockSpec((TM, tn), lambda i, j: (i, j)),
        out_shape=jax.ShapeDtypeStruct((t, d_a), out_dtype),
        scratch_shapes=[pltpu.VMEM((TM, d), BF16)],
        compiler_params=_cparams(("parallel", "arbitrary")),
        name=f"gmlp_proj_l{li}_{'v' if half else 'u'}",
    )(h, norm_g, w_in, ln_g, ln_b)


def _gmlp_mix_kernel(h_ref, u_ref, v_ref, wmix_ref, bias_ref, wo_ref, o_ref, t_ref, *, groups):
    for c in range(TM // CHUNK_A):
        rows = slice(c * CHUNK_A, (c + 1) * CHUNK_A)
        for g in range(groups):
            cols = slice(g * LANES, (g + 1) * LANES)
            s = _dot(wmix_ref[g], v_ref[rows, cols].astype(BF16)) + bias_ref[:, cols]
            t_ref[rows, cols] = (u_ref[rows, cols].astype(F32) * s).astype(BF16)
    o_ref[...] = h_ref[...] + _dot(t_ref[...], wo_ref[...])


def _gmlp_mix(h, u, v, wmix, bias, w_out, j_layer, n_prompt_tiles):
    t, d = h.shape
    d_a = u.shape[1]
    groups = d_a // LANES
    return pl.pallas_call(
        functools.partial(_gmlp_mix_kernel, groups=groups),
        grid=(t // TM,),
        in_specs=[
            pl.BlockSpec((TM, d), lambda i: (i, 0)),
            pl.BlockSpec((TM, d_a), lambda i: (i, 0)),
            pl.BlockSpec((TM, d_a), lambda i: (i, 0)),
            pl.BlockSpec((None, groups, CHUNK_A, CHUNK_A), lambda i: (i // n_prompt_tiles, 0, 0, 0)),
            pl.BlockSpec((None, CHUNK_A, d_a), lambda i: (i // n_prompt_tiles, 0, 0)),
            _resident((None, d_a, d), lambda i: (j_layer, 0, 0)),
        ],
        out_specs=pl.BlockSpec((TM, d), lambda i: (i, 0)),
        out_shape=jax.ShapeDtypeStruct((t, d), F32),
        scratch_shapes=[pltpu.VMEM((TM, d_a), BF16)],
        compiler_params=_cparams(("parallel",)),
        name="gmlp_mix",
    )(h, u, v, wmix, bias, w_out)


def _hgrn_proj_kernel(x_ref, g_ref, wq_ref, wf_ref, wi_ref, wg_ref, lbl_ref,
                      q_ref, f_ref, i_ref, gt_ref, xn_ref, *, li):
    @pl.when(pl.program_id(1) == 0)
    def _():
        xn_ref[...] = _rms(x_ref[...], g_ref[...]).astype(BF16)

    xn = xn_ref[...]
    zq = _dot(xn, wq_ref[...].astype(BF16))
    q_ref[...] = (zq * jax.nn.sigmoid(zq)).astype(BF16)

    logits = lbl_ref[...]
    ex = jnp.exp(logits - jnp.max(logits, axis=0, keepdims=True))
    sm = ex / jnp.sum(ex, axis=0, keepdims=True)
    lb = jnp.sum(sm[: li + 1], axis=0, keepdims=True) - sm[0:1]
    zf = _dot(xn, wf_ref[...].astype(BF16))
    f_ref[...] = lb + (1.0 - lb) * jax.nn.sigmoid(zf)

    i_ref[...] = _dot(xn, wi_ref[...].astype(BF16)).astype(BF16)

    zg = _dot(xn, wg_ref[...].astype(BF16))
    gt_ref[...] = (zg * jax.nn.sigmoid(zg)).astype(BF16)


def _hgrn_proj(h, norm_g, w_in, lb_logits, li, j_layer):
    t, d = h.shape
    d_b = w_in.shape[-1] // 4
    per_part = d_b // TNQ
    w_spec = lambda p: pl.BlockSpec((None, d, TNQ), lambda i, j, p=p: (j_layer, 0, p * per_part + j))
    o_spec = pl.BlockSpec((TM_STREAM, TNQ), lambda i, j: (i, j))
    bf = jax.ShapeDtypeStruct((t, d_b), BF16)
    return pl.pallas_call(
        functools.partial(_hgrn_proj_kernel, li=li),
        grid=(t // TM_STREAM, per_part),
        in_specs=[
            pl.BlockSpec((TM_STREAM, d), lambda i, j: (i, 0)),
            pl.BlockSpec((None, 1, d), lambda i, j: (li, 0, 0)),
            w_spec(0), w_spec(1), w_spec(2), w_spec(3),
            pl.BlockSpec((lb_logits.shape[0], TNQ), lambda i, j: (0, j)),
        ],
        out_specs=[o_spec, o_spec, o_spec, o_spec],
        out_shape=[bf, jax.ShapeDtypeStruct((t, d_b), F32), bf, bf],
        scratch_shapes=[pltpu.VMEM((TM_STREAM, d), BF16)],
        compiler_params=_cparams(("parallel", "arbitrary")),
        name="hgrn_proj",
    )(h, norm_g, w_in, w_in, w_in, w_in, lb_logits)


def _hgrn_halves():
    halves, half = [], HGRN_BLOCK // 2
    while half >= 1:
        halves.append(half)
        half //= 2
    return tuple(halves)


def _hgrn_norm_gate(o, ng, gate):
    ms = jnp.mean(o * o, axis=-1, keepdims=True)
    return (o * lax.rsqrt(ms + EPS) * ng * gate).astype(BF16)


def _hgrn_prompt_kernel(f_ref, q_ref, i_ref, g_ref, ng_ref, og_ref, sfin_ref,
                        st_ref, tril_ref, sgn_ref, mask_ref, *, nblk, hps):
    c = HGRN_BLOCK
    w = hps * LANES
    halves = _hgrn_halves()
    nl = len(halves)

    @pl.when((pl.program_id(0) == 0) & (pl.program_id(1) == 0))
    def _():
        r = lax.broadcasted_iota(jnp.int32, (c, c), 0)
        s = lax.broadcasted_iota(jnp.int32, (c, c), 1)
        tril_ref[...] = jnp.where(s <= r, 1.0, 0.0).astype(BF16)
        for lvl, half in enumerate(halves):
            par = 2 * half
            second = (r & half) != 0
            sgn_ref[lvl] = jnp.where(second, 1.0, -1.0)
            valid = second & ((s & half) == 0) & ((r // par) == (s // par))
            mask_ref[lvl] = jnp.where(valid, 1.0, 0.0)
        mask_ref[nl] = jnp.where(r == s, 1.0, 0.0)

    st_ref[...] = jnp.zeros_like(st_ref)
    low_sub = lax.broadcasted_iota(jnp.int32, (c // SUBLANES, SUBLANES, w), 1) < SUBLANES // 2

    def body(blk, carry):
        rows = pl.ds(pl.multiple_of(blk * c, c), c)
        f = f_ref[rows, :]
        qb = q_ref[rows, :]
        vb = i_ref[rows, :]
        q = qb.astype(F32)
        k = 1.0 - f
        kb = k.astype(BF16)

        x = jnp.log(f)
        hi = x.astype(BF16)
        r1 = x - hi.astype(F32)
        mid = r1.astype(BF16)
        lo = (r1 - mid.astype(F32)).astype(BF16)
        tril = tril_ref[...]
        b = _dot(tril, hi) + _dot(tril, mid) + _dot(tril, lo)
        b3 = b.reshape(c // SUBLANES, SUBLANES, w)

        att = [None] * hps

        def add_level(lvl, qt, kt):
            for hd in range(hps):
                cols = slice(hd * LANES, (hd + 1) * LANES)
                a = _dot_nt(qt[:, cols], kt[:, cols]) * mask_ref[lvl]
                att[hd] = a if att[hd] is None else att[hd] + a

        for lvl, half in enumerate(halves):
            par = 2 * half
            if half == 1:
                add_level(lvl, (q * f).astype(BF16), kb)
                continue
            if half >= SUBLANES:
                pieces = [jnp.broadcast_to(b[p * par + half - 1:p * par + half, :], (par, w))
                          for p in range(c // par)]
                beta = pieces[0] if len(pieces) == 1 else jnp.concatenate(pieces, axis=0)
            elif half == SUBLANES // 2:
                beta = jnp.broadcast_to(b3[:, half - 1:half, :], b3.shape).reshape(c, w)
            else:
                beta = jnp.where(low_sub, b3[:, half - 1:half, :],
                                 b3[:, par + half - 1:par + half, :]).reshape(c, w)
            sgn = jnp.concatenate([sgn_ref[lvl]] * hps, axis=1)
            e = jnp.exp((b - beta) * sgn).astype(BF16)
            add_level(lvl, qb * e, kb * e)
        add_level(nl, qb, kb)

        qe = (q * jnp.exp(b)).astype(BF16)
        b_last = b[c - 1:c, :]
        kd = (k * jnp.exp(b_last - b)).astype(BF16)
        e_last = jnp.exp(b_last)
        gate = g_ref[rows, :].astype(F32)
        ng = ng_ref[...]
        for hd in range(hps):
            cols = slice(hd * LANES, (hd + 1) * LANES)
            st = st_ref[hd]
            o = _dot_nt(qe[:, cols], st.astype(BF16)) + _dot(att[hd].astype(BF16), vb[:, cols])
            og_ref[rows, cols] = _hgrn_norm_gate(o, ng[:, cols], gate[:, cols])
            st_ref[hd] = st * e_last[:, cols] + _dot_tn(vb[:, cols], kd[:, cols])
        return carry

    lax.fori_loop(0, nblk, body, 0, unroll=2)
    for hd in range(hps):
        sfin_ref[hd] = st_ref[hd].T


def _hgrn_prompt(f, q, i, g, norm_g, batch, seq, heads):
    hps = HGRN_HEADS_PER_STEP
    w = hps * LANES
    ngrp = heads // hps
    nblk = seq // HGRN_BLOCK
    nl = len(_hgrn_halves())
    part = pl.BlockSpec((seq, w), lambda s, h: (s, h))
    return pl.pallas_call(
        functools.partial(_hgrn_prompt_kernel, nblk=nblk, hps=hps),
        grid=(batch, ngrp),
        in_specs=[part, part, part, part,
                  pl.BlockSpec((1, w), lambda s, h: (0, h))],
        out_specs=[pl.BlockSpec((seq, w), lambda s, h: (s, h)),
                   pl.BlockSpec((None, hps, LANES, LANES), lambda s, h: (s, h, 0, 0))],
        out_shape=[jax.ShapeDtypeStruct((batch * seq, heads * LANES), BF16),
                   jax.ShapeDtypeStruct((batch, heads, LANES, LANES), F32)],
        scratch_shapes=[pltpu.VMEM((hps, LANES, LANES), F32),
                        pltpu.VMEM((HGRN_BLOCK, HGRN_BLOCK), BF16),
                        pltpu.VMEM((nl, HGRN_BLOCK, LANES), F32),
                        pltpu.VMEM((nl + 1, HGRN_BLOCK, HGRN_BLOCK), F32)],
        compiler_params=_cparams(("arbitrary", "arbitrary")),
        name="hgrn_prompt",
    )(f, q, i, g, norm_g)


def _hgrn_sample_kernel(f_ref, q_ref, i_ref, g_ref, ng_ref, s0_ref, og_ref, s1_ref, *, heads, dec_seq):
    rows_per_tile = SUBLANES
    per_tile = rows_per_tile // dec_seq
    row = lax.broadcasted_iota(jnp.int32, (rows_per_tile, LANES), 0)
    pos = row % dec_seq
    first = row < dec_seq
    zpad = jnp.zeros((rows_per_tile, LANES), F32)
    prow = lax.broadcasted_iota(jnp.int32, (4 * rows_per_tile, LANES), 0)
    in_pieces = prow < 3 * rows_per_tile
    sel_a = jnp.where(in_pieces & (prow % rows_per_tile == dec_seq - 1), 1.0, 0.0).astype(BF16)
    sel_b = jnp.where(in_pieces & (prow % rows_per_tile == 2 * dec_seq - 1), 1.0, 0.0).astype(BF16)
    f_all = f_ref[...]
    q_all = q_ref[...].astype(F32)
    v_all = i_ref[...].astype(F32)
    g_all = g_ref[...].astype(F32)
    ng = ng_ref[...]

    def pick(x, s):
        return jnp.where(first, x[s:s + 1, :], x[dec_seq + s:dec_seq + s + 1, :])

    for tile in range(SAMPLE_BATCH_TILE // per_tile):
        rows = slice(tile * rows_per_tile, (tile + 1) * rows_per_tile)
        for h in range(heads):
            cols = slice(h * LANES, (h + 1) * LANES)
            q = q_all[rows, cols]
            f = f_all[rows, cols]
            v = v_all[rows, cols]
            k = 1.0 - f
            b = jnp.log(f)
            shift = 1
            while shift < dec_seq:
                b = b + jnp.where(pos >= shift, pltpu.roll(b, shift, 0), 0.0)
                shift *= 2
            eb = jnp.exp(b)
            qe = (q * eb).astype(BF16)
            s_a = s0_ref[tile * per_tile, h]
            s_b = s0_ref[tile * per_tile + 1, h]
            o = jnp.where(first, _dot(qe, s_a.astype(BF16)), _dot(qe, s_b.astype(BF16)))
            for s in range(dec_seq):
                e = jnp.exp(jnp.minimum(b - pick(b, s), 0.0))
                wgt = jnp.sum(q * e * pick(k, s), axis=-1, keepdims=True)
                o = o + jnp.where(pos >= s, wgt, 0.0) * pick(v, s)
            og_ref[rows, cols] = _hgrn_norm_gate(o, ng[:, cols], g_all[rows, cols])

            b_last = pick(b, dec_seq - 1)
            kd = jnp.concatenate([k * jnp.exp(b_last - b), zpad], axis=0).astype(BF16)
            v_a = jnp.concatenate([jnp.where(first, v, 0.0), zpad], axis=0).astype(BF16)
            v_b = jnp.concatenate([jnp.where(first, 0.0, v), zpad], axis=0).astype(BF16)
            hi = eb.astype(BF16).astype(F32)
            mid = (eb - hi).astype(BF16).astype(F32)
            lo = eb - hi - mid
            pieces = jnp.concatenate([hi, mid, lo, zpad], axis=0).astype(BF16)
            s1_ref[tile * per_tile, h] = s_a * _dot_tn(pieces, sel_a) + _dot_tn(kd, v_a)
            s1_ref[tile * per_tile + 1, h] = s_b * _dot_tn(pieces, sel_b) + _dot_tn(kd, v_b)


def _hgrn_sample(f, q, i, g, norm_g, s0, n_prompt, dec_batch, dec_seq, heads):
    rows = SAMPLE_BATCH_TILE * dec_seq
    first_blk = n_prompt // rows
    d_b = heads * LANES
    part = pl.BlockSpec((rows, d_b), lambda i: (first_blk + i, 0))
    return pl.pallas_call(
        functools.partial(_hgrn_sample_kernel, heads=heads, dec_seq=dec_seq),
        grid=(dec_batch // SAMPLE_BATCH_TILE,),
        in_specs=[part, part, part, part,
                  _resident((1, d_b), lambda i: (0, 0)),
                  pl.BlockSpec((SAMPLE_BATCH_TILE, heads, LANES, LANES), lambda i: (i, 0, 0, 0))],
        out_specs=[pl.BlockSpec((rows, d_b), lambda i: (i, 0)),
                   pl.BlockSpec((SAMPLE_BATCH_TILE, heads, LANES, LANES), lambda i: (i, 0, 0, 0))],
        out_shape=[jax.ShapeDtypeStruct((dec_batch * dec_seq, d_b), BF16),
                   jax.ShapeDtypeStruct(s0.shape, F32)],
        compiler_params=_cparams(("parallel",)),
        name="hgrn_sample",
    )(f, q, i, g, norm_g, s0)


def _out_proj_kernel(h_ref, xa_ref, xb_ref, w_ref, o_ref, *, na):
    x = jnp.where(pl.program_id(0) < na, xa_ref[...], xb_ref[...])
    o_ref[...] = h_ref[...] + _dot(x, w_ref[...])


def _out_proj(h, xa, xb, w, j_layer):
    t, d = h.shape
    kdim = xa.shape[1]
    na = xa.shape[0] // TM
    return pl.pallas_call(
        functools.partial(_out_proj_kernel, na=na),
        grid=(t // TM,),
        in_specs=[
            pl.BlockSpec((TM, d), lambda i: (i, 0)),
            pl.BlockSpec((TM, kdim), lambda i: (jnp.minimum(i, na - 1), 0)),
            pl.BlockSpec((TM, kdim), lambda i: (jnp.maximum(i - na, 0), 0)),
            _resident((None, kdim, d), lambda i: (j_layer, 0, 0)),
        ],
        out_specs=pl.BlockSpec((TM, d), lambda i: (i, 0)),
        out_shape=jax.ShapeDtypeStruct((t, d), F32),
        compiler_params=_cparams(("parallel",)),
        name="hgrn_out_proj",
    )(h, xa, xb, w)


def kernel(x_prompt, x_sample, state_hgrn, p_prompt, p_sample, ffn1_w_in, ffn1_w_out, ffn2_w_in, ffn2_w_out, norm_ffn1, norm_mix, norm_ffn2, norm_ple, a_w_in, a_ln_g, a_ln_b, a_w_s, a_b_s, a_w_out, b_w_in, b_lb_logits, b_norm_g, b_w_out, ple_w_proj, ple_w_gate, final_norm):
    batch, seq, d = x_prompt.shape
    dec_batch, dec_seq, _ = x_sample.shape
    depth = ffn1_w_in.shape[0]
    n_prompt = batch * seq
    n_sample = dec_batch * dec_seq
    t = n_prompt + n_sample
    heads = b_norm_g.shape[-1] // LANES
    assert n_prompt % TM == 0 and n_sample % TM == 0 and t % TM_STREAM == 0
    assert seq % HGRN_BLOCK == 0 and heads % HGRN_HEADS_PER_STEP == 0
    assert CHUNK_A % dec_seq == 0 and SUBLANES % dec_seq == 0 and dec_batch % SAMPLE_BATCH_TILE == 0
    assert a_w_s.shape[-1] == CHUNK_A and seq % CHUNK_A == 0

    p_a = p_prompt.reshape(depth, n_prompt, -1)
    p_b = p_sample.reshape(depth, n_sample, -1)

    bf = lambda w: w.astype(BF16)
    vec = lambda g: g.reshape(g.shape[0], 1, g.shape[-1])
    a_w_in_b, a_w_out_b, b_w_out_b = bf(a_w_in), bf(a_w_out), bf(b_w_out)
    ple_w_proj_b, ple_w_gate_b = bf(ple_w_proj), bf(ple_w_gate)
    n_ffn1, n_mix, n_ffn2, n_ple = vec(norm_ffn1), vec(norm_mix), vec(norm_ffn2), vec(norm_ple)
    ln_g, ln_b = vec(a_ln_g), vec(a_ln_b)
    final_g = final_norm.reshape(1, d)

    reps = CHUNK_A // dec_seq
    tri = jnp.tril(jnp.ones((CHUNK_A, CHUNK_A), bool))
    pos_c = jnp.arange(CHUNK_A) // dec_seq
    blockdiag = pos_c[:, None] == pos_c[None, :]
    w_prompt = jnp.where(tri, a_w_s, 0.0)
    w_sample = jnp.where(tri & blockdiag, jnp.tile(a_w_s[:, :, :dec_seq, :dec_seq], (1, 1, reps, reps)), 0.0)
    wmix = jnp.stack([w_prompt, w_sample], axis=1).astype(BF16)
    b_prompt = jnp.swapaxes(a_b_s, 1, 2)
    b_sample = jnp.tile(b_prompt[:, :dec_seq], (1, reps, 1))
    bias = jnp.repeat(jnp.stack([b_prompt, b_sample], axis=1), LANES, axis=-1)

    new_v, s_prompt, s_sample = [], [], []
    h = None
    for li in range(depth):
        if li == 0:
            h = _ffn(x_prompt.reshape(n_prompt, d), x_sample.reshape(n_sample, d), n_ffn1, ffn1_w_in, ffn1_w_out, li)
        else:
            h = _ffn(h, None, n_ffn1, ffn1_w_in, ffn1_w_out, li)
        j = li // 2
        if li % 2 == 0:
            u = _gmlp_proj(h, n_mix, a_w_in_b, ln_g, ln_b, li, j, half=0, layer_norm=False, out_dtype=BF16)
            v = _gmlp_proj(h, n_mix, a_w_in_b, ln_g, ln_b, li, j, half=1, layer_norm=True, out_dtype=F32)
            new_v.append(v[n_prompt:].reshape(dec_batch, dec_seq, -1))
            h = _gmlp_mix(h, u, v, wmix[j], bias[j], a_w_out_b, j, n_prompt // TM)
        else:
            q, f, iv, g = _hgrn_proj(h, n_mix, b_w_in, b_lb_logits, li, j)
            ng = b_norm_g[j].reshape(1, heads * LANES)
            og_p, s_fin = _hgrn_prompt(f, q, iv, g, ng, batch, seq, heads)
            og_s, s_new = _hgrn_sample(f, q, iv, g, ng, state_hgrn[j], n_prompt, dec_batch, dec_seq, heads)
            s_prompt.append(s_fin)
            s_sample.append(s_new)
            h = _out_proj(h, og_p, og_s, b_w_out_b, j)
        h = _ffn(h, None, n_ffn2, ffn2_w_in, ffn2_w_out, li)
        h = _ple(h, p_a, p_b, n_ple, ple_w_gate_b, ple_w_proj_b, final_g, li, final=(li == depth - 1))

    y_prompt = h[0].reshape(batch, seq, d)
    y_sample = h[1].reshape(dec_batch, dec_seq, d)
    return (y_prompt, y_sample, jnp.stack(s_prompt), jnp.stack(s_sample), jnp.stack(new_v))
```

```python
import functools

import jax
import jax.numpy as jnp
from jax import lax
from jax.experimental import pallas as pl
from jax.experimental.pallas import tpu as pltpu

F32 = jnp.float32
BF16 = jnp.bfloat16
EPS = 1e-6

LANES = 128
SUBLANES = 8
VMEM_LIMIT_BYTES = 56 * 1024 * 1024

TM = 512
TM_STREAM = 1088
TF = 256
TN = 512
TNQ = 256
HGRN_BLOCK = 128
HGRN_HEADS_PER_STEP = 4
CHUNK_A = 128
SAMPLE_BATCH_TILE = 4


def _cparams(sem):
    return pltpu.CompilerParams(dimension_semantics=sem, vmem_limit_bytes=VMEM_LIMIT_BYTES)


def _rms(x, g):
    ms = jnp.mean(x * x, axis=-1, keepdims=True)
    return x * lax.rsqrt(ms + EPS) * g


def _resident(shape, index_map):
    return pl.BlockSpec(shape, index_map, pipeline_mode=pl.Buffered(1))


def _dot(a, b):
    return jnp.dot(a, b, preferred_element_type=F32)


def _dot_nt(a, b):
    return lax.dot_general(a, b, (((1,), (1,)), ((), ())), preferred_element_type=F32)


def _dot_tn(a, b):
    return lax.dot_general(a, b, (((0,), (0,)), ((), ())), preferred_element_type=F32)


def _ffn_kernel(xa_ref, xb_ref, g_ref, wg_ref, wu_ref, wo_ref, o_ref, xbuf_ref, xn_ref, sem, *, n_a, n_b):
    tm = o_ref.shape[0]
    full_a, rem_a = divmod(n_a, tm)
    n_tiles = (n_a + n_b) // tm

    def tile_copies(tile, act):
        @pl.when(tile < full_a)
        def _():
            act(pltpu.make_async_copy(xa_ref.at[pl.ds(tile * tm, tm)], xbuf_ref, sem.at[0]))

        if n_b:
            @pl.when(tile == full_a)
            def _():
                if rem_a:
                    act(pltpu.make_async_copy(xa_ref.at[pl.ds(full_a * tm, rem_a)],
                                              xbuf_ref.at[pl.ds(0, rem_a)], sem.at[0]))
                act(pltpu.make_async_copy(xb_ref, xbuf_ref.at[pl.ds(rem_a, n_b)], sem.at[1]))

    @pl.when(pl.program_id(1) == 0)
    def _():
        i = pl.program_id(0)

        @pl.when(i == 0)
        def _():
            tile_copies(i, lambda cp: cp.start())

        tile_copies(i, lambda cp: cp.wait())
        x = xbuf_ref[...]
        o_ref[...] = x
        xn_ref[...] = _rms(x, g_ref[...]).astype(BF16)

        @pl.when(i + 1 < n_tiles)
        def _():
            tile_copies(i + 1, lambda cp: cp.start())

    xn = xn_ref[...]
    gate = _dot(xn, wg_ref[...].astype(BF16))
    up = _dot(xn, wu_ref[...].astype(BF16))
    act = (0.5 * gate * jax.nn.sigmoid(gate) * up).astype(BF16)
    o_ref[...] += _dot(act, wo_ref[...].astype(BF16))


def _ffn(xa, xb, norm_g, w_in, w_out, li):
    n_a, d = xa.shape
    n_b = 0 if xb is None else xb.shape[0]
    t = n_a + n_b
    assert t % TM_STREAM == 0 and (n_b == 0 or n_a % TM_STREAM + n_b == TM_STREAM)
    d_ff = w_out.shape[1]
    nf = d_ff // TF
    return pl.pallas_call(
        functools.partial(_ffn_kernel, n_a=n_a, n_b=n_b),
        grid=(t // TM_STREAM, nf),
        in_specs=[
            pl.BlockSpec(memory_space=pl.ANY),
            pl.BlockSpec(memory_space=pl.ANY),
            pl.BlockSpec((None, 1, d), lambda i, j: (li, 0, 0)),
            pl.BlockSpec((None, d, TF), lambda i, j: (li, 0, j)),
            pl.BlockSpec((None, d, TF), lambda i, j: (li, 0, j + nf)),
            pl.BlockSpec((None, TF, d), lambda i, j: (li, j, 0)),
        ],
        out_specs=pl.BlockSpec((TM_STREAM, d), lambda i, j: (i, 0)),
        out_shape=jax.ShapeDtypeStruct((t, d), F32),
        scratch_shapes=[pltpu.VMEM((TM_STREAM, d), F32), pltpu.VMEM((TM_STREAM, d), BF16),
                        pltpu.SemaphoreType.DMA((2,))],
        compiler_params=_cparams(("arbitrary", "arbitrary")),
        name=f"ffn_l{li}",
    )(xa, xa if xb is None else xb, norm_g, w_in, w_in, w_out)


def _ple_kernel(h_ref, pa_ref, pb_ref, g_ref, wgate_ref, wproj_ref, fn_ref, *o_refs, final, na):
    i = pl.program_id(0)
    h = h_ref[...]
    hn = _rms(h, g_ref[...]).astype(BF16)
    gate = jax.nn.sigmoid(_dot(hn, wgate_ref[...]))
    p = jnp.where(i < na, pa_ref[...], pb_ref[...])
    proj = _dot(p.astype(BF16), wproj_ref[...])
    out = h + proj * gate
    if not final:
        o_refs[0][...] = out
        return
    out = _rms(out, fn_ref[...])

    @pl.when(i < na)
    def _():
        o_refs[0][...] = out

    @pl.when(i >= na)
    def _():
        o_refs[1][...] = out


def _ple(h, pa, pb, norm_g, w_gate, w_proj, final_g, li, final):
    t, d = h.shape
    dp = pa.shape[-1]
    na = pa.shape[1] // TM
    in_a = lambda i: jnp.minimum(i, na - 1)
    in_b = lambda i: jnp.maximum(i - na, 0)
    if final:
        out_specs = [pl.BlockSpec((TM, d), lambda i: (in_a(i), 0)), pl.BlockSpec((TM, d), lambda i: (in_b(i), 0))]
        out_shape = [jax.ShapeDtypeStruct((pa.shape[1], d), F32), jax.ShapeDtypeStruct((pb.shape[1], d), F32)]
    else:
        out_specs = pl.BlockSpec((TM, d), lambda i: (i, 0))
        out_shape = jax.ShapeDtypeStruct((t, d), F32)
    return pl.pallas_call(
        functools.partial(_ple_kernel, final=final, na=na),
        grid=(t // TM,),
        in_specs=[
            pl.BlockSpec((TM, d), lambda i: (i, 0)),
            pl.BlockSpec((None, TM, dp), lambda i: (li, in_a(i), 0)),
            pl.BlockSpec((None, TM, dp), lambda i: (li, in_b(i), 0)),
            _resident((None, 1, d), lambda i: (li, 0, 0)),
            _resident((None, d, d), lambda i: (li, 0, 0)),
            _resident((None, dp, d), lambda i: (li, 0, 0)),
            _resident((1, d), lambda i: (0, 0)),
        ],
        out_specs=out_specs,
        out_shape=out_shape,
        compiler_params=_cparams(("arbitrary",)),
        name=f"ple_l{li}",
    )(h, pa, pb, norm_g, w_gate, w_proj, final_g)


def _gelu(z):
    return 0.5 * z * (1.0 + lax.erf(z * (2.0 ** -0.5)))


def _gmlp_proj_kernel(x_ref, g_ref, wu_ref, wv_ref, u_ref, v_ref, xn_ref):
    @pl.when(pl.program_id(1) == 0)
    def _():
        xn_ref[...] = _rms(x_ref[...], g_ref[...]).astype(BF16)

    xn = xn_ref[...]
    u_ref[...] = _gelu(_dot(xn, wu_ref[...].astype(BF16))).astype(BF16)
    v_ref[...] = _gelu(_dot(xn, wv_ref[...].astype(BF16)))


def _gmlp_proj(h, norm_g, w_in, li, j_layer):
    t, d = h.shape
    d_a = w_in.shape[-1] // 2
    nn = d_a // TN
    w_spec = lambda p: pl.BlockSpec((None, d, TN), lambda i, j, p=p: (j_layer, 0, p * nn + j))
    o_spec = pl.BlockSpec((TM_STREAM, TN), lambda i, j: (i, j))
    return pl.pallas_call(
        _gmlp_proj_kernel,
        grid=(t // TM_STREAM, nn),
        in_specs=[
            pl.BlockSpec((TM_STREAM, d), lambda i, j: (i, 0)),
            pl.BlockSpec((None, 1, d), lambda i, j: (li, 0, 0)),
            w_spec(0), w_spec(1),
        ],
        out_specs=[o_spec, o_spec],
        out_shape=[jax.ShapeDtypeStruct((t, d_a), BF16), jax.ShapeDtypeStruct((t, d_a), F32)],
        scratch_shapes=[pltpu.VMEM((TM_STREAM, d), BF16)],
        compiler_params=_cparams(("parallel", "arbitrary")),
        name="gmlp_proj",
    )(h, norm_g, w_in, w_in)


def _layer_norm(v, g, b):
    mu = jnp.mean(v, axis=-1, keepdims=True)
    vc = v - mu
    var = jnp.mean(vc * vc, axis=-1, keepdims=True)
    return vc * lax.rsqrt(var + EPS) * g + b


def _gmlp_mix_kernel(h_ref, u_ref, v_ref, lng_ref, lnb_ref, wmix_ref, bias_ref, wo_ref,
                     o_ref, t_ref, *, groups):
    for c in range(TM // CHUNK_A):
        rows = slice(c * CHUNK_A, (c + 1) * CHUNK_A)
        vn = _layer_norm(v_ref[rows, :], lng_ref[...], lnb_ref[...]).astype(BF16)
        for g in range(groups):
            cols = slice(g * LANES, (g + 1) * LANES)
            s = _dot(wmix_ref[g], vn[:, cols]) + bias_ref[:, cols]
            t_ref[rows, cols] = (u_ref[rows, cols].astype(F32) * s).astype(BF16)
    o_ref[...] = h_ref[...] + _dot(t_ref[...], wo_ref[...])


def _gmlp_mix(h, u, v, ln_g, ln_b, wmix, bias, w_out, j_layer, na):
    t, d = h.shape
    d_a = u.shape[1]
    groups = d_a // LANES
    return pl.pallas_call(
        functools.partial(_gmlp_mix_kernel, groups=groups),
        grid=(t // TM,),
        in_specs=[
            pl.BlockSpec((TM, d), lambda i: (i, 0)),
            pl.BlockSpec((TM, d_a), lambda i: (i, 0)),
            pl.BlockSpec((TM, d_a), lambda i: (i, 0)),
            _resident((None, 1, d_a), lambda i: (j_layer, 0, 0)),
            _resident((None, 1, d_a), lambda i: (j_layer, 0, 0)),
            pl.BlockSpec((None, groups, CHUNK_A, CHUNK_A), lambda i: (i // na, 0, 0, 0)),
            pl.BlockSpec((None, CHUNK_A, d_a), lambda i: (i // na, 0, 0)),
            _resident((None, d_a, d), lambda i: (j_layer, 0, 0)),
        ],
        out_specs=pl.BlockSpec((TM, d), lambda i: (i, 0)),
        out_shape=jax.ShapeDtypeStruct((t, d), F32),
        scratch_shapes=[pltpu.VMEM((TM, d_a), BF16)],
        compiler_params=_cparams(("parallel",)),
        name="gmlp_mix",
    )(h, u, v, ln_g, ln_b, wmix, bias, w_out)


def _gmlp_sample_v_kernel(v_ref, lng_ref, lnb_ref, o_ref):
    o_ref[...] = _layer_norm(v_ref[...], lng_ref[...], lnb_ref[...])


def _gmlp_sample_v(v, ln_g, ln_b, j_layer, n_prompt, n_sample):
    d_a = v.shape[1]
    first = n_prompt // CHUNK_A
    return pl.pallas_call(
        _gmlp_sample_v_kernel,
        grid=(n_sample // CHUNK_A,),
        in_specs=[pl.BlockSpec((CHUNK_A, d_a), lambda i: (first + i, 0)),
                  _resident((None, 1, d_a), lambda i: (j_layer, 0, 0)),
                  _resident((None, 1, d_a), lambda i: (j_layer, 0, 0))],
        out_specs=pl.BlockSpec((CHUNK_A, d_a), lambda i: (i, 0)),
        out_shape=jax.ShapeDtypeStruct((n_sample, d_a), F32),
        compiler_params=_cparams(("parallel",)),
        name="gmlp_sample_v",
    )(v, ln_g, ln_b)


def _hgrn_proj_kernel(x_ref, g_ref, wq_ref, wf_ref, wi_ref, wg_ref, lbl_ref,
                      q_ref, f_ref, i_ref, gt_ref, xn_ref, *, li):
    @pl.when(pl.program_id(1) == 0)
    def _():
        xn_ref[...] = _rms(x_ref[...], g_ref[...]).astype(BF16)

    xn = xn_ref[...]
    zq = _dot(xn, wq_ref[...].astype(BF16))
    q_ref[...] = (zq * jax.nn.sigmoid(zq)).astype(BF16)

    logits = lbl_ref[...]
    ex = jnp.exp(logits - jnp.max(logits, axis=0, keepdims=True))
    sm = ex / jnp.sum(ex, axis=0, keepdims=True)
    lb = jnp.sum(sm[: li + 1], axis=0, keepdims=True) - sm[0:1]
    zf = _dot(xn, wf_ref[...].astype(BF16))
    f_ref[...] = lb + (1.0 - lb) * jax.nn.sigmoid(zf)

    i_ref[...] = _dot(xn, wi_ref[...].astype(BF16)).astype(BF16)

    zg = _dot(xn, wg_ref[...].astype(BF16))
    gt_ref[...] = (zg * jax.nn.sigmoid(zg)).astype(BF16)


def _hgrn_proj(h, norm_g, w_in, lb_logits, li, j_layer):
    t, d = h.shape
    d_b = w_in.shape[-1] // 4
    per_part = d_b // TNQ
    w_spec = lambda p: pl.BlockSpec((None, d, TNQ), lambda i, j, p=p: (j_layer, 0, p * per_part + j))
    o_spec = pl.BlockSpec((TM_STREAM, TNQ), lambda i, j: (i, j))
    bf = jax.ShapeDtypeStruct((t, d_b), BF16)
    return pl.pallas_call(
        functools.partial(_hgrn_proj_kernel, li=li),
        grid=(t // TM_STREAM, per_part),
        in_specs=[
            pl.BlockSpec((TM_STREAM, d), lambda i, j: (i, 0)),
            pl.BlockSpec((None, 1, d), lambda i, j: (li, 0, 0)),
            w_spec(0), w_spec(1), w_spec(2), w_spec(3),
            pl.BlockSpec((lb_logits.shape[0], TNQ), lambda i, j: (0, j)),
        ],
        out_specs=[o_spec, o_spec, o_spec, o_spec],
        out_shape=[bf, jax.ShapeDtypeStruct((t, d_b), F32), bf, bf],
        scratch_shapes=[pltpu.VMEM((TM_STREAM, d), BF16)],
        compiler_params=_cparams(("parallel", "arbitrary")),
        name="hgrn_proj",
    )(h, norm_g, w_in, w_in, w_in, w_in, lb_logits)


def _hgrn_halves():
    halves, half = [], HGRN_BLOCK // 2
    while half >= 1:
        halves.append(half)
        half //= 2
    return tuple(halves)


def _hgrn_norm_gate(o, ng, gate):
    ms = jnp.mean(o * o, axis=-1, keepdims=True)
    return (o * lax.rsqrt(ms + EPS) * ng * gate).astype(BF16)


def _hgrn_prompt_kernel(f_ref, q_ref, i_ref, g_ref, ng_ref, og_ref, sfin_ref,
                        st_ref, tril_ref, sgn_ref, mask_ref, *, nblk, hps):
    c = HGRN_BLOCK
    w = hps * LANES
    halves = _hgrn_halves()
    nl = len(halves)

    @pl.when((pl.program_id(0) == 0) & (pl.program_id(1) == 0))
    def _():
        r = lax.broadcasted_iota(jnp.int32, (c, c), 0)
        s = lax.broadcasted_iota(jnp.int32, (c, c), 1)
        tril_ref[...] = jnp.where(s <= r, 1.0, 0.0).astype(BF16)
        for lvl, half in enumerate(halves):
            par = 2 * half
            second = (r & half) != 0
            sgn_ref[lvl] = jnp.where(second, 1.0, -1.0)
            valid = second & ((s & half) == 0) & ((r // par) == (s // par))
            mask_ref[lvl] = jnp.where(valid, 1.0, 0.0).astype(BF16)
        mask_ref[nl] = jnp.where(r == s, 1.0, 0.0).astype(BF16)

    st_ref[...] = jnp.zeros_like(st_ref)
    low_sub = lax.broadcasted_iota(jnp.int32, (c // SUBLANES, SUBLANES, w), 1) < SUBLANES // 2

    def body(blk, carry):
        rows = pl.ds(pl.multiple_of(blk * c, c), c)
        f = f_ref[rows, :]
        qb = q_ref[rows, :]
        vb = i_ref[rows, :]
        q = qb.astype(F32)
        k = 1.0 - f
        kb = k.astype(BF16)

        x = jnp.log(f)
        hi = x.astype(BF16)
        r1 = x - hi.astype(F32)
        mid = r1.astype(BF16)
        lo = (r1 - mid.astype(F32)).astype(BF16)
        tril = tril_ref[...]
        b = _dot(tril, hi) + _dot(tril, mid) + _dot(tril, lo)
        b3 = b.reshape(c // SUBLANES, SUBLANES, w)

        att = [None] * hps

        def add_level(lvl, qt, kt):
            for hd in range(hps):
                cols = slice(hd * LANES, (hd + 1) * LANES)
                a = _dot_nt(qt[:, cols], kt[:, cols]).astype(BF16) * mask_ref[lvl]
                att[hd] = a if att[hd] is None else att[hd] + a

        for lvl, half in enumerate(halves):
            par = 2 * half
            if half == 1:
                add_level(lvl, (q * f).astype(BF16), kb)
                continue
            if half >= SUBLANES:
                pieces = [jnp.broadcast_to(b[p * par + half - 1:p * par + half, :], (par, w))
                          for p in range(c // par)]
                beta = pieces[0] if len(pieces) == 1 else jnp.concatenate(pieces, axis=0)
            elif half == SUBLANES // 2:
                beta = jnp.broadcast_to(b3[:, half - 1:half, :], b3.shape).reshape(c, w)
            else:
                beta = jnp.where(low_sub, b3[:, half - 1:half, :],
                                 b3[:, par + half - 1:par + half, :]).reshape(c, w)
            sgn = jnp.concatenate([sgn_ref[lvl]] * hps, axis=1)
            e = jnp.exp((b - beta) * sgn).astype(BF16)
            add_level(lvl, qb * e, kb * e)
        add_level(nl, qb, kb)

        qe = (q * jnp.exp(b)).astype(BF16)
        b_last = b[c - 1:c, :]
        kd = (k * jnp.exp(b_last - b)).astype(BF16)
        e_last = jnp.exp(b_last)
        gate = g_ref[rows, :].astype(F32)
        ng = ng_ref[...]
        for hd in range(hps):
            cols = slice(hd * LANES, (hd + 1) * LANES)
            st = st_ref[hd]
            o = _dot_nt(qe[:, cols], st.astype(BF16)) + _dot(att[hd], vb[:, cols])
            og_ref[rows, cols] = _hgrn_norm_gate(o, ng[:, cols], gate[:, cols])
            st_ref[hd] = st * e_last[:, cols] + _dot_tn(vb[:, cols], kd[:, cols])
        return carry

    lax.fori_loop(0, nblk, body, 0, unroll=2)
    for hd in range(hps):
        sfin_ref[hd] = st_ref[hd].T


def _hgrn_prompt(f, q, i, g, norm_g, batch, seq, heads):
    hps = HGRN_HEADS_PER_STEP
    w = hps * LANES
    ngrp = heads // hps
    nblk = seq // HGRN_BLOCK
    nl = len(_hgrn_halves())
    part = pl.BlockSpec((seq, w), lambda s, h: (s, h))
    return pl.pallas_call(
        functools.partial(_hgrn_prompt_kernel, nblk=nblk, hps=hps),
        grid=(batch, ngrp),
        in_specs=[part, part, part, part,
                  pl.BlockSpec((1, w), lambda s, h: (0, h))],
        out_specs=[pl.BlockSpec((seq, w), lambda s, h: (s, h)),
                   pl.BlockSpec((None, hps, LANES, LANES), lambda s, h: (s, h, 0, 0))],
        out_shape=[jax.ShapeDtypeStruct((batch * seq, heads * LANES), BF16),
                   jax.ShapeDtypeStruct((batch, heads, LANES, LANES), F32)],
        scratch_shapes=[pltpu.VMEM((hps, LANES, LANES), F32),
                        pltpu.VMEM((HGRN_BLOCK, HGRN_BLOCK), BF16),
                        pltpu.VMEM((nl, HGRN_BLOCK, LANES), F32),
                        pltpu.VMEM((nl + 1, HGRN_BLOCK, HGRN_BLOCK), BF16)],
        compiler_params=_cparams(("arbitrary", "arbitrary")),
        name="hgrn_prompt",
    )(f, q, i, g, norm_g)


def _hgrn_sample_kernel(f_ref, q_ref, i_ref, g_ref, ng_ref, s0_ref, og_ref, s1_ref, *, heads, dec_seq):
    rows_per_tile = SUBLANES
    per_tile = rows_per_tile // dec_seq
    row = lax.broadcasted_iota(jnp.int32, (rows_per_tile, LANES), 0)
    pos = row % dec_seq
    first = row < dec_seq
    zpad = jnp.zeros((rows_per_tile, LANES), F32)
    prow = lax.broadcasted_iota(jnp.int32, (4 * rows_per_tile, LANES), 0)
    in_pieces = prow < 3 * rows_per_tile
    sel_a = jnp.where(in_pieces & (prow % rows_per_tile == dec_seq - 1), 1.0, 0.0).astype(BF16)
    sel_b = jnp.where(in_pieces & (prow % rows_per_tile == 2 * dec_seq - 1), 1.0, 0.0).astype(BF16)
    f_all = f_ref[...]
    q_all = q_ref[...].astype(F32)
    v_all = i_ref[...].astype(F32)
    g_all = g_ref[...].astype(F32)
    ng = ng_ref[...]

    def pick(x, s):
        return jnp.where(first, x[s:s + 1, :], x[dec_seq + s:dec_seq + s + 1, :])

    for tile in range(SAMPLE_BATCH_TILE // per_tile):
        rows = slice(tile * rows_per_tile, (tile + 1) * rows_per_tile)
        for h in range(heads):
            cols = slice(h * LANES, (h + 1) * LANES)
            q = q_all[rows, cols]
            f = f_all[rows, cols]
            v = v_all[rows, cols]
            k = 1.0 - f
            b = jnp.log(f)
            shift = 1
            while shift < dec_seq:
                b = b + jnp.where(pos >= shift, pltpu.roll(b, shift, 0), 0.0)
                shift *= 2
            eb = jnp.exp(b)
            qe = (q * eb).astype(BF16)
            s_a = s0_ref[tile * per_tile, h]
            s_b = s0_ref[tile * per_tile + 1, h]
            o = jnp.where(first, _dot(qe, s_a.astype(BF16)), _dot(qe, s_b.astype(BF16)))
            for s in range(dec_seq):
                e = jnp.exp(jnp.minimum(b - pick(b, s), 0.0))
                wgt = jnp.sum(q * e * pick(k, s), axis=-1, keepdims=True)
                o = o + jnp.where(pos >= s, wgt, 0.0) * pick(v, s)
            og_ref[rows, cols] = _hgrn_norm_gate(o, ng[:, cols], g_all[rows, cols])

            b_last = pick(b, dec_seq - 1)
            kd = jnp.concatenate([k * jnp.exp(b_last - b), zpad], axis=0).astype(BF16)
            v_a = jnp.concatenate([jnp.where(first, v, 0.0), zpad], axis=0).astype(BF16)
            v_b = jnp.concatenate([jnp.where(first, 0.0, v), zpad], axis=0).astype(BF16)
            hi = eb.astype(BF16).astype(F32)
            mid = (eb - hi).astype(BF16).astype(F32)
            lo = eb - hi - mid
            pieces = jnp.concatenate([hi, mid, lo, zpad], axis=0).astype(BF16)
            s1_ref[tile * per_tile, h] = s_a * _dot_tn(pieces, sel_a) + _dot_tn(kd, v_a)
            s1_ref[tile * per_tile + 1, h] = s_b * _dot_tn(pieces, sel_b) + _dot_tn(kd, v_b)


def _hgrn_sample(f, q, i, g, norm_g, s0, n_prompt, dec_batch, dec_seq, heads):
    rows = SAMPLE_BATCH_TILE * dec_seq
    first_blk = n_prompt // rows
    d_b = heads * LANES
    part = pl.BlockSpec((rows, d_b), lambda i: (first_blk + i, 0))
    return pl.pallas_call(
        functools.partial(_hgrn_sample_kernel, heads=heads, dec_seq=dec_seq),
        grid=(dec_batch // SAMPLE_BATCH_TILE,),
        in_specs=[part, part, part, part,
                  _resident((1, d_b), lambda i: (0, 0)),
                  pl.BlockSpec((SAMPLE_BATCH_TILE, heads, LANES, LANES), lambda i: (i, 0, 0, 0))],
        out_specs=[pl.BlockSpec((rows, d_b), lambda i: (i, 0)),
                   pl.BlockSpec((SAMPLE_BATCH_TILE, heads, LANES, LANES), lambda i: (i, 0, 0, 0))],
        out_shape=[jax.ShapeDtypeStruct((dec_batch * dec_seq, d_b), BF16),
                   jax.ShapeDtypeStruct(s0.shape, F32)],
        compiler_params=_cparams(("parallel",)),
        name="hgrn_sample",
    )(f, q, i, g, norm_g, s0)


def _out_proj_kernel(h_ref, xa_ref, xb_ref, w_ref, o_ref, *, na):
    x = jnp.where(pl.program_id(0) < na, xa_ref[...], xb_ref[...])
    o_ref[...] = h_ref[...] + _dot(x, w_ref[...])


def _out_proj(h, xa, xb, w, j_layer):
    t, d = h.shape
    kdim = xa.shape[1]
    na = xa.shape[0] // TM
    return pl.pallas_call(
        functools.partial(_out_proj_kernel, na=na),
        grid=(t // TM,),
        in_specs=[
            pl.BlockSpec((TM, d), lambda i: (i, 0)),
            pl.BlockSpec((TM, kdim), lambda i: (jnp.minimum(i, na - 1), 0)),
            pl.BlockSpec((TM, kdim), lambda i: (jnp.maximum(i - na, 0), 0)),
            _resident((None, kdim, d), lambda i: (j_layer, 0, 0)),
        ],
        out_specs=pl.BlockSpec((TM, d), lambda i: (i, 0)),
        out_shape=jax.ShapeDtypeStruct((t, d), F32),
        compiler_params=_cparams(("parallel",)),
        name="hgrn_out_proj",
    )(h, xa, xb, w)


def kernel(x_prompt, x_sample, state_hgrn, p_prompt, p_sample, ffn1_w_in, ffn1_w_out, ffn2_w_in, ffn2_w_out, norm_ffn1, norm_mix, norm_ffn2, norm_ple, a_w_in, a_ln_g, a_ln_b, a_w_s, a_b_s, a_w_out, b_w_in, b_lb_logits, b_norm_g, b_w_out, ple_w_proj, ple_w_gate, final_norm):
    batch, seq, d = x_prompt.shape
    dec_batch, dec_seq, _ = x_sample.shape
    depth = ffn1_w_in.shape[0]
    n_prompt = batch * seq
    n_sample = dec_batch * dec_seq
    t = n_prompt + n_sample
    heads = b_norm_g.shape[-1] // LANES
    assert n_prompt % TM == 0 and n_sample % TM == 0 and t % TM_STREAM == 0
    assert seq % HGRN_BLOCK == 0 and heads % HGRN_HEADS_PER_STEP == 0
    assert CHUNK_A % dec_seq == 0 and SUBLANES % dec_seq == 0 and dec_batch % SAMPLE_BATCH_TILE == 0
    assert a_w_s.shape[-1] == CHUNK_A and seq % CHUNK_A == 0

    p_a = p_prompt.reshape(depth, n_prompt, -1)
    p_b = p_sample.reshape(depth, n_sample, -1)

    bf = lambda w: w.astype(BF16)
    vec = lambda g: g.reshape(g.shape[0], 1, g.shape[-1])
    a_w_out_b, b_w_out_b = bf(a_w_out), bf(b_w_out)
    ple_w_proj_b, ple_w_gate_b = bf(ple_w_proj), bf(ple_w_gate)
    n_ffn1, n_mix, n_ffn2, n_ple = vec(norm_ffn1), vec(norm_mix), vec(norm_ffn2), vec(norm_ple)
    ln_g, ln_b = vec(a_ln_g), vec(a_ln_b)
    final_g = final_norm.reshape(1, d)

    reps = CHUNK_A // dec_seq
    tri = jnp.tril(jnp.ones((CHUNK_A, CHUNK_A), bool))
    pos_c = jnp.arange(CHUNK_A) // dec_seq
    blockdiag = pos_c[:, None] == pos_c[None, :]
    w_prompt = jnp.where(tri, a_w_s, 0.0)
    w_sample = jnp.where(tri & blockdiag, jnp.tile(a_w_s[:, :, :dec_seq, :dec_seq], (1, 1, reps, reps)), 0.0)
    wmix = jnp.stack([w_prompt, w_sample], axis=1).astype(BF16)
    b_prompt = jnp.swapaxes(a_b_s, 1, 2)
    b_sample = jnp.tile(b_prompt[:, :dec_seq], (1, reps, 1))
    bias = jnp.repeat(jnp.stack([b_prompt, b_sample], axis=1), LANES, axis=-1)

    new_v, s_prompt, s_sample = [], [], []
    h = None
    for li in range(depth):
        if li == 0:
            h = _ffn(x_prompt.reshape(n_prompt, d), x_sample.reshape(n_sample, d), n_ffn1, ffn1_w_in, ffn1_w_out, li)
        else:
            h = _ffn(h, None, n_ffn1, ffn1_w_in, ffn1_w_out, li)
        j = li // 2
        if li % 2 == 0:
            u, v = _gmlp_proj(h, n_mix, a_w_in, li, j)
            new_v.append(_gmlp_sample_v(v, ln_g, ln_b, j, n_prompt, n_sample).reshape(dec_batch, dec_seq, -1))
            h = _gmlp_mix(h, u, v, ln_g, ln_b, wmix[j], bias[j], a_w_out_b, j, n_prompt // TM)
        else:
            q, f, iv, g = _hgrn_proj(h, n_mix, b_w_in, b_lb_logits, li, j)
            ng = b_norm_g[j].reshape(1, heads * LANES)
            og_p, s_fin = _hgrn_prompt(f, q, iv, g, ng, batch, seq, heads)
            og_s, s_new = _hgrn_sample(f, q, iv, g, ng, state_hgrn[j], n_prompt, dec_batch, dec_seq, heads)
            s_prompt.append(s_fin)
            s_sample.append(s_new)
            h = _out_proj(h, og_p, og_s, b_w_out_b, j)
        h = _ffn(h, None, n_ffn2, ffn2_w_in, ffn2_w_out, li)
        h = _ple(h, p_a, p_b, n_ple, ple_w_gate_b, ple_w_proj_b, final_g, li, final=(li == depth - 1))

    y_prompt = h[0].reshape(batch, seq, d)
    y_sample = h[1].reshape(dec_batch, dec_seq, d)
    return (y_prompt, y_sample, jnp.stack(s_prompt), jnp.stack(s_sample), jnp.stack(new_v))
```

```python
import functools

import jax
import jax.numpy as jnp
from jax import lax
from jax.experimental import pallas as pl
from jax.experimental.pallas import tpu as pltpu

F32 = jnp.float32
BF16 = jnp.bfloat16
EPS = 1e-6

LANES = 128
SUBLANES = 8
VMEM_LIMIT_BYTES = 56 * 1024 * 1024

TM = 512
TM_STREAM = 1088
TF = 256
WEIGHT_SLOTS = 3
TN = 512
TNQ = 256
HGRN_BLOCK = 128
HGRN_HEADS_PER_STEP = 4
CHUNK_A = 128
SAMPLE_BATCH_TILE = 4


def _cparams(sem):
    return pltpu.CompilerParams(dimension_semantics=sem, vmem_limit_bytes=VMEM_LIMIT_BYTES)


def _rms(x, g):
    ms = jnp.mean(x * x, axis=-1, keepdims=True)
    return x * lax.rsqrt(ms + EPS) * g


def _resident(shape, index_map):
    return pl.BlockSpec(shape, index_map, pipeline_mode=pl.Buffered(1))


def _dot(a, b):
    return jnp.dot(a, b, preferred_element_type=F32)


def _dot_nt(a, b):
    return lax.dot_general(a, b, (((1,), (1,)), ((), ())), preferred_element_type=F32)


def _dot_tn(a, b):
    return lax.dot_general(a, b, (((0,), (0,)), ((), ())), preferred_element_type=F32)


def _ffn_kernel(xa_ref, xb_ref, g_ref, w_in_ref, w_out_ref, o_ref,
                xbuf_ref, xn_ref, wg_ref, wu_ref, wo_ref, sem, wsem, *, n_a, n_b, li, nf):
    tm = o_ref.shape[0]
    tf = wg_ref.shape[2]
    full_a, rem_a = divmod(n_a, tm)
    n_tiles = (n_a + n_b) // tm
    n_steps = n_tiles * nf
    step = pl.program_id(0) * nf + pl.program_id(1)

    def weight_copies(s, act):
        slot = s % WEIGHT_SLOTS
        col = (s % nf) * tf
        act(pltpu.make_async_copy(w_in_ref.at[li, :, pl.ds(col, tf)], wg_ref.at[slot], wsem.at[slot, 0]))
        act(pltpu.make_async_copy(w_in_ref.at[li, :, pl.ds(nf * tf + col, tf)], wu_ref.at[slot], wsem.at[slot, 1]))
        act(pltpu.make_async_copy(w_out_ref.at[li, pl.ds(col, tf), :], wo_ref.at[slot], wsem.at[slot, 2]))

    @pl.when(step == 0)
    def _():
        for s in range(WEIGHT_SLOTS - 1):
            weight_copies(s, lambda cp: cp.start())

    @pl.when(step + WEIGHT_SLOTS - 1 < n_steps)
    def _():
        weight_copies(step + WEIGHT_SLOTS - 1, lambda cp: cp.start())

    def tile_copies(tile, act):
        @pl.when(tile < full_a)
        def _():
            act(pltpu.make_async_copy(xa_ref.at[pl.ds(tile * tm, tm)], xbuf_ref, sem.at[0]))

        if n_b:
            @pl.when(tile == full_a)
            def _():
                if rem_a:
                    act(pltpu.make_async_copy(xa_ref.at[pl.ds(full_a * tm, rem_a)],
                                              xbuf_ref.at[pl.ds(0, rem_a)], sem.at[0]))
                act(pltpu.make_async_copy(xb_ref, xbuf_ref.at[pl.ds(rem_a, n_b)], sem.at[1]))

    @pl.when(pl.program_id(1) == 0)
    def _():
        i = pl.program_id(0)

        @pl.when(i == 0)
        def _():
            tile_copies(i, lambda cp: cp.start())

        tile_copies(i, lambda cp: cp.wait())
        x = xbuf_ref[...]
        o_ref[...] = x
        xn_ref[...] = _rms(x, g_ref[...]).astype(BF16)

        @pl.when(i + 1 < n_tiles)
        def _():
            tile_copies(i + 1, lambda cp: cp.start())

    weight_copies(step, lambda cp: cp.wait())
    slot = step % WEIGHT_SLOTS
    xn = xn_ref[...]
    gate = _dot(xn, wg_ref[slot].astype(BF16))
    up = _dot(xn, wu_ref[slot].astype(BF16))
    act = (0.5 * gate * jax.nn.sigmoid(gate) * up).astype(BF16)
    o_ref[...] += _dot(act, wo_ref[slot].astype(BF16))


def _ffn(xa, xb, norm_g, w_in, w_out, li):
    n_a, d = xa.shape
    n_b = 0 if xb is None else xb.shape[0]
    t = n_a + n_b
    assert t % TM_STREAM == 0 and (n_b == 0 or n_a % TM_STREAM + n_b == TM_STREAM)
    d_ff = w_out.shape[1]
    nf = d_ff // TF
    assert (t // TM_STREAM) * nf >= WEIGHT_SLOTS - 1
    return pl.pallas_call(
        functools.partial(_ffn_kernel, n_a=n_a, n_b=n_b, li=li, nf=nf),
        grid=(t // TM_STREAM, nf),
        in_specs=[
            pl.BlockSpec(memory_space=pl.ANY),
            pl.BlockSpec(memory_space=pl.ANY),
            pl.BlockSpec((None, 1, d), lambda i, j: (li, 0, 0)),
            pl.BlockSpec(memory_space=pl.ANY),
            pl.BlockSpec(memory_space=pl.ANY),
        ],
        out_specs=pl.BlockSpec((TM_STREAM, d), lambda i, j: (i, 0)),
        out_shape=jax.ShapeDtypeStruct((t, d), F32),
        scratch_shapes=[pltpu.VMEM((TM_STREAM, d), F32), pltpu.VMEM((TM_STREAM, d), BF16),
                        pltpu.VMEM((WEIGHT_SLOTS, d, TF), F32), pltpu.VMEM((WEIGHT_SLOTS, d, TF), F32),
                        pltpu.VMEM((WEIGHT_SLOTS, TF, d), F32),
                        pltpu.SemaphoreType.DMA((2,)), pltpu.SemaphoreType.DMA((WEIGHT_SLOTS, 3))],
        compiler_params=_cparams(("arbitrary", "arbitrary")),
        name=f"ffn_l{li}",
    )(xa, xa if xb is None else xb, norm_g, w_in, w_out)


def _ple_kernel(h_ref, pa_ref, pb_ref, g_ref, wgate_ref, wproj_ref, fn_ref, *o_refs, final, na):
    i = pl.program_id(0)
    h = h_ref[...]
    hn = _rms(h, g_ref[...]).astype(BF16)
    gate = jax.nn.sigmoid(_dot(hn, wgate_ref[...]))
    p = jnp.where(i < na, pa_ref[...], pb_ref[...])
    proj = _dot(p.astype(BF16), wproj_ref[...])
    out = h + proj * gate
    if not final:
        o_refs[0][...] = out
        return
    out = _rms(out, fn_ref[...])

    @pl.when(i < na)
    def _():
        o_refs[0][...] = out

    @pl.when(i >= na)
    def _():
        o_refs[1][...] = out


def _ple(h, pa, pb, norm_g, w_gate, w_proj, final_g, li, final):
    t, d = h.shape
    dp = pa.shape[-1]
    na = pa.shape[1] // TM
    in_a = lambda i: jnp.minimum(i, na - 1)
    in_b = lambda i: jnp.maximum(i - na, 0)
    if final:
        out_specs = [pl.BlockSpec((TM, d), lambda i: (in_a(i), 0)), pl.BlockSpec((TM, d), lambda i: (in_b(i), 0))]
        out_shape = [jax.ShapeDtypeStruct((pa.shape[1], d), F32), jax.ShapeDtypeStruct((pb.shape[1], d), F32)]
    else:
        out_specs = pl.BlockSpec((TM, d), lambda i: (i, 0))
        out_shape = jax.ShapeDtypeStruct((t, d), F32)
    return pl.pallas_call(
        functools.partial(_ple_kernel, final=final, na=na),
        grid=(t // TM,),
        in_specs=[
            pl.BlockSpec((TM, d), lambda i: (i, 0)),
            pl.BlockSpec((None, TM, dp), lambda i: (li, in_a(i), 0)),
            pl.BlockSpec((None, TM, dp), lambda i: (li, in_b(i), 0)),
            _resident((None, 1, d), lambda i: (li, 0, 0)),
            _resident((None, d, d), lambda i: (li, 0, 0)),
            _resident((None, dp, d), lambda i: (li, 0, 0)),
            _resident((1, d), lambda i: (0, 0)),
        ],
        out_specs=out_specs,
        out_shape=out_shape,
        compiler_params=_cparams(("arbitrary",)),
        name=f"ple_l{li}",
    )(h, pa, pb, norm_g, w_gate, w_proj, final_g)


def _gelu(z):
    return 0.5 * z * (1.0 + lax.erf(z * (2.0 ** -0.5)))


def _gmlp_proj_kernel(x_ref, g_ref, w_ref, lng_ref, lnb_ref, o_ref, xn_ref, *, layer_norm):
    @pl.when(pl.program_id(1) == 0)
    def _():
        xn_ref[...] = _rms(x_ref[...], g_ref[...]).astype(BF16)

    z = _gelu(_dot(xn_ref[...], w_ref[...]))
    if layer_norm:
        mu = jnp.mean(z, axis=-1, keepdims=True)
        zc = z - mu
        var = jnp.mean(zc * zc, axis=-1, keepdims=True)
        z = zc * lax.rsqrt(var + EPS) * lng_ref[...] + lnb_ref[...]
    o_ref[...] = z.astype(o_ref.dtype)


def _gmlp_proj(h, norm_g, w_in, ln_g, ln_b, li, j_layer, *, half, layer_norm, out_dtype):
    t, d = h.shape
    d_a = w_in.shape[-1] // 2
    tn = d_a
    nn = d_a // tn
    off = half * nn
    return pl.pallas_call(
        functools.partial(_gmlp_proj_kernel, layer_norm=layer_norm),
        grid=(t // TM, nn),
        in_specs=[
            pl.BlockSpec((TM, d), lambda i, j: (i, 0)),
            pl.BlockSpec((None, 1, d), lambda i, j: (li, 0, 0)),
            pl.BlockSpec((None, d, tn), lambda i, j: (j_layer, 0, j + off)),
            pl.BlockSpec((None, 1, tn), lambda i, j: (j_layer, 0, j)),
            pl.BlockSpec((None, 1, tn), lambda i, j: (j_layer, 0, j)),
        ],
        out_specs=pl.BlockSpec((TM, tn), lambda i, j: (i, j)),
        out_shape=jax.ShapeDtypeStruct((t, d_a), out_dtype),
        scratch_shapes=[pltpu.VMEM((TM, d), BF16)],
        compiler_params=_cparams(("parallel", "arbitrary")),
        name=f"gmlp_proj_l{li}_{'v' if half else 'u'}",
    )(h, norm_g, w_in, ln_g, ln_b)


def _gmlp_mix_kernel(h_ref, u_ref, v_ref, wmix_ref, bias_ref, wo_ref, o_ref, t_ref, *, groups):
    for c in range(TM // CHUNK_A):
        rows = slice(c * CHUNK_A, (c + 1) * CHUNK_A)
        for g in range(groups):
            cols = slice(g * LANES, (g + 1) * LANES)
            s = _dot(wmix_ref[g], v_ref[rows, cols].astype(BF16)) + bias_ref[:, cols]
            t_ref[rows, cols] = (u_ref[rows, cols].astype(F32) * s).astype(BF16)
    o_ref[...] = h_ref[...] + _dot(t_ref[...], wo_ref[...])


def _gmlp_mix(h, u, v, wmix, bias, w_out, j_layer, n_prompt_tiles):
    t, d = h.shape
    d_a = u.shape[1]
    groups = d_a // LANES
    return pl.pallas_call(
        functools.partial(_gmlp_mix_kernel, groups=groups),
        grid=(t // TM,),
        in_specs=[
            pl.BlockSpec((TM, d), lambda i: (i, 0)),
            pl.BlockSpec((TM, d_a), lambda i: (i, 0)),
            pl.BlockSpec((TM, d_a), lambda i: (i, 0)),
            pl.BlockSpec((None, groups, CHUNK_A, CHUNK_A), lambda i: (i // n_prompt_tiles, 0, 0, 0)),
            pl.BlockSpec((None, CHUNK_A, d_a), lambda i: (i // n_prompt_tiles, 0, 0)),
            _resident((None, d_a, d), lambda i: (j_layer, 0, 0)),
        ],
        out_specs=pl.BlockSpec((TM, d), lambda i: (i, 0)),
        out_shape=jax.ShapeDtypeStruct((t, d), F32),
        scratch_shapes=[pltpu.VMEM((TM, d_a), BF16)],
        compiler_params=_cparams(("parallel",)),
        name="gmlp_mix",
    )(h, u, v, wmix, bias, w_out)


def _hgrn_proj_kernel(x_ref, g_ref, wq_ref, wf_ref, wi_ref, wg_ref, lbl_ref,
                      q_ref, f_ref, i_ref, gt_ref, xn_ref, *, li):
    @pl.when(pl.program_id(1) == 0)
    def _():
        xn_ref[...] = _rms(x_ref[...], g_ref[...]).astype(BF16)

    xn = xn_ref[...]
    zq = _dot(xn, wq_ref[...].astype(BF16))
    q_ref[...] = (zq * jax.nn.sigmoid(zq)).astype(BF16)

    logits = lbl_ref[...]
    ex = jnp.exp(logits - jnp.max(logits, axis=0, keepdims=True))
    sm = ex / jnp.sum(ex, axis=0, keepdims=True)
    lb = jnp.sum(sm[: li + 1], axis=0, keepdims=True) - sm[0:1]
    zf = _dot(xn, wf_ref[...].astype(BF16))
    f_ref[...] = lb + (1.0 - lb) * jax.nn.sigmoid(zf)

    i_ref[...] = _dot(xn, wi_ref[...].astype(BF16)).astype(BF16)

    zg = _dot(xn, wg_ref[...].astype(BF16))
    gt_ref[...] = (zg * jax.nn.sigmoid(zg)).astype(BF16)


def _hgrn_proj(h, norm_g, w_in, lb_logits, li, j_layer):
    t, d = h.shape
    d_b = w_in.shape[-1] // 4
    per_part = d_b // TNQ
    w_spec = lambda p: pl.BlockSpec((None, d, TNQ), lambda i, j, p=p: (j_layer, 0, p * per_part + j))
    o_spec = pl.BlockSpec((TM_STREAM, TNQ), lambda i, j: (i, j))
    bf = jax.ShapeDtypeStruct((t, d_b), BF16)
    return pl.pallas_call(
        functools.partial(_hgrn_proj_kernel, li=li),
        grid=(t // TM_STREAM, per_part),
        in_specs=[
            pl.BlockSpec((TM_STREAM, d), lambda i, j: (i, 0)),
            pl.BlockSpec((None, 1, d), lambda i, j: (li, 0, 0)),
            w_spec(0), w_spec(1), w_spec(2), w_spec(3),
            pl.BlockSpec((lb_logits.shape[0], TNQ), lambda i, j: (0, j)),
        ],
        out_specs=[o_spec, o_spec, o_spec, o_spec],
        out_shape=[bf, jax.ShapeDtypeStruct((t, d_b), F32), bf, bf],
        scratch_shapes=[pltpu.VMEM((TM_STREAM, d), BF16)],
        compiler_params=_cparams(("parallel", "arbitrary")),
        name="hgrn_proj",
    )(h, norm_g, w_in, w_in, w_in, w_in, lb_logits)


def _hgrn_halves():
    halves, half = [], HGRN_BLOCK // 2
    while half >= 1:
        halves.append(half)
        half //= 2
    return tuple(halves)


def _hgrn_norm_gate(o, ng, gate):
    ms = jnp.mean(o * o, axis=-1, keepdims=True)
    return (o * lax.rsqrt(ms + EPS) * ng * gate).astype(BF16)


def _hgrn_prompt_kernel(f_ref, q_ref, i_ref, g_ref, ng_ref, og_ref, sfin_ref,
                        st_ref, tril_ref, sgn_ref, mask_ref, *, nblk, hps):
    c = HGRN_BLOCK
    w = hps * LANES
    halves = _hgrn_halves()
    nl = len(halves)

    @pl.when((pl.program_id(0) == 0) & (pl.program_id(1) == 0))
    def _():
        r = lax.broadcasted_iota(jnp.int32, (c, c), 0)
        s = lax.broadcasted_iota(jnp.int32, (c, c), 1)
        tril_ref[...] = jnp.where(s <= r, 1.0, 0.0).astype(BF16)
        for lvl, half in enumerate(halves):
            par = 2 * half
            second = (r & half) != 0
            sgn_ref[lvl] = jnp.where(second, 1.0, -1.0)
            valid = second & ((s & half) == 0) & ((r // par) == (s // par))
            mask_ref[lvl] = jnp.where(valid, 1.0, 0.0).astype(BF16)
        mask_ref[nl] = jnp.where(r == s, 1.0, 0.0).astype(BF16)

    st_ref[...] = jnp.zeros_like(st_ref)
    low_sub = lax.broadcasted_iota(jnp.int32, (c // SUBLANES, SUBLANES, w), 1) < SUBLANES // 2

    def body(blk, carry):
        rows = pl.ds(pl.multiple_of(blk * c, c), c)
        f = f_ref[rows, :]
        qb = q_ref[rows, :]
        vb = i_ref[rows, :]
        q = qb.astype(F32)
        k = 1.0 - f
        kb = k.astype(BF16)

        x = jnp.log(f)
        hi = x.astype(BF16)
        r1 = x - hi.astype(F32)
        mid = r1.astype(BF16)
        lo = (r1 - mid.astype(F32)).astype(BF16)
        tril = tril_ref[...]
        b = _dot(tril, hi) + _dot(tril, mid) + _dot(tril, lo)
        b3 = b.reshape(c // SUBLANES, SUBLANES, w)

        att = [None] * hps

        def add_level(lvl, qt, kt):
            for hd in range(hps):
                cols = slice(hd * LANES, (hd + 1) * LANES)
                a = _dot_nt(qt[:, cols], kt[:, cols]).astype(BF16) * mask_ref[lvl]
                att[hd] = a if att[hd] is None else att[hd] + a

        for lvl, half in enumerate(halves):
            par = 2 * half
            if half == 1:
                add_level(lvl, (q * f).astype(BF16), kb)
                continue
            if half >= SUBLANES:
                pieces = [jnp.broadcast_to(b[p * par + half - 1:p * par + half, :], (par, w))
                          for p in range(c // par)]
                beta = pieces[0] if len(pieces) == 1 else jnp.concatenate(pieces, axis=0)
            elif half == SUBLANES // 2:
                beta = jnp.broadcast_to(b3[:, half - 1:half, :], b3.shape).reshape(c, w)
            else:
                beta = jnp.where(low_sub, b3[:, half - 1:half, :],
                                 b3[:, par + half - 1:par + half, :]).reshape(c, w)
            sgn = jnp.concatenate([sgn_ref[lvl]] * hps, axis=1)
            e = jnp.exp((b - beta) * sgn).astype(BF16)
            add_level(lvl, qb * e, kb * e)
        add_level(nl, qb, kb)

        qe = (q * jnp.exp(b)).astype(BF16)
        b_last = b[c - 1:c, :]
        kd = (k * jnp.exp(b_last - b)).astype(BF16)
        e_last = jnp.exp(b_last)
        gate = g_ref[rows, :].astype(F32)
        ng = ng_ref[...]
        for hd in range(hps):
            cols = slice(hd * LANES, (hd + 1) * LANES)
            st = st_ref[hd]
            o = _dot_nt(qe[:, cols], st.astype(BF16)) + _dot(att[hd], vb[:, cols])
            og_ref[rows, cols] = _hgrn_norm_gate(o, ng[:, cols], gate[:, cols])
            st_ref[hd] = st * e_last[:, cols] + _dot_tn(vb[:, cols], kd[:, cols])
        return carry

    lax.fori_loop(0, nblk, body, 0, unroll=2)
    for hd in range(hps):
        sfin_ref[hd] = st_ref[hd].T


def _hgrn_prompt(f, q, i, g, norm_g, batch, seq, heads):
    hps = HGRN_HEADS_PER_STEP
    w = hps * LANES
    ngrp = heads // hps
    nblk = seq // HGRN_BLOCK
    nl = len(_hgrn_halves())
    part = pl.BlockSpec((seq, w), lambda s, h: (s, h))
    return pl.pallas_call(
        functools.partial(_hgrn_prompt_kernel, nblk=nblk, hps=hps),
        grid=(batch, ngrp),
        in_specs=[part, part, part, part,
                  pl.BlockSpec((1, w), lambda s, h: (0, h))],
        out_specs=[pl.BlockSpec((seq, w), lambda s, h: (s, h)),
                   pl.BlockSpec((None, hps, LANES, LANES), lambda s, h: (s, h, 0, 0))],
        out_shape=[jax.ShapeDtypeStruct((batch * seq, heads * LANES), BF16),
                   jax.ShapeDtypeStruct((batch, heads, LANES, LANES), F32)],
        scratch_shapes=[pltpu.VMEM((hps, LANES, LANES), F32),
                        pltpu.VMEM((HGRN_BLOCK, HGRN_BLOCK), BF16),
                        pltpu.VMEM((nl, HGRN_BLOCK, LANES), F32),
                        pltpu.VMEM((nl + 1, HGRN_BLOCK, HGRN_BLOCK), BF16)],
        compiler_params=_cparams(("arbitrary", "arbitrary")),
        name="hgrn_prompt",
    )(f, q, i, g, norm_g)


def _hgrn_sample_kernel(f_ref, q_ref, i_ref, g_ref, ng_ref, s0_ref, og_ref, s1_ref, *, heads, dec_seq):
    rows_per_tile = SUBLANES
    per_tile = rows_per_tile // dec_seq
    row = lax.broadcasted_iota(jnp.int32, (rows_per_tile, LANES), 0)
    pos = row % dec_seq
    first = row < dec_seq
    zpad = jnp.zeros((rows_per_tile, LANES), F32)
    prow = lax.broadcasted_iota(jnp.int32, (4 * rows_per_tile, LANES), 0)
    in_pieces = prow < 3 * rows_per_tile
    sel_a = jnp.where(in_pieces & (prow % rows_per_tile == dec_seq - 1), 1.0, 0.0).astype(BF16)
    sel_b = jnp.where(in_pieces & (prow % rows_per_tile == 2 * dec_seq - 1), 1.0, 0.0).astype(BF16)
    f_all = f_ref[...]
    q_all = q_ref[...].astype(F32)
    v_all = i_ref[...].astype(F32)
    g_all = g_ref[...].astype(F32)
    ng = ng_ref[...]

    def pick(x, s):
        return jnp.where(first, x[s:s + 1, :], x[dec_seq + s:dec_seq + s + 1, :])

    for tile in range(SAMPLE_BATCH_TILE // per_tile):
        rows = slice(tile * rows_per_tile, (tile + 1) * rows_per_tile)
        for h in range(heads):
            cols = slice(h * LANES, (h + 1) * LANES)
            q = q_all[rows, cols]
            f = f_all[rows, cols]
            v = v_all[rows, cols]
            k = 1.0 - f
            b = jnp.log(f)
            shift = 1
            while shift < dec_seq:
                b = b + jnp.where(pos >= shift, pltpu.roll(b, shift, 0), 0.0)
                shift *= 2
            eb = jnp.exp(b)
            qe = (q * eb).astype(BF16)
            s_a = s0_ref[tile * per_tile, h]
            s_b = s0_ref[tile * per_tile + 1, h]
            o = jnp.where(first, _dot(qe, s_a.astype(BF16)), _dot(qe, s_b.astype(BF16)))
            for s in range(dec_seq):
                e = jnp.exp(jnp.minimum(b - pick(b, s), 0.0))
                wgt = jnp.sum(q * e * pick(k, s), axis=-1, keepdims=True)
                o = o + jnp.where(pos >= s, wgt, 0.0) * pick(v, s)
            og_ref[rows, cols] = _hgrn_norm_gate(o, ng[:, cols], g_all[rows, cols])

            b_last = pick(b, dec_seq - 1)
            kd = jnp.concatenate([k * jnp.exp(b_last - b), zpad], axis=0).astype(BF16)
            v_a = jnp.concatenate([jnp.where(first, v, 0.0), zpad], axis=0).astype(BF16)
            v_b = jnp.concatenate([jnp.where(first, 0.0, v), zpad], axis=0).astype(BF16)
            hi = eb.astype(BF16).astype(F32)
            mid = (eb - hi).astype(BF16).astype(F32)
            lo = eb - hi - mid
            pieces = jnp.concatenate([hi, mid, lo, zpad], axis=0).astype(BF16)
            s1_ref[tile * per_tile, h] = s_a * _dot_tn(pieces, sel_a) + _dot_tn(kd, v_a)
            s1_ref[tile * per_tile + 1, h] = s_b * _dot_tn(pieces, sel_b) + _dot_tn(kd, v_b)


def _hgrn_sample(f, q, i, g, norm_g, s0, n_prompt, dec_batch, dec_seq, heads):
    rows = SAMPLE_BATCH_TILE * dec_seq
    first_blk = n_prompt // rows
    d_b = heads * LANES
    part = pl.BlockSpec((rows, d_b), lambda i: (first_blk + i, 0))
    return pl.pallas_call(
        functools.partial(_hgrn_sample_kernel, heads=heads, dec_seq=dec_seq),
        grid=(dec_batch // SAMPLE_BATCH_TILE,),
        in_specs=[part, part, part, part,
                  _resident((1, d_b), lambda i: (0, 0)),
                  pl.BlockSpec((SAMPLE_BATCH_TILE, heads, LANES, LANES), lambda i: (i, 0, 0, 0))],
        out_specs=[pl.BlockSpec((rows, d_b), lambda i: (i, 0)),
                   pl.BlockSpec((SAMPLE_BATCH_TILE, heads, LANES, LANES), lambda i: (i, 0, 0, 0))],
        out_shape=[jax.ShapeDtypeStruct((dec_batch * dec_seq, d_b), BF16),
                   jax.ShapeDtypeStruct(s0.shape, F32)],
        compiler_params=_cparams(("parallel",)),
        name="hgrn_sample",
    )(f, q, i, g, norm_g, s0)


def _out_proj_kernel(h_ref, xa_ref, xb_ref, w_ref, o_ref, *, na):
    x = jnp.where(pl.program_id(0) < na, xa_ref[...], xb_ref[...])
    o_ref[...] = h_ref[...] + _dot(x, w_ref[...])


def _out_proj(h, xa, xb, w, j_layer):
    t, d = h.shape
    kdim = xa.shape[1]
    na = xa.shape[0] // TM
    return pl.pallas_call(
        functools.partial(_out_proj_kernel, na=na),
        grid=(t // TM,),
        in_specs=[
            pl.BlockSpec((TM, d), lambda i: (i, 0)),
            pl.BlockSpec((TM, kdim), lambda i: (jnp.minimum(i, na - 1), 0)),
            pl.BlockSpec((TM, kdim), lambda i: (jnp.maximum(i - na, 0), 0)),
            _resident((None, kdim, d), lambda i: (j_layer, 0, 0)),
        ],
        out_specs=pl.BlockSpec((TM, d), lambda i: (i, 0)),
        out_shape=jax.ShapeDtypeStruct((t, d), F32),
        compiler_params=_cparams(("parallel",)),
        name="hgrn_out_proj",
    )(h, xa, xb, w)


def kernel(x_prompt, x_sample, state_hgrn, p_prompt, p_sample, ffn1_w_in, ffn1_w_out, ffn2_w_in, ffn2_w_out, norm_ffn1, norm_mix, norm_ffn2, norm_ple, a_w_in, a_ln_g, a_ln_b, a_w_s, a_b_s, a_w_out, b_w_in, b_lb_logits, b_norm_g, b_w_out, ple_w_proj, ple_w_gate, final_norm):
    batch, seq, d = x_prompt.shape
    dec_batch, dec_seq, _ = x_sample.shape
    depth = ffn1_w_in.shape[0]
    n_prompt = batch * seq
    n_sample = dec_batch * dec_seq
    t = n_prompt + n_sample
    heads = b_norm_g.shape[-1] // LANES
    assert n_prompt % TM == 0 and n_sample % TM == 0 and t % TM_STREAM == 0
    assert seq % HGRN_BLOCK == 0 and heads % HGRN_HEADS_PER_STEP == 0
    assert CHUNK_A % dec_seq == 0 and SUBLANES % dec_seq == 0 and dec_batch % SAMPLE_BATCH_TILE == 0
    assert a_w_s.shape[-1] == CHUNK_A and seq % CHUNK_A == 0

    p_a = p_prompt.reshape(depth, n_prompt, -1)
    p_b = p_sample.reshape(depth, n_sample, -1)

    bf = lambda w: w.astype(BF16)
    vec = lambda g: g.reshape(g.shape[0], 1, g.shape[-1])
    a_w_in_b, a_w_out_b, b_w_out_b = bf(a_w_in), bf(a_w_out), bf(b_w_out)
    ple_w_proj_b, ple_w_gate_b = bf(ple_w_proj), bf(ple_w_gate)
    n_ffn1, n_mix, n_ffn2, n_ple = vec(norm_ffn1), vec(norm_mix), vec(norm_ffn2), vec(norm_ple)
    ln_g, ln_b = vec(a_ln_g), vec(a_ln_b)
    final_g = final_norm.reshape(1, d)

    reps = CHUNK_A // dec_seq
    tri = jnp.tril(jnp.ones((CHUNK_A, CHUNK_A), bool))
    pos_c = jnp.arange(CHUNK_A) // dec_seq
    blockdiag = pos_c[:, None] == pos_c[None, :]
    w_prompt = jnp.where(tri, a_w_s, 0.0)
    w_sample = jnp.where(tri & blockdiag, jnp.tile(a_w_s[:, :, :dec_seq, :dec_seq], (1, 1, reps, reps)), 0.0)
    wmix = jnp.stack([w_prompt, w_sample], axis=1).astype(BF16)
    b_prompt = jnp.swapaxes(a_b_s, 1, 2)
    b_sample = jnp.tile(b_prompt[:, :dec_seq], (1, reps, 1))
    bias = jnp.repeat(jnp.stack([b_prompt, b_sample], axis=1), LANES, axis=-1)

    new_v, s_prompt, s_sample = [], [], []
    h = None
    for li in range(depth):
        if li == 0:
            h = _ffn(x_prompt.reshape(n_prompt, d), x_sample.reshape(n_sample, d), n_ffn1, ffn1_w_in, ffn1_w_out, li)
        else:
            h = _ffn(h, None, n_ffn1, ffn1_w_in, ffn1_w_out, li)
        j = li // 2
        if li % 2 == 0:
            u = _gmlp_proj(h, n_mix, a_w_in_b, ln_g, ln_b, li, j, half=0, layer_norm=False, out_dtype=BF16)
            v = _gmlp_proj(h, n_mix, a_w_in_b, ln_g, ln_b, li, j, half=1, layer_norm=True, out_dtype=F32)
            new_v.append(v[n_prompt:].reshape(dec_batch, dec_seq, -1))
            h = _gmlp_mix(h, u, v, wmix[j], bias[j], a_w_out_b, j, n_prompt // TM)
        else:
            q, f, iv, g = _hgrn_proj(h, n_mix, b_w_in, b_lb_logits, li, j)
            ng = b_norm_g[j].reshape(1, heads * LANES)
            og_p, s_fin = _hgrn_prompt(f, q, iv, g, ng, batch, seq, heads)
            og_s, s_new = _hgrn_sample(f, q, iv, g, ng, state_hgrn[j], n_prompt, dec_batch, dec_seq, heads)
            s_prompt.append(s_fin)
            s_sample.append(s_new)
            h = _out_proj(h, og_p, og_s, b_w_out_b, j)
        h = _ffn(h, None, n_ffn2, ffn2_w_in, ffn2_w_out, li)
        h = _ple(h, p_a, p_b, n_ple, ple_w_gate_b, ple_w_proj_b, final_g, li, final=(li == depth - 1))

    y_prompt = h[0].reshape(batch, seq, d)
    y_sample = h[1].reshape(dec_batch, dec_seq, d)
    return (y_prompt, y_sample, jnp.stack(s_prompt), jnp.stack(s_sample), jnp.stack(new_v))
```

```python
import functools

import jax
import jax.numpy as jnp
from jax import lax
from jax.experimental import pallas as pl
from jax.experimental.pallas import tpu as pltpu

F32 = jnp.float32
BF16 = jnp.bfloat16
EPS = 1e-6

LANES = 128
SUBLANES = 8
VMEM_LIMIT_BYTES = 63 * 1024 * 1024

TM = 512
TM_STREAM = 1088
TF = 512
TN = 512
TNQ = 256
HGRN_BLOCK = 128
HGRN_HEADS_PER_STEP = 4
CHUNK_A = 128
SAMPLE_BATCH_TILE = 4


def _cparams(sem):
    return pltpu.CompilerParams(dimension_semantics=sem, vmem_limit_bytes=VMEM_LIMIT_BYTES)


def _rms(x, g):
    ms = jnp.mean(x * x, axis=-1, keepdims=True)
    return x * lax.rsqrt(ms + EPS) * g


def _resident(shape, index_map):
    return pl.BlockSpec(shape, index_map, pipeline_mode=pl.Buffered(1))


def _dot(a, b):
    return jnp.dot(a, b, preferred_element_type=F32)


def _dot_nt(a, b):
    return lax.dot_general(a, b, (((1,), (1,)), ((), ())), preferred_element_type=F32)


def _dot_tn(a, b):
    return lax.dot_general(a, b, (((0,), (0,)), ((), ())), preferred_element_type=F32)


def _ffn_kernel(xa_ref, xb_ref, g_ref, wg_ref, wu_ref, wo_ref, o_ref, xbuf_ref, xn_ref, sem, *, n_a, n_b):
    tm = o_ref.shape[0]
    full_a, rem_a = divmod(n_a, tm)
    n_tiles = (n_a + n_b) // tm

    def tile_copies(tile, act):
        @pl.when(tile < full_a)
        def _():
            act(pltpu.make_async_copy(xa_ref.at[pl.ds(tile * tm, tm)], xbuf_ref, sem.at[0]))

        if n_b:
            @pl.when(tile == full_a)
            def _():
                if rem_a:
                    act(pltpu.make_async_copy(xa_ref.at[pl.ds(full_a * tm, rem_a)],
                                              xbuf_ref.at[pl.ds(0, rem_a)], sem.at[0]))
                act(pltpu.make_async_copy(xb_ref, xbuf_ref.at[pl.ds(rem_a, n_b)], sem.at[1]))

    @pl.when(pl.program_id(1) == 0)
    def _():
        i = pl.program_id(0)

        @pl.when(i == 0)
        def _():
            tile_copies(i, lambda cp: cp.start())

        tile_copies(i, lambda cp: cp.wait())
        x = xbuf_ref[...]
        o_ref[...] = x
        xn_ref[...] = _rms(x, g_ref[...]).astype(BF16)

        @pl.when(i + 1 < n_tiles)
        def _():
            tile_copies(i + 1, lambda cp: cp.start())

    xn = xn_ref[...]
    gate = _dot(xn, wg_ref[...].astype(BF16))
    up = _dot(xn, wu_ref[...].astype(BF16))
    act = (0.5 * gate * jax.nn.sigmoid(gate) * up).astype(BF16)
    o_ref[...] += _dot(act, wo_ref[...].astype(BF16))


def _ffn(xa, xb, norm_g, w_in, w_out, li):
    n_a, d = xa.shape
    n_b = 0 if xb is None else xb.shape[0]
    t = n_a + n_b
    assert t % TM_STREAM == 0 and (n_b == 0 or n_a % TM_STREAM + n_b == TM_STREAM)
    d_ff = w_out.shape[1]
    nf = d_ff // TF
    return pl.pallas_call(
        functools.partial(_ffn_kernel, n_a=n_a, n_b=n_b),
        grid=(t // TM_STREAM, nf),
        in_specs=[
            pl.BlockSpec(memory_space=pl.ANY),
            pl.BlockSpec(memory_space=pl.ANY),
            pl.BlockSpec((None, 1, d), lambda i, j: (li, 0, 0)),
            pl.BlockSpec((None, d, TF), lambda i, j: (li, 0, j)),
            pl.BlockSpec((None, d, TF), lambda i, j: (li, 0, j + nf)),
            pl.BlockSpec((None, TF, d), lambda i, j: (li, j, 0)),
        ],
        out_specs=pl.BlockSpec((TM_STREAM, d), lambda i, j: (i, 0)),
        out_shape=jax.ShapeDtypeStruct((t, d), F32),
        scratch_shapes=[pltpu.VMEM((TM_STREAM, d), F32), pltpu.VMEM((TM_STREAM, d), BF16),
                        pltpu.SemaphoreType.DMA((2,))],
        compiler_params=_cparams(("arbitrary", "arbitrary")),
        name=f"ffn_l{li}",
    )(xa, xa if xb is None else xb, norm_g, w_in, w_in, w_out)


def _ple_kernel(h_ref, pa_ref, pb_ref, g_ref, wgate_ref, wproj_ref, fn_ref, *o_refs, final, na):
    i = pl.program_id(0)
    h = h_ref[...]
    hn = _rms(h, g_ref[...]).astype(BF16)
    gate = jax.nn.sigmoid(_dot(hn, wgate_ref[...]))
    p = jnp.where(i < na, pa_ref[...], pb_ref[...])
    proj = _dot(p.astype(BF16), wproj_ref[...])
    out = h + proj * gate
    if not final:
        o_refs[0][...] = out
        return
    out = _rms(out, fn_ref[...])

    @pl.when(i < na)
    def _():
        o_refs[0][...] = out

    @pl.when(i >= na)
    def _():
        o_refs[1][...] = out


def _ple(h, pa, pb, norm_g, w_gate, w_proj, final_g, li, final):
    t, d = h.shape
    dp = pa.shape[-1]
    na = pa.shape[1] // TM
    in_a = lambda i: jnp.minimum(i, na - 1)
    in_b = lambda i: jnp.maximum(i - na, 0)
    if final:
        out_specs = [pl.BlockSpec((TM, d), lambda i: (in_a(i), 0)), pl.BlockSpec((TM, d), lambda i: (in_b(i), 0))]
        out_shape = [jax.ShapeDtypeStruct((pa.shape[1], d), F32), jax.ShapeDtypeStruct((pb.shape[1], d), F32)]
    else:
        out_specs = pl.BlockSpec((TM, d), lambda i: (i, 0))
        out_shape = jax.ShapeDtypeStruct((t, d), F32)
    return pl.pallas_call(
        functools.partial(_ple_kernel, final=final, na=na),
        grid=(t // TM,),
        in_specs=[
            pl.BlockSpec((TM, d), lambda i: (i, 0)),
            pl.BlockSpec((None, TM, dp), lambda i: (li, in_a(i), 0)),
            pl.BlockSpec((None, TM, dp), lambda i: (li, in_b(i), 0)),
            _resident((None, 1, d), lambda i: (li, 0, 0)),
            _resident((None, d, d), lambda i: (li, 0, 0)),
            _resident((None, dp, d), lambda i: (li, 0, 0)),
            _resident((1, d), lambda i: (0, 0)),
        ],
        out_specs=out_specs,
        out_shape=out_shape,
        compiler_params=_cparams(("arbitrary",)),
        name=f"ple_l{li}",
    )(h, pa, pb, norm_g, w_gate, w_proj, final_g)


def _gelu(z):
    return 0.5 * z * (1.0 + lax.erf(z * (2.0 ** -0.5)))


def _gmlp_proj_kernel(x_ref, g_ref, w_ref, lng_ref, lnb_ref, o_ref, xn_ref, *, layer_norm):
    @pl.when(pl.program_id(1) == 0)
    def _():
        xn_ref[...] = _rms(x_ref[...], g_ref[...]).astype(BF16)

    z = _gelu(_dot(xn_ref[...], w_ref[...]))
    if layer_norm:
        mu = jnp.mean(z, axis=-1, keepdims=True)
        zc = z - mu
        var = jnp.mean(zc * zc, axis=-1, keepdims=True)
        z = zc * lax.rsqrt(var + EPS) * lng_ref[...] + lnb_ref[...]
    o_ref[...] = z.astype(o_ref.dtype)


def _gmlp_proj(h, norm_g, w_in, ln_g, ln_b, li, j_layer, *, half, layer_norm, out_dtype):
    t, d = h.shape
    d_a = w_in.shape[-1] // 2
    tn = d_a
    nn = d_a // tn
    off = half * nn
    return pl.pallas_call(
        functools.partial(_gmlp_proj_kernel, layer_norm=layer_norm),
        grid=(t // TM, nn),
        in_specs=[
            pl.BlockSpec((TM, d), lambda i, j: (i, 0)),
            pl.BlockSpec((None, 1, d), lambda i, j: (li, 0, 0)),
            pl.BlockSpec((None, d, tn), lambda i, j: (j_layer, 0, j + off)),
            pl.BlockSpec((None, 1, tn), lambda i, j: (j_layer, 0, j)),
            pl.BlockSpec((None, 1, tn), lambda i, j: (j_layer, 0, j)),
        ],
        out_specs=pl.BlockSpec((TM, tn), lambda i, j: (i, j)),
        out_shape=jax.ShapeDtypeStruct((t, d_a), out_dtype),
        scratch_shapes=[pltpu.VMEM((TM, d), BF16)],
        compiler_params=_cparams(("parallel", "arbitrary")),
        name=f"gmlp_proj_l{li}_{'v' if half else 'u'}",
    )(h, norm_g, w_in, ln_g, ln_b)


def _gmlp_mix_kernel(h_ref, u_ref, v_ref, wmix_ref, bias_ref, wo_ref, o_ref, t_ref, *, groups):
    for c in range(TM // CHUNK_A):
        rows = slice(c * CHUNK_A, (c + 1) * CHUNK_A)
        for g in range(groups):
            cols = slice(g * LANES, (g + 1) * LANES)
            s = _dot(wmix_ref[g], v_ref[rows, cols].astype(BF16)) + bias_ref[:, cols]
            t_ref[rows, cols] = (u_ref[rows, cols].astype(F32) * s).astype(BF16)
    o_ref[...] = h_ref[...] + _dot(t_ref[...], wo_ref[...])


def _gmlp_mix(h, u, v, wmix, bias, w_out, j_layer, n_prompt_tiles):
    t, d = h.shape
    d_a = u.shape[1]
    groups = d_a // LANES
    return pl.pallas_call(
        functools.partial(_gmlp_mix_kernel, groups=groups),
        grid=(t // TM,),
        in_specs=[
            pl.BlockSpec((TM, d), lambda i: (i, 0)),
            pl.BlockSpec((TM, d_a), lambda i: (i, 0)),
            pl.BlockSpec((TM, d_a), lambda i: (i, 0)),
            pl.BlockSpec((None, groups, CHUNK_A, CHUNK_A), lambda i: (i // n_prompt_tiles, 0, 0, 0)),
            pl.BlockSpec((None, CHUNK_A, d_a), lambda i: (i // n_prompt_tiles, 0, 0)),
            _resident((None, d_a, d), lambda i: (j_layer, 0, 0)),
        ],
        out_specs=pl.BlockSpec((TM, d), lambda i: (i, 0)),
        out_shape=jax.ShapeDtypeStruct((t, d), F32),
        scratch_shapes=[pltpu.VMEM((TM, d_a), BF16)],
        compiler_params=_cparams(("parallel",)),
        name="gmlp_mix",
    )(h, u, v, wmix, bias, w_out)


def _hgrn_proj_kernel(x_ref, g_ref, wq_ref, wf_ref, wi_ref, wg_ref, lbl_ref,
                      q_ref, f_ref, i_ref, gt_ref, xn_ref, *, li):
    @pl.when(pl.program_id(1) == 0)
    def _():
        xn_ref[...] = _rms(x_ref[...], g_ref[...]).astype(BF16)

    xn = xn_ref[...]
    zq = _dot(xn, wq_ref[...].astype(BF16))
    q_ref[...] = (zq * jax.nn.sigmoid(zq)).astype(BF16)

    logits = lbl_ref[...]
    ex = jnp.exp(logits - jnp.max(logits, axis=0, keepdims=True))
    sm = ex / jnp.sum(ex, axis=0, keepdims=True)
    lb = jnp.sum(sm[: li + 1], axis=0, keepdims=True) - sm[0:1]
    zf = _dot(xn, wf_ref[...].astype(BF16))
    f_ref[...] = lb + (1.0 - lb) * jax.nn.sigmoid(zf)

    i_ref[...] = _dot(xn, wi_ref[...].astype(BF16)).astype(BF16)

    zg = _dot(xn, wg_ref[...].astype(BF16))
    gt_ref[...] = (zg * jax.nn.sigmoid(zg)).astype(BF16)


def _hgrn_proj(h, norm_g, w_in, lb_logits, li, j_layer):
    t, d = h.shape
    d_b = w_in.shape[-1] // 4
    per_part = d_b // TNQ
    w_spec = lambda p: pl.BlockSpec((None, d, TNQ), lambda i, j, p=p: (j_layer, 0, p * per_part + j))
    o_spec = pl.BlockSpec((TM_STREAM, TNQ), lambda i, j: (i, j))
    bf = jax.ShapeDtypeStruct((t, d_b), BF16)
    return pl.pallas_call(
        functools.partial(_hgrn_proj_kernel, li=li),
        grid=(t // TM_STREAM, per_part),
        in_specs=[
            pl.BlockSpec((TM_STREAM, d), lambda i, j: (i, 0)),
            pl.BlockSpec((None, 1, d), lambda i, j: (li, 0, 0)),
            w_spec(0), w_spec(1), w_spec(2), w_spec(3),
            pl.BlockSpec((lb_logits.shape[0], TNQ), lambda i, j: (0, j)),
        ],
        out_specs=[o_spec, o_spec, o_spec, o_spec],
        out_shape=[bf, jax.ShapeDtypeStruct((t, d_b), F32), bf, bf],
        scratch_shapes=[pltpu.VMEM((TM_STREAM, d), BF16)],
        compiler_params=_cparams(("parallel", "arbitrary")),
        name="hgrn_proj",
    )(h, norm_g, w_in, w_in, w_in, w_in, lb_logits)


def _hgrn_halves():
    halves, half = [], HGRN_BLOCK // 2
    while half >= 1:
        halves.append(half)
        half //= 2
    return tuple(halves)


def _hgrn_norm_gate(o, ng, gate):
    ms = jnp.mean(o * o, axis=-1, keepdims=True)
    return (o * lax.rsqrt(ms + EPS) * ng * gate).astype(BF16)


def _hgrn_prompt_kernel(f_ref, q_ref, i_ref, g_ref, ng_ref, og_ref, sfin_ref,
                        st_ref, tril_ref, sgn_ref, mask_ref, *, nblk, hps):
    c = HGRN_BLOCK
    w = hps * LANES
    halves = _hgrn_halves()
    nl = len(halves)

    @pl.when((pl.program_id(0) == 0) & (pl.program_id(1) == 0))
    def _():
        r = lax.broadcasted_iota(jnp.int32, (c, c), 0)
        s = lax.broadcasted_iota(jnp.int32, (c, c), 1)
        tril_ref[...] = jnp.where(s <= r, 1.0, 0.0).astype(BF16)
        for lvl, half in enumerate(halves):
            par = 2 * half
            second = (r & half) != 0
            sgn_ref[lvl] = jnp.where(second, 1.0, -1.0)
            valid = second & ((s & half) == 0) & ((r // par) == (s // par))
            mask_ref[lvl] = jnp.where(valid, 1.0, 0.0).astype(BF16)
        mask_ref[nl] = jnp.where(r == s, 1.0, 0.0).astype(BF16)

    st_ref[...] = jnp.zeros_like(st_ref)
    low_sub = lax.broadcasted_iota(jnp.int32, (c // SUBLANES, SUBLANES, w), 1) < SUBLANES // 2

    def body(blk, carry):
        rows = pl.ds(pl.multiple_of(blk * c, c), c)
        f = f_ref[rows, :]
        qb = q_ref[rows, :]
        vb = i_ref[rows, :]
        q = qb.astype(F32)
        k = 1.0 - f
        kb = k.astype(BF16)

        x = jnp.log(f)
        hi = x.astype(BF16)
        r1 = x - hi.astype(F32)
        mid = r1.astype(BF16)
        lo = (r1 - mid.astype(F32)).astype(BF16)
        tril = tril_ref[...]
        b = _dot(tril, hi) + _dot(tril, mid) + _dot(tril, lo)
        b3 = b.reshape(c // SUBLANES, SUBLANES, w)

        att = [None] * hps

        def add_level(lvl, qt, kt):
            for hd in range(hps):
                cols = slice(hd * LANES, (hd + 1) * LANES)
                a = _dot_nt(qt[:, cols], kt[:, cols]).astype(BF16) * mask_ref[lvl]
                att[hd] = a if att[hd] is None else att[hd] + a

        for lvl, half in enumerate(halves):
            par = 2 * half
            if half == 1:
                add_level(lvl, (q * f).astype(BF16), kb)
                continue
            if half >= SUBLANES:
                pieces = [jnp.broadcast_to(b[p * par + half - 1:p * par + half, :], (par, w))
                          for p in range(c // par)]
                beta = pieces[0] if len(pieces) == 1 else jnp.concatenate(pieces, axis=0)
            elif half == SUBLANES // 2:
                beta = jnp.broadcast_to(b3[:, half - 1:half, :], b3.shape).reshape(c, w)
            else:
                beta = jnp.where(low_sub, b3[:, half - 1:half, :],
                                 b3[:, par + half - 1:par + half, :]).reshape(c, w)
            sgn = jnp.concatenate([sgn_ref[lvl]] * hps, axis=1)
            e = jnp.exp((b - beta) * sgn).astype(BF16)
            add_level(lvl, qb * e, kb * e)
        add_level(nl, qb, kb)

        qe = (q * jnp.exp(b)).astype(BF16)
        b_last = b[c - 1:c, :]
        kd = (k * jnp.exp(b_last - b)).astype(BF16)
        e_last = jnp.exp(b_last)
        gate = g_ref[rows, :].astype(F32)
        ng = ng_ref[...]
        for hd in range(hps):
            cols = slice(hd * LANES, (hd + 1) * LANES)
            st = st_ref[hd]
            o = _dot_nt(qe[:, cols], st.astype(BF16)) + _dot(att[hd], vb[:, cols])
            og_ref[rows, cols] = _hgrn_norm_gate(o, ng[:, cols], gate[:, cols])
            st_ref[hd] = st * e_last[:, cols] + _dot_tn(vb[:, cols], kd[:, cols])
        return carry

    lax.fori_loop(0, nblk, body, 0, unroll=2)
    for hd in range(hps):
        sfin_ref[hd] = st_ref[hd].T


def _hgrn_prompt(f, q, i, g, norm_g, batch, seq, heads):
    hps = HGRN_HEADS_PER_STEP
    w = hps * LANES
    ngrp = heads // hps
    nblk = seq // HGRN_BLOCK
    nl = len(_hgrn_halves())
    part = pl.BlockSpec((seq, w), lambda s, h: (s, h))
    return pl.pallas_call(
        functools.partial(_hgrn_prompt_kernel, nblk=nblk, hps=hps),
        grid=(batch, ngrp),
        in_specs=[part, part, part, part,
                  pl.BlockSpec((1, w), lambda s, h: (0, h))],
        out_specs=[pl.BlockSpec((seq, w), lambda s, h: (s, h)),
                   pl.BlockSpec((None, hps, LANES, LANES), lambda s, h: (s, h, 0, 0))],
        out_shape=[jax.ShapeDtypeStruct((batch * seq, heads * LANES), BF16),
                   jax.ShapeDtypeStruct((batch, heads, LANES, LANES), F32)],
        scratch_shapes=[pltpu.VMEM((hps, LANES, LANES), F32),
                        pltpu.VMEM((HGRN_BLOCK, HGRN_BLOCK), BF16),
                        pltpu.VMEM((nl, HGRN_BLOCK, LANES), F32),
                        pltpu.VMEM((nl + 1, HGRN_BLOCK, HGRN_BLOCK), BF16)],
        compiler_params=_cparams(("arbitrary", "arbitrary")),
        name="hgrn_prompt",
    )(f, q, i, g, norm_g)


def _hgrn_sample_kernel(f_ref, q_ref, i_ref, g_ref, ng_ref, s0_ref, og_ref, s1_ref, *, heads, dec_seq):
    rows_per_tile = SUBLANES
    per_tile = rows_per_tile // dec_seq
    row = lax.broadcasted_iota(jnp.int32, (rows_per_tile, LANES), 0)
    pos = row % dec_seq
    first = row < dec_seq
    zpad = jnp.zeros((rows_per_tile, LANES), F32)
    prow = lax.broadcasted_iota(jnp.int32, (4 * rows_per_tile, LANES), 0)
    in_pieces = prow < 3 * rows_per_tile
    sel_a = jnp.where(in_pieces & (prow % rows_per_tile == dec_seq - 1), 1.0, 0.0).astype(BF16)
    sel_b = jnp.where(in_pieces & (prow % rows_per_tile == 2 * dec_seq - 1), 1.0, 0.0).astype(BF16)
    f_all = f_ref[...]
    q_all = q_ref[...].astype(F32)
    v_all = i_ref[...].astype(F32)
    g_all = g_ref[...].astype(F32)
    ng = ng_ref[...]

    def pick(x, s):
        return jnp.where(first, x[s:s + 1, :], x[dec_seq + s:dec_seq + s + 1, :])

    for tile in range(SAMPLE_BATCH_TILE // per_tile):
        rows = slice(tile * rows_per_tile, (tile + 1) * rows_per_tile)
        for h in range(heads):
            cols = slice(h * LANES, (h + 1) * LANES)
            q = q_all[rows, cols]
            f = f_all[rows, cols]
            v = v_all[rows, cols]
            k = 1.0 - f
            b = jnp.log(f)
            shift = 1
            while shift < dec_seq:
                b = b + jnp.where(pos >= shift, pltpu.roll(b, shift, 0), 0.0)
                shift *= 2
            eb = jnp.exp(b)
            qe = (q * eb).astype(BF16)
            s_a = s0_ref[tile * per_tile, h]
            s_b = s0_ref[tile * per_tile + 1, h]
            o = jnp.where(first, _dot(qe, s_a.astype(BF16)), _dot(qe, s_b.astype(BF16)))
            for s in range(dec_seq):
                e = jnp.exp(jnp.minimum(b - pick(b, s), 0.0))
                wgt = jnp.sum(q * e * pick(k, s), axis=-1, keepdims=True)
                o = o + jnp.where(pos >= s, wgt, 0.0) * pick(v, s)
            og_ref[rows, cols] = _hgrn_norm_gate(o, ng[:, cols], g_all[rows, cols])

            b_last = pick(b, dec_seq - 1)
            kd = jnp.concatenate([k * jnp.exp(b_last - b), zpad], axis=0).astype(BF16)
            v_a = jnp.concatenate([jnp.where(first, v, 0.0), zpad], axis=0).astype(BF16)
            v_b = jnp.concatenate([jnp.where(first, 0.0, v), zpad], axis=0).astype(BF16)
            hi = eb.astype(BF16).astype(F32)
            mid = (eb - hi).astype(BF16).astype(F32)
            lo = eb - hi - mid
            pieces = jnp.concatenate([hi, mid, lo, zpad], axis=0).astype(BF16)
            s1_ref[tile * per_tile, h] = s_a * _dot_tn(pieces, sel_a) + _dot_tn(kd, v_a)
            s1_ref[tile * per_tile + 1, h] = s_b * _dot_tn(pieces, sel_b) + _dot_tn(kd, v_b)


def _hgrn_sample(f, q, i, g, norm_g, s0, n_prompt, dec_batch, dec_seq, heads):
    rows = SAMPLE_BATCH_TILE * dec_seq
    first_blk = n_prompt // rows
    d_b = heads * LANES
    part = pl.BlockSpec((rows, d_b), lambda i: (first_blk + i, 0))
    return pl.pallas_call(
        functools.partial(_hgrn_sample_kernel, heads=heads, dec_seq=dec_seq),
        grid=(dec_batch // SAMPLE_BATCH_TILE,),
        in_specs=[part, part, part, part,
                  _resident((1, d_b), lambda i: (0, 0)),
                  pl.BlockSpec((SAMPLE_BATCH_TILE, heads, LANES, LANES), lambda i: (i, 0, 0, 0))],
        out_specs=[pl.BlockSpec((rows, d_b), lambda i: (i, 0)),
                   pl.BlockSpec((SAMPLE_BATCH_TILE, heads, LANES, LANES), lambda i: (i, 0, 0, 0))],
        out_shape=[jax.ShapeDtypeStruct((dec_batch * dec_seq, d_b), BF16),
                   jax.ShapeDtypeStruct(s0.shape, F32)],
        compiler_params=_cparams(("parallel",)),
        name="hgrn_sample",
    )(f, q, i, g, norm_g, s0)


def _out_proj_kernel(h_ref, xa_ref, xb_ref, w_ref, o_ref, *, na):
    x = jnp.where(pl.program_id(0) < na, xa_ref[...], xb_ref[...])
    o_ref[...] = h_ref[...] + _dot(x, w_ref[...])


def _out_proj(h, xa, xb, w, j_layer):
    t, d = h.shape
    kdim = xa.shape[1]
    na = xa.shape[0] // TM
    return pl.pallas_call(
        functools.partial(_out_proj_kernel, na=na),
        grid=(t // TM,),
        in_specs=[
            pl.BlockSpec((TM, d), lambda i: (i, 0)),
            pl.BlockSpec((TM, kdim), lambda i: (jnp.minimum(i, na - 1), 0)),
            pl.BlockSpec((TM, kdim), lambda i: (jnp.maximum(i - na, 0), 0)),
            _resident((None, kdim, d), lambda i: (j_layer, 0, 0)),
        ],
        out_specs=pl.BlockSpec((TM, d), lambda i: (i, 0)),
        out_shape=jax.ShapeDtypeStruct((t, d), F32),
        compiler_params=_cparams(("parallel",)),
        name="hgrn_out_proj",
    )(h, xa, xb, w)


def kernel(x_prompt, x_sample, state_hgrn, p_prompt, p_sample, ffn1_w_in, ffn1_w_out, ffn2_w_in, ffn2_w_out, norm_ffn1, norm_mix, norm_ffn2, norm_ple, a_w_in, a_ln_g, a_ln_b, a_w_s, a_b_s, a_w_out, b_w_in, b_lb_logits, b_norm_g, b_w_out, ple_w_proj, ple_w_gate, final_norm):
    batch, seq, d = x_prompt.shape
    dec_batch, dec_seq, _ = x_sample.shape
    depth = ffn1_w_in.shape[0]
    n_prompt = batch * seq
    n_sample = dec_batch * dec_seq
    t = n_prompt + n_sample
    heads = b_norm_g.shape[-1] // LANES
    assert n_prompt % TM == 0 and n_sample % TM == 0 and t % TM_STREAM == 0
    assert seq % HGRN_BLOCK == 0 and heads % HGRN_HEADS_PER_STEP == 0
    assert CHUNK_A % dec_seq == 0 and SUBLANES % dec_seq == 0 and dec_batch % SAMPLE_BATCH_TILE == 0
    assert a_w_s.shape[-1] == CHUNK_A and seq % CHUNK_A == 0

    p_a = p_prompt.reshape(depth, n_prompt, -1)
    p_b = p_sample.reshape(depth, n_sample, -1)

    bf = lambda w: w.astype(BF16)
    vec = lambda g: g.reshape(g.shape[0], 1, g.shape[-1])
    a_w_in_b, a_w_out_b, b_w_out_b = bf(a_w_in), bf(a_w_out), bf(b_w_out)
    ple_w_proj_b, ple_w_gate_b = bf(ple_w_proj), bf(ple_w_gate)
    n_ffn1, n_mix, n_ffn2, n_ple = vec(norm_ffn1), vec(norm_mix), vec(norm_ffn2), vec(norm_ple)
    ln_g, ln_b = vec(a_ln_g), vec(a_ln_b)
    final_g = final_norm.reshape(1, d)

    reps = CHUNK_A // dec_seq
    tri = jnp.tril(jnp.ones((CHUNK_A, CHUNK_A), bool))
    pos_c = jnp.arange(CHUNK_A) // dec_seq
    blockdiag = pos_c[:, None] == pos_c[None, :]
    w_prompt = jnp.where(tri, a_w_s, 0.0)
    w_sample = jnp.where(tri & blockdiag, jnp.tile(a_w_s[:, :, :dec_seq, :dec_seq], (1, 1, reps, reps)), 0.0)
    wmix = jnp.stack([w_prompt, w_sample], axis=1).astype(BF16)
    b_prompt = jnp.swapaxes(a_b_s, 1, 2)
    b_sample = jnp.tile(b_prompt[:, :dec_seq], (1, reps, 1))
    bias = jnp.repeat(jnp.stack([b_prompt, b_sample], axis=1), LANES, axis=-1)

    new_v, s_prompt, s_sample = [], [], []
    h = None
    for li in range(depth):
        if li == 0:
            h = _ffn(x_prompt.reshape(n_prompt, d), x_sample.reshape(n_sample, d), n_ffn1, ffn1_w_in, ffn1_w_out, li)
        else:
            h = _ffn(h, None, n_ffn1, ffn1_w_in, ffn1_w_out, li)
        j = li // 2
        if li % 2 == 0:
            u = _gmlp_proj(h, n_mix, a_w_in_b, ln_g, ln_b, li, j, half=0, layer_norm=False, out_dtype=BF16)
            v = _gmlp_proj(h, n_mix, a_w_in_b, ln_g, ln_b, li, j, half=1, layer_norm=True, out_dtype=F32)
            new_v.append(v[n_prompt:].reshape(dec_batch, dec_seq, -1))
            h = _gmlp_mix(h, u, v, wmix[j], bias[j], a_w_out_b, j, n_prompt // TM)
        else:
            q, f, iv, g = _hgrn_proj(h, n_mix, b_w_in, b_lb_logits, li, j)
            ng = b_norm_g[j].reshape(1, heads * LANES)
            og_p, s_fin = _hgrn_prompt(f, q, iv, g, ng, batch, seq, heads)
            og_s, s_new = _hgrn_sample(f, q, iv, g, ng, state_hgrn[j], n_prompt, dec_batch, dec_seq, heads)
            s_prompt.append(s_fin)
            s_sample.append(s_new)
            h = _out_proj(h, og_p, og_s, b_w_out_b, j)
        h = _ffn(h, None, n_ffn2, ffn2_w_in, ffn2_w_out, li)
        h = _ple(h, p_a, p_b, n_ple, ple_w_gate_b, ple_w_proj_b, final_g, li, final=(li == depth - 1))

    y_prompt = h[0].reshape(batch, seq, d)
    y_sample = h[1].reshape(dec_batch, dec_seq, d)
    return (y_prompt, y_sample, jnp.stack(s_prompt), jnp.stack(s_sample), jnp.stack(new_v))
```

```python
import functools

import jax
import jax.numpy as jnp
from jax import lax
from jax.experimental import pallas as pl
from jax.experimental.pallas import tpu as pltpu

F32 = jnp.float32
BF16 = jnp.bfloat16
EPS = 1e-6

LANES = 128
SUBLANES = 8
VMEM_LIMIT_BYTES = 63 * 1024 * 1024

TM = 512
TM_STREAM = 1088
TF = 512
TN = 512
TNQ = 512
HGRN_BLOCK = 128
HGRN_HEADS_PER_STEP = 8
CHUNK_A = 128
SAMPLE_BATCH_TILE = 4


def _cparams(sem):
    return pltpu.CompilerParams(dimension_semantics=sem, vmem_limit_bytes=VMEM_LIMIT_BYTES)


def _rms(x, g):
    ms = jnp.mean(x * x, axis=-1, keepdims=True)
    return x * lax.rsqrt(ms + EPS) * g


def _resident(shape, index_map):
    return pl.BlockSpec(shape, index_map, pipeline_mode=pl.Buffered(1))


def _dot(a, b):
    return jnp.dot(a, b, preferred_element_type=F32)


def _dot_nt(a, b):
    return lax.dot_general(a, b, (((1,), (1,)), ((), ())), preferred_element_type=F32)


def _dot_tn(a, b):
    return lax.dot_general(a, b, (((0,), (0,)), ((), ())), preferred_element_type=F32)


def _ffn_kernel(xa_ref, xb_ref, g_ref, wg_ref, wu_ref, wo_ref, o_ref, xbuf_ref, xn_ref, sem, *, n_a, n_b):
    tm = o_ref.shape[0]
    full_a, rem_a = divmod(n_a, tm)
    n_tiles = (n_a + n_b) // tm

    def tile_copies(tile, act):
        @pl.when(tile < full_a)
        def _():
            act(pltpu.make_async_copy(xa_ref.at[pl.ds(tile * tm, tm)], xbuf_ref, sem.at[0]))

        if n_b:
            @pl.when(tile == full_a)
            def _():
                if rem_a:
                    act(pltpu.make_async_copy(xa_ref.at[pl.ds(full_a * tm, rem_a)],
                                              xbuf_ref.at[pl.ds(0, rem_a)], sem.at[0]))
                act(pltpu.make_async_copy(xb_ref, xbuf_ref.at[pl.ds(rem_a, n_b)], sem.at[1]))

    @pl.when(pl.program_id(1) == 0)
    def _():
        i = pl.program_id(0)

        @pl.when(i == 0)
        def _():
            tile_copies(i, lambda cp: cp.start())

        tile_copies(i, lambda cp: cp.wait())
        x = xbuf_ref[...]
        o_ref[...] = x
        xn_ref[...] = _rms(x, g_ref[...]).astype(BF16)

        @pl.when(i + 1 < n_tiles)
        def _():
            tile_copies(i + 1, lambda cp: cp.start())

    xn = xn_ref[...]
    gate = _dot(xn, wg_ref[...].astype(BF16))
    up = _dot(xn, wu_ref[...].astype(BF16))
    act = (0.5 * gate * jax.nn.sigmoid(gate) * up).astype(BF16)
    o_ref[...] += _dot(act, wo_ref[...].astype(BF16))


def _ffn(xa, xb, norm_g, w_in, w_out, li):
    n_a, d = xa.shape
    n_b = 0 if xb is None else xb.shape[0]
    t = n_a + n_b
    assert t % TM_STREAM == 0 and (n_b == 0 or n_a % TM_STREAM + n_b == TM_STREAM)
    d_ff = w_out.shape[1]
    nf = d_ff // TF
    return pl.pallas_call(
        functools.partial(_ffn_kernel, n_a=n_a, n_b=n_b),
        grid=(t // TM_STREAM, nf),
        in_specs=[
            pl.BlockSpec(memory_space=pl.ANY),
            pl.BlockSpec(memory_space=pl.ANY),
            pl.BlockSpec((None, 1, d), lambda i, j: (li, 0, 0)),
            pl.BlockSpec((None, d, TF), lambda i, j: (li, 0, j)),
            pl.BlockSpec((None, d, TF), lambda i, j: (li, 0, j + nf)),
            pl.BlockSpec((None, TF, d), lambda i, j: (li, j, 0)),
        ],
        out_specs=pl.BlockSpec((TM_STREAM, d), lambda i, j: (i, 0)),
        out_shape=jax.ShapeDtypeStruct((t, d), F32),
        scratch_shapes=[pltpu.VMEM((TM_STREAM, d), F32), pltpu.VMEM((TM_STREAM, d), BF16),
                        pltpu.SemaphoreType.DMA((2,))],
        compiler_params=_cparams(("arbitrary", "arbitrary")),
        name=f"ffn_l{li}",
    )(xa, xa if xb is None else xb, norm_g, w_in, w_in, w_out)


def _ple_kernel(h_ref, pa_ref, pb_ref, g_ref, wgate_ref, wproj_ref, fn_ref, *o_refs, final, na):
    i = pl.program_id(0)
    h = h_ref[...]
    hn = _rms(h, g_ref[...]).astype(BF16)
    gate = jax.nn.sigmoid(_dot(hn, wgate_ref[...]))
    p = jnp.where(i < na, pa_ref[...], pb_ref[...])
    proj = _dot(p.astype(BF16), wproj_ref[...])
    out = h + proj * gate
    if not final:
        o_refs[0][...] = out
        return
    out = _rms(out, fn_ref[...])

    @pl.when(i < na)
    def _():
        o_refs[0][...] = out

    @pl.when(i >= na)
    def _():
        o_refs[1][...] = out


def _ple(h, pa, pb, norm_g, w_gate, w_proj, final_g, li, final):
    t, d = h.shape
    dp = pa.shape[-1]
    na = pa.shape[1] // TM
    in_a = lambda i: jnp.minimum(i, na - 1)
    in_b = lambda i: jnp.maximum(i - na, 0)
    if final:
        out_specs = [pl.BlockSpec((TM, d), lambda i: (in_a(i), 0)), pl.BlockSpec((TM, d), lambda i: (in_b(i), 0))]
        out_shape = [jax.ShapeDtypeStruct((pa.shape[1], d), F32), jax.ShapeDtypeStruct((pb.shape[1], d), F32)]
    else:
        out_specs = pl.BlockSpec((TM, d), lambda i: (i, 0))
        out_shape = jax.ShapeDtypeStruct((t, d), F32)
    return pl.pallas_call(
        functools.partial(_ple_kernel, final=final, na=na),
        grid=(t // TM,),
        in_specs=[
            pl.BlockSpec((TM, d), lambda i: (i, 0)),
            pl.BlockSpec((None, TM, dp), lambda i: (li, in_a(i), 0)),
            pl.BlockSpec((None, TM, dp), lambda i: (li, in_b(i), 0)),
            _resident((None, 1, d), lambda i: (li, 0, 0)),
            _resident((None, d, d), lambda i: (li, 0, 0)),
            _resident((None, dp, d), lambda i: (li, 0, 0)),
            _resident((1, d), lambda i: (0, 0)),
        ],
        out_specs=out_specs,
        out_shape=out_shape,
        compiler_params=_cparams(("arbitrary",)),
        name=f"ple_l{li}",
    )(h, pa, pb, norm_g, w_gate, w_proj, final_g)


def _gelu(z):
    return 0.5 * z * (1.0 + lax.erf(z * (2.0 ** -0.5)))


def _gmlp_proj_kernel(x_ref, g_ref, w_ref, lng_ref, lnb_ref, o_ref, xn_ref, *, layer_norm):
    @pl.when(pl.program_id(1) == 0)
    def _():
        xn_ref[...] = _rms(x_ref[...], g_ref[...]).astype(BF16)

    z = _gelu(_dot(xn_ref[...], w_ref[...]))
    if layer_norm:
        mu = jnp.mean(z, axis=-1, keepdims=True)
        zc = z - mu
        var = jnp.mean(zc * zc, axis=-1, keepdims=True)
        z = zc * lax.rsqrt(var + EPS) * lng_ref[...] + lnb_ref[...]
    o_ref[...] = z.astype(o_ref.dtype)


def _gmlp_proj(h, norm_g, w_in, ln_g, ln_b, li, j_layer, *, half, layer_norm, out_dtype):
    t, d = h.shape
    d_a = w_in.shape[-1] // 2
    tn = d_a
    nn = d_a // tn
    off = half * nn
    return pl.pallas_call(
        functools.partial(_gmlp_proj_kernel, layer_norm=layer_norm),
        grid=(t // TM, nn),
        in_specs=[
            pl.BlockSpec((TM, d), lambda i, j: (i, 0)),
            pl.BlockSpec((None, 1, d), lambda i, j: (li, 0, 0)),
            pl.BlockSpec((None, d, tn), lambda i, j: (j_layer, 0, j + off)),
            pl.BlockSpec((None, 1, tn), lambda i, j: (j_layer, 0, j)),
            pl.BlockSpec((None, 1, tn), lambda i, j: (j_layer, 0, j)),
        ],
        out_specs=pl.BlockSpec((TM, tn), lambda i, j: (i, j)),
        out_shape=jax.ShapeDtypeStruct((t, d_a), out_dtype),
        scratch_shapes=[pltpu.VMEM((TM, d), BF16)],
        compiler_params=_cparams(("parallel", "arbitrary")),
        name=f"gmlp_proj_l{li}_{'v' if half else 'u'}",
    )(h, norm_g, w_in, ln_g, ln_b)


def _gmlp_mix_kernel(h_ref, u_ref, v_ref, wmix_ref, bias_ref, wo_ref, o_ref, t_ref, *, groups):
    for c in range(TM // CHUNK_A):
        rows = slice(c * CHUNK_A, (c + 1) * CHUNK_A)
        for g in range(groups):
            cols = slice(g * LANES, (g + 1) * LANES)
            s = _dot(wmix_ref[g], v_ref[rows, cols].astype(BF16)) + bias_ref[:, cols]
            t_ref[rows, cols] = (u_ref[rows, cols].astype(F32) * s).astype(BF16)
    o_ref[...] = h_ref[...] + _dot(t_ref[...], wo_ref[...])


def _gmlp_mix(h, u, v, wmix, bias, w_out, j_layer, n_prompt_tiles):
    t, d = h.shape
    d_a = u.shape[1]
    groups = d_a // LANES
    return pl.pallas_call(
        functools.partial(_gmlp_mix_kernel, groups=groups),
        grid=(t // TM,),
        in_specs=[
            pl.BlockSpec((TM, d), lambda i: (i, 0)),
            pl.BlockSpec((TM, d_a), lambda i: (i, 0)),
            pl.BlockSpec((TM, d_a), lambda i: (i, 0)),
            pl.BlockSpec((None, groups, CHUNK_A, CHUNK_A), lambda i: (i // n_prompt_tiles, 0, 0, 0)),
            pl.BlockSpec((None, CHUNK_A, d_a), lambda i: (i // n_prompt_tiles, 0, 0)),
            _resident((None, d_a, d), lambda i: (j_layer, 0, 0)),
        ],
        out_specs=pl.BlockSpec((TM, d), lambda i: (i, 0)),
        out_shape=jax.ShapeDtypeStruct((t, d), F32),
        scratch_shapes=[pltpu.VMEM((TM, d_a), BF16)],
        compiler_params=_cparams(("parallel",)),
        name="gmlp_mix",
    )(h, u, v, wmix, bias, w_out)


def _hgrn_proj_kernel(x_hbm_ref, g_ref, wq_ref, wf_ref, wi_ref, wg_ref, lbl_ref,
                      q_ref, f_ref, i_ref, gt_ref, xbuf_ref, xn_ref, sem, *, li, n_tiles):
    tm = xn_ref.shape[0]

    def row_copy(tile):
        return pltpu.make_async_copy(x_hbm_ref.at[pl.ds(tile * tm, tm)], xbuf_ref, sem.at[0])

    @pl.when(pl.program_id(1) == 0)
    def _():
        i = pl.program_id(0)

        @pl.when(i == 0)
        def _():
            row_copy(i).start()

        row_copy(i).wait()
        xn_ref[...] = _rms(xbuf_ref[...], g_ref[...]).astype(BF16)

        @pl.when(i + 1 < n_tiles)
        def _():
            row_copy(i + 1).start()

    xn = xn_ref[...]
    zq = _dot(xn, wq_ref[...].astype(BF16))
    q_ref[...] = (zq * jax.nn.sigmoid(zq)).astype(BF16)

    logits = lbl_ref[...]
    ex = jnp.exp(logits - jnp.max(logits, axis=0, keepdims=True))
    sm = ex / jnp.sum(ex, axis=0, keepdims=True)
    lb = jnp.sum(sm[: li + 1], axis=0, keepdims=True) - sm[0:1]
    zf = _dot(xn, wf_ref[...].astype(BF16))
    f_ref[...] = lb + (1.0 - lb) * jax.nn.sigmoid(zf)

    i_ref[...] = _dot(xn, wi_ref[...].astype(BF16)).astype(BF16)

    zg = _dot(xn, wg_ref[...].astype(BF16))
    gt_ref[...] = (zg * jax.nn.sigmoid(zg)).astype(BF16)


def _hgrn_proj(h, norm_g, w_in, lb_logits, li, j_layer):
    t, d = h.shape
    d_b = w_in.shape[-1] // 4
    per_part = d_b // TNQ
    w_spec = lambda p: pl.BlockSpec((None, d, TNQ), lambda i, j, p=p: (j_layer, 0, p * per_part + j))
    o_spec = pl.BlockSpec((TM_STREAM, TNQ), lambda i, j: (i, j))
    bf = jax.ShapeDtypeStruct((t, d_b), BF16)
    return pl.pallas_call(
        functools.partial(_hgrn_proj_kernel, li=li, n_tiles=t // TM_STREAM),
        grid=(t // TM_STREAM, per_part),
        in_specs=[
            pl.BlockSpec(memory_space=pl.ANY),
            pl.BlockSpec((None, 1, d), lambda i, j: (li, 0, 0)),
            w_spec(0), w_spec(1), w_spec(2), w_spec(3),
            pl.BlockSpec((lb_logits.shape[0], TNQ), lambda i, j: (0, j)),
        ],
        out_specs=[o_spec, o_spec, o_spec, o_spec],
        out_shape=[bf, jax.ShapeDtypeStruct((t, d_b), F32), bf, bf],
        scratch_shapes=[pltpu.VMEM((TM_STREAM, d), F32), pltpu.VMEM((TM_STREAM, d), BF16),
                        pltpu.SemaphoreType.DMA((1,))],
        compiler_params=_cparams(("arbitrary", "arbitrary")),
        name="hgrn_proj",
    )(h, norm_g, w_in, w_in, w_in, w_in, lb_logits)


def _hgrn_halves():
    halves, half = [], HGRN_BLOCK // 2
    while half >= 1:
        halves.append(half)
        half //= 2
    return tuple(halves)


def _hgrn_norm_gate(o, ng, gate):
    ms = jnp.mean(o * o, axis=-1, keepdims=True)
    return (o * lax.rsqrt(ms + EPS) * ng * gate).astype(BF16)


def _hgrn_prompt_kernel(f_ref, q_ref, i_ref, g_ref, ng_ref, og_ref, sfin_ref,
                        st_ref, tril_ref, sgn_ref, mask_ref, *, nblk, hps):
    c = HGRN_BLOCK
    w = hps * LANES
    halves = _hgrn_halves()
    nl = len(halves)

    @pl.when((pl.program_id(0) == 0) & (pl.program_id(1) == 0))
    def _():
        r = lax.broadcasted_iota(jnp.int32, (c, c), 0)
        s = lax.broadcasted_iota(jnp.int32, (c, c), 1)
        tril_ref[...] = jnp.where(s <= r, 1.0, 0.0).astype(BF16)
        for lvl, half in enumerate(halves):
            par = 2 * half
            second = (r & half) != 0
            sgn_ref[lvl] = jnp.where(second, 1.0, -1.0)
            valid = second & ((s & half) == 0) & ((r // par) == (s // par))
            mask_ref[lvl] = jnp.where(valid, 1.0, 0.0).astype(BF16)
        mask_ref[nl] = jnp.where(r == s, 1.0, 0.0).astype(BF16)

    st_ref[...] = jnp.zeros_like(st_ref)
    low_sub = lax.broadcasted_iota(jnp.int32, (c // SUBLANES, SUBLANES, w), 1) < SUBLANES // 2

    def body(blk, carry):
        rows = pl.ds(pl.multiple_of(blk * c, c), c)
        f = f_ref[rows, :]
        qb = q_ref[rows, :]
        vb = i_ref[rows, :]
        q = qb.astype(F32)
        k = 1.0 - f
        kb = k.astype(BF16)

        x = jnp.log(f)
        hi = x.astype(BF16)
        r1 = x - hi.astype(F32)
        mid = r1.astype(BF16)
        lo = (r1 - mid.astype(F32)).astype(BF16)
        tril = tril_ref[...]
        b = _dot(tril, hi) + _dot(tril, mid) + _dot(tril, lo)
        b3 = b.reshape(c // SUBLANES, SUBLANES, w)

        att = [None] * hps

        def add_level(lvl, qt, kt):
            for hd in range(hps):
                cols = slice(hd * LANES, (hd + 1) * LANES)
                a = _dot_nt(qt[:, cols], kt[:, cols]).astype(BF16) * mask_ref[lvl]
                att[hd] = a if att[hd] is None else att[hd] + a

        for lvl, half in enumerate(halves):
            par = 2 * half
            if half == 1:
                add_level(lvl, (q * f).astype(BF16), kb)
                continue
            if half >= SUBLANES:
                pieces = [jnp.broadcast_to(b[p * par + half - 1:p * par + half, :], (par, w))
                          for p in range(c // par)]
                beta = pieces[0] if len(pieces) == 1 else jnp.concatenate(pieces, axis=0)
            elif half == SUBLANES // 2:
                beta = jnp.broadcast_to(b3[:, half - 1:half, :], b3.shape).reshape(c, w)
            else:
                beta = jnp.where(low_sub, b3[:, half - 1:half, :],
                                 b3[:, par + half - 1:par + half, :]).reshape(c, w)
            sgn = jnp.concatenate([sgn_ref[lvl]] * hps, axis=1)
            e = jnp.exp((b - beta) * sgn).astype(BF16)
            add_level(lvl, qb * e, kb * e)
        add_level(nl, qb, kb)

        qe = (q * jnp.exp(b)).astype(BF16)
        b_last = b[c - 1:c, :]
        kd = (k * jnp.exp(b_last - b)).astype(BF16)
        e_last = jnp.exp(b_last)
        gate = g_ref[rows, :].astype(F32)
        ng = ng_ref[...]
        for hd in range(hps):
            cols = slice(hd * LANES, (hd + 1) * LANES)
            st = st_ref[hd]
            o = _dot_nt(qe[:, cols], st.astype(BF16)) + _dot(att[hd], vb[:, cols])
            og_ref[rows, cols] = _hgrn_norm_gate(o, ng[:, cols], gate[:, cols])
            st_ref[hd] = st * e_last[:, cols] + _dot_tn(vb[:, cols], kd[:, cols])
        return carry

    lax.fori_loop(0, nblk, body, 0, unroll=2)
    for hd in range(hps):
        sfin_ref[hd] = st_ref[hd].T


def _hgrn_prompt(f, q, i, g, norm_g, batch, seq, heads):
    hps = HGRN_HEADS_PER_STEP
    w = hps * LANES
    ngrp = heads // hps
    nblk = seq // HGRN_BLOCK
    nl = len(_hgrn_halves())
    part = pl.BlockSpec((seq, w), lambda s, h: (s, h))
    return pl.pallas_call(
        functools.partial(_hgrn_prompt_kernel, nblk=nblk, hps=hps),
        grid=(batch, ngrp),
        in_specs=[part, part, part, part,
                  pl.BlockSpec((1, w), lambda s, h: (0, h))],
        out_specs=[pl.BlockSpec((seq, w), lambda s, h: (s, h)),
                   pl.BlockSpec((None, hps, LANES, LANES), lambda s, h: (s, h, 0, 0))],
        out_shape=[jax.ShapeDtypeStruct((batch * seq, heads * LANES), BF16),
                   jax.ShapeDtypeStruct((batch, heads, LANES, LANES), F32)],
        scratch_shapes=[pltpu.VMEM((hps, LANES, LANES), F32),
                        pltpu.VMEM((HGRN_BLOCK, HGRN_BLOCK), BF16),
                        pltpu.VMEM((nl, HGRN_BLOCK, LANES), F32),
                        pltpu.VMEM((nl + 1, HGRN_BLOCK, HGRN_BLOCK), BF16)],
        compiler_params=_cparams(("arbitrary", "arbitrary")),
        name="hgrn_prompt",
    )(f, q, i, g, norm_g)


def _hgrn_sample_kernel(f_ref, q_ref, i_ref, g_ref, ng_ref, s0_ref, og_ref, s1_ref, *, heads, dec_seq):
    rows_per_tile = SUBLANES
    per_tile = rows_per_tile // dec_seq
    row = lax.broadcasted_iota(jnp.int32, (rows_per_tile, LANES), 0)
    pos = row % dec_seq
    first = row < dec_seq
    zpad = jnp.zeros((rows_per_tile, LANES), F32)
    prow = lax.broadcasted_iota(jnp.int32, (4 * rows_per_tile, LANES), 0)
    in_pieces = prow < 3 * rows_per_tile
    sel_a = jnp.where(in_pieces & (prow % rows_per_tile == dec_seq - 1), 1.0, 0.0).astype(BF16)
    sel_b = jnp.where(in_pieces & (prow % rows_per_tile == 2 * dec_seq - 1), 1.0, 0.0).astype(BF16)
    f_all = f_ref[...]
    q_all = q_ref[...].astype(F32)
    v_all = i_ref[...].astype(F32)
    g_all = g_ref[...].astype(F32)
    ng = ng_ref[...]

    def pick(x, s):
        return jnp.where(first, x[s:s + 1, :], x[dec_seq + s:dec_seq + s + 1, :])

    for tile in range(SAMPLE_BATCH_TILE // per_tile):
        rows = slice(tile * rows_per_tile, (tile + 1) * rows_per_tile)
        for h in range(heads):
            cols = slice(h * LANES, (h + 1) * LANES)
            q = q_all[rows, cols]
            f = f_all[rows, cols]
            v = v_all[rows, cols]
            k = 1.0 - f
            b = jnp.log(f)
            shift = 1
            while shift < dec_seq:
                b = b + jnp.where(pos >= shift, pltpu.roll(b, shift, 0), 0.0)
                shift *= 2
            eb = jnp.exp(b)
            qe = (q * eb).astype(BF16)
            s_a = s0_ref[tile * per_tile, h]
            s_b = s0_ref[tile * per_tile + 1, h]
            o = jnp.where(first, _dot(qe, s_a.astype(BF16)), _dot(qe, s_b.astype(BF16)))
            for s in range(dec_seq):
                e = jnp.exp(jnp.minimum(b - pick(b, s), 0.0))
                wgt = jnp.sum(q * e * pick(k, s), axis=-1, keepdims=True)
                o = o + jnp.where(pos >= s, wgt, 0.0) * pick(v, s)
            og_ref[rows, cols] = _hgrn_norm_gate(o, ng[:, cols], g_all[rows, cols])

            b_last = pick(b, dec_seq - 1)
            kd = jnp.concatenate([k * jnp.exp(b_last - b), zpad], axis=0).astype(BF16)
            v_a = jnp.concatenate([jnp.where(first, v, 0.0), zpad], axis=0).astype(BF16)
            v_b = jnp.concatenate([jnp.where(first, 0.0, v), zpad], axis=0).astype(BF16)
            hi = eb.astype(BF16).astype(F32)
            mid = (eb - hi).astype(BF16).astype(F32)
            lo = eb - hi - mid
            pieces = jnp.concatenate([hi, mid, lo, zpad], axis=0).astype(BF16)
            s1_ref[tile * per_tile, h] = s_a * _dot_tn(pieces, sel_a) + _dot_tn(kd, v_a)
            s1_ref[tile * per_tile + 1, h] = s_b * _dot_tn(pieces, sel_b) + _dot_tn(kd, v_b)


def _hgrn_sample(f, q, i, g, norm_g, s0, n_prompt, dec_batch, dec_seq, heads):
    rows = SAMPLE_BATCH_TILE * dec_seq
    first_blk = n_prompt // rows
    d_b = heads * LANES
    part = pl.BlockSpec((rows, d_b), lambda i: (first_blk + i, 0))
    return pl.pallas_call(
        functools.partial(_hgrn_sample_kernel, heads=heads, dec_seq=dec_seq),
        grid=(dec_batch // SAMPLE_BATCH_TILE,),
        in_specs=[part, part, part, part,
                  _resident((1, d_b), lambda i: (0, 0)),
                  pl.BlockSpec((SAMPLE_BATCH_TILE, heads, LANES, LANES), lambda i: (i, 0, 0, 0))],
        out_specs=[pl.BlockSpec((rows, d_b), lambda i: (i, 0)),
                   pl.BlockSpec((SAMPLE_BATCH_TILE, heads, LANES, LANES), lambda i: (i, 0, 0, 0))],
        out_shape=[jax.ShapeDtypeStruct((dec_batch * dec_seq, d_b), BF16),
                   jax.ShapeDtypeStruct(s0.shape, F32)],
        compiler_params=_cparams(("parallel",)),
        name="hgrn_sample",
    )(f, q, i, g, norm_g, s0)


def _out_proj_kernel(h_ref, xa_ref, xb_ref, w_ref, o_ref, *, na):
    x = jnp.where(pl.program_id(0) < na, xa_ref[...], xb_ref[...])
    o_ref[...] = h_ref[...] + _dot(x, w_ref[...])


def _out_proj(h, xa, xb, w, j_layer):
    t, d = h.shape
    kdim = xa.shape[1]
    na = xa.shape[0] // TM
    return pl.pallas_call(
        functools.partial(_out_proj_kernel, na=na),
        grid=(t // TM,),
        in_specs=[
            pl.BlockSpec((TM, d), lambda i: (i, 0)),
            pl.BlockSpec((TM, kdim), lambda i: (jnp.minimum(i, na - 1), 0)),
            pl.BlockSpec((TM, kdim), lambda i: (jnp.maximum(i - na, 0), 0)),
            _resident((None, kdim, d), lambda i: (j_layer, 0, 0)),
        ],
        out_specs=pl.BlockSpec((TM, d), lambda i: (i, 0)),
        out_shape=jax.ShapeDtypeStruct((t, d), F32),
        compiler_params=_cparams(("parallel",)),
        name="hgrn_out_proj",
    )(h, xa, xb, w)


def kernel(x_prompt, x_sample, state_hgrn, p_prompt, p_sample, ffn1_w_in, ffn1_w_out, ffn2_w_in, ffn2_w_out, norm_ffn1, norm_mix, norm_ffn2, norm_ple, a_w_in, a_ln_g, a_ln_b, a_w_s, a_b_s, a_w_out, b_w_in, b_lb_logits, b_norm_g, b_w_out, ple_w_proj, ple_w_gate, final_norm):
    batch, seq, d = x_prompt.shape
    dec_batch, dec_seq, _ = x_sample.shape
    depth = ffn1_w_in.shape[0]
    n_prompt = batch * seq
    n_sample = dec_batch * dec_seq
    t = n_prompt + n_sample
    heads = b_norm_g.shape[-1] // LANES
    assert n_prompt % TM == 0 and n_sample % TM == 0 and t % TM_STREAM == 0
    assert seq % HGRN_BLOCK == 0 and heads % HGRN_HEADS_PER_STEP == 0
    assert CHUNK_A % dec_seq == 0 and SUBLANES % dec_seq == 0 and dec_batch % SAMPLE_BATCH_TILE == 0
    assert a_w_s.shape[-1] == CHUNK_A and seq % CHUNK_A == 0

    p_a = p_prompt.reshape(depth, n_prompt, -1)
    p_b = p_sample.reshape(depth, n_sample, -1)

    bf = lambda w: w.astype(BF16)
    vec = lambda g: g.reshape(g.shape[0], 1, g.shape[-1])
    a_w_in_b, a_w_out_b, b_w_out_b = bf(a_w_in), bf(a_w_out), bf(b_w_out)
    ple_w_proj_b, ple_w_gate_b = bf(ple_w_proj), bf(ple_w_gate)
    n_ffn1, n_mix, n_ffn2, n_ple = vec(norm_ffn1), vec(norm_mix), vec(norm_ffn2), vec(norm_ple)
    ln_g, ln_b = vec(a_ln_g), vec(a_ln_b)
    final_g = final_norm.reshape(1, d)

    reps = CHUNK_A // dec_seq
    tri = jnp.tril(jnp.ones((CHUNK_A, CHUNK_A), bool))
    pos_c = jnp.arange(CHUNK_A) // dec_seq
    blockdiag = pos_c[:, None] == pos_c[None, :]
    w_prompt = jnp.where(tri, a_w_s, 0.0)
    w_sample = jnp.where(tri & blockdiag, jnp.tile(a_w_s[:, :, :dec_seq, :dec_seq], (1, 1, reps, reps)), 0.0)
    wmix = jnp.stack([w_prompt, w_sample], axis=1).astype(BF16)
    b_prompt = jnp.swapaxes(a_b_s, 1, 2)
    b_sample = jnp.tile(b_prompt[:, :dec_seq], (1, reps, 1))
    bias = jnp.repeat(jnp.stack([b_prompt, b_sample], axis=1), LANES, axis=-1)

    new_v, s_prompt, s_sample = [], [], []
    h = None
    for li in range(depth):
        if li == 0:
            h = _ffn(x_prompt.reshape(n_prompt, d), x_sample.reshape(n_sample, d), n_ffn1, ffn1_w_in, ffn1_w_out, li)
        else:
            h = _ffn(h, None, n_ffn1, ffn1_w_in, ffn1_w_out, li)
        j = li // 2
        if li % 2 == 0:
            u = _gmlp_proj(h, n_mix, a_w_in_b, ln_g, ln_b, li, j, half=0, layer_norm=False, out_dtype=BF16)
            v = _gmlp_proj(h, n_mix, a_w_in_b, ln_g, ln_b, li, j, half=1, layer_norm=True, out_dtype=F32)
            new_v.append(v[n_prompt:].reshape(dec_batch, dec_seq, -1))
            h = _gmlp_mix(h, u, v, wmix[j], bias[j], a_w_out_b, j, n_prompt // TM)
        else:
            q, f, iv, g = _hgrn_proj(h, n_mix, b_w_in, b_lb_logits, li, j)
            ng = b_norm_g[j].reshape(1, heads * LANES)
            og_p, s_fin = _hgrn_prompt(f, q, iv, g, ng, batch, seq, heads)
            og_s, s_new = _hgrn_sample(f, q, iv, g, ng, state_hgrn[j], n_prompt, dec_batch, dec_seq, heads)
            s_prompt.append(s_fin)
            s_sample.append(s_new)
            h = _out_proj(h, og_p, og_s, b_w_out_b, j)
        h = _ffn(h, None, n_ffn2, ffn2_w_in, ffn2_w_out, li)
        h = _ple(h, p_a, p_b, n_ple, ple_w_gate_b, ple_w_proj_b, final_g, li, final=(li == depth - 1))

    y_prompt = h[0].reshape(batch, seq, d)
    y_sample = h[1].reshape(dec_batch, dec_seq, d)
    return (y_prompt, y_sample, jnp.stack(s_prompt), jnp.stack(s_sample), jnp.stack(new_v))
```

```python
import functools

import jax
import jax.numpy as jnp
from jax import lax
from jax.experimental import pallas as pl
from jax.experimental.pallas import tpu as pltpu

F32 = jnp.float32
BF16 = jnp.bfloat16
EPS = 1e-6

LANES = 128
SUBLANES = 8
VMEM_LIMIT_BYTES = 63 * 1024 * 1024

TM = 512
TM_STREAM = 1088
TF = 512
TN = 512
TNQ = 512
HGRN_BLOCK = 128
HGRN_HEADS_PER_STEP = 8
CHUNK_A = 128
SAMPLE_BATCH_TILE = 4


def _cparams(sem):
    return pltpu.CompilerParams(dimension_semantics=sem, vmem_limit_bytes=VMEM_LIMIT_BYTES)


def _rms(x, g):
    ms = jnp.mean(x * x, axis=-1, keepdims=True)
    return x * lax.rsqrt(ms + EPS) * g


def _resident(shape, index_map):
    return pl.BlockSpec(shape, index_map, pipeline_mode=pl.Buffered(1))


def _dot(a, b):
    return jnp.dot(a, b, preferred_element_type=F32)


def _dot_nt(a, b):
    return lax.dot_general(a, b, (((1,), (1,)), ((), ())), preferred_element_type=F32)


def _dot_tn(a, b):
    return lax.dot_general(a, b, (((0,), (0,)), ((), ())), preferred_element_type=F32)


def _ffn_kernel(xa_ref, xb_ref, g_ref, wg_ref, wu_ref, wo_ref, o_ref, xbuf_ref, xn_ref, sem, *, n_a, n_b):
    tm = o_ref.shape[0]
    full_a, rem_a = divmod(n_a, tm)
    n_tiles = (n_a + n_b) // tm

    def tile_copies(tile, act):
        @pl.when(tile < full_a)
        def _():
            act(pltpu.make_async_copy(xa_ref.at[pl.ds(tile * tm, tm)], xbuf_ref, sem.at[0]))

        if n_b:
            @pl.when(tile == full_a)
            def _():
                if rem_a:
                    act(pltpu.make_async_copy(xa_ref.at[pl.ds(full_a * tm, rem_a)],
                                              xbuf_ref.at[pl.ds(0, rem_a)], sem.at[0]))
                act(pltpu.make_async_copy(xb_ref, xbuf_ref.at[pl.ds(rem_a, n_b)], sem.at[1]))

    @pl.when(pl.program_id(1) == 0)
    def _():
        i = pl.program_id(0)

        @pl.when(i == 0)
        def _():
            tile_copies(i, lambda cp: cp.start())

        tile_copies(i, lambda cp: cp.wait())
        x = xbuf_ref[...]
        o_ref[...] = x
        xn_ref[...] = _rms(x, g_ref[...]).astype(BF16)

        @pl.when(i + 1 < n_tiles)
        def _():
            tile_copies(i + 1, lambda cp: cp.start())

    xn = xn_ref[...]
    gate = _dot(xn, wg_ref[...].astype(BF16))
    up = _dot(xn, wu_ref[...].astype(BF16))
    act = (0.5 * gate * jax.nn.sigmoid(gate) * up).astype(BF16)
    o_ref[...] += _dot(act, wo_ref[...].astype(BF16))


def _ffn(xa, xb, norm_g, w_in, w_out, li):
    n_a, d = xa.shape
    n_b = 0 if xb is None else xb.shape[0]
    t = n_a + n_b
    assert t % TM_STREAM == 0 and (n_b == 0 or n_a % TM_STREAM + n_b == TM_STREAM)
    d_ff = w_out.shape[1]
    nf = d_ff // TF
    return pl.pallas_call(
        functools.partial(_ffn_kernel, n_a=n_a, n_b=n_b),
        grid=(t // TM_STREAM, nf),
        in_specs=[
            pl.BlockSpec(memory_space=pl.ANY),
            pl.BlockSpec(memory_space=pl.ANY),
            pl.BlockSpec((None, 1, d), lambda i, j: (li, 0, 0)),
            pl.BlockSpec((None, d, TF), lambda i, j: (li, 0, j)),
            pl.BlockSpec((None, d, TF), lambda i, j: (li, 0, j + nf)),
            pl.BlockSpec((None, TF, d), lambda i, j: (li, j, 0)),
        ],
        out_specs=pl.BlockSpec((TM_STREAM, d), lambda i, j: (i, 0)),
        out_shape=jax.ShapeDtypeStruct((t, d), F32),
        scratch_shapes=[pltpu.VMEM((TM_STREAM, d), F32), pltpu.VMEM((TM_STREAM, d), BF16),
                        pltpu.SemaphoreType.DMA((2,))],
        compiler_params=_cparams(("arbitrary", "arbitrary")),
        name=f"ffn_l{li}",
    )(xa, xa if xb is None else xb, norm_g, w_in, w_in, w_out)


def _ple_kernel(h_ref, pa_ref, pb_ref, g_ref, wgate_ref, wproj_ref, fn_ref, *o_refs, final, na):
    i = pl.program_id(0)
    h = h_ref[...]
    hn = _rms(h, g_ref[...]).astype(BF16)
    gate = jax.nn.sigmoid(_dot(hn, wgate_ref[...]))
    p = jnp.where(i < na, pa_ref[...], pb_ref[...])
    proj = _dot(p.astype(BF16), wproj_ref[...])
    out = h + proj * gate
    if not final:
        o_refs[0][...] = out
        return
    out = _rms(out, fn_ref[...])

    @pl.when(i < na)
    def _():
        o_refs[0][...] = out

    @pl.when(i >= na)
    def _():
        o_refs[1][...] = out


def _ple(h, pa, pb, norm_g, w_gate, w_proj, final_g, li, final):
    t, d = h.shape
    dp = pa.shape[-1]
    na = pa.shape[1] // TM
    in_a = lambda i: jnp.minimum(i, na - 1)
    in_b = lambda i: jnp.maximum(i - na, 0)
    if final:
        out_specs = [pl.BlockSpec((TM, d), lambda i: (in_a(i), 0)), pl.BlockSpec((TM, d), lambda i: (in_b(i), 0))]
        out_shape = [jax.ShapeDtypeStruct((pa.shape[1], d), F32), jax.ShapeDtypeStruct((pb.shape[1], d), F32)]
    else:
        out_specs = pl.BlockSpec((TM, d), lambda i: (i, 0))
        out_shape = jax.ShapeDtypeStruct((t, d), F32)
    return pl.pallas_call(
        functools.partial(_ple_kernel, final=final, na=na),
        grid=(t // TM,),
        in_specs=[
            pl.BlockSpec((TM, d), lambda i: (i, 0)),
            pl.BlockSpec((None, TM, dp), lambda i: (li, in_a(i), 0)),
            pl.BlockSpec((None, TM, dp), lambda i: (li, in_b(i), 0)),
            _resident((None, 1, d), lambda i: (li, 0, 0)),
            _resident((None, d, d), lambda i: (li, 0, 0)),
            _resident((None, dp, d), lambda i: (li, 0, 0)),
            _resident((1, d), lambda i: (0, 0)),
        ],
        out_specs=out_specs,
        out_shape=out_shape,
        compiler_params=_cparams(("arbitrary",)),
        name=f"ple_l{li}",
    )(h, pa, pb, norm_g, w_gate, w_proj, final_g)


def _gelu(z):
    return 0.5 * z * (1.0 + lax.erf(z * (2.0 ** -0.5)))


def _gmlp_proj_kernel(x_ref, g_ref, w_ref, lng_ref, lnb_ref, u_ref, v_ref, xn_ref):
    j = pl.program_id(1)

    @pl.when(j == 0)
    def _():
        xn_ref[...] = _rms(x_ref[...], g_ref[...]).astype(BF16)
        u_ref[...] = _gelu(_dot(xn_ref[...], w_ref[...])).astype(BF16)

    @pl.when(j == 1)
    def _():
        z = _gelu(_dot(xn_ref[...], w_ref[...]))
        mu = jnp.mean(z, axis=-1, keepdims=True)
        zc = z - mu
        var = jnp.mean(zc * zc, axis=-1, keepdims=True)
        v_ref[...] = zc * lax.rsqrt(var + EPS) * lng_ref[...] + lnb_ref[...]


def _gmlp_proj(h, norm_g, w_in, ln_g, ln_b, li, j_layer):
    t, d = h.shape
    d_a = w_in.shape[-1] // 2
    o_spec = pl.BlockSpec((TM, d_a), lambda i, j: (i, 0))
    return pl.pallas_call(
        _gmlp_proj_kernel,
        grid=(t // TM, 2),
        in_specs=[
            pl.BlockSpec((TM, d), lambda i, j: (i, 0)),
            pl.BlockSpec((None, 1, d), lambda i, j: (li, 0, 0)),
            pl.BlockSpec((None, d, d_a), lambda i, j: (j_layer, 0, j)),
            pl.BlockSpec((None, 1, d_a), lambda i, j: (j_layer, 0, 0)),
            pl.BlockSpec((None, 1, d_a), lambda i, j: (j_layer, 0, 0)),
        ],
        out_specs=[o_spec, o_spec],
        out_shape=[jax.ShapeDtypeStruct((t, d_a), BF16), jax.ShapeDtypeStruct((t, d_a), F32)],
        scratch_shapes=[pltpu.VMEM((TM, d), BF16)],
        compiler_params=_cparams(("parallel", "arbitrary")),
        name="gmlp_proj",
    )(h, norm_g, w_in, ln_g, ln_b)


def _gmlp_mix_kernel(h_ref, u_ref, v_ref, wmix_ref, bias_ref, wo_ref, o_ref, t_ref, *, groups):
    for c in range(TM // CHUNK_A):
        rows = slice(c * CHUNK_A, (c + 1) * CHUNK_A)
        for g in range(groups):
            cols = slice(g * LANES, (g + 1) * LANES)
            s = _dot(wmix_ref[g], v_ref[rows, cols].astype(BF16)) + bias_ref[:, cols]
            t_ref[rows, cols] = (u_ref[rows, cols].astype(F32) * s).astype(BF16)
    o_ref[...] = h_ref[...] + _dot(t_ref[...], wo_ref[...])


def _gmlp_mix(h, u, v, wmix, bias, w_out, j_layer, n_prompt_tiles):
    t, d = h.shape
    d_a = u.shape[1]
    groups = d_a // LANES
    return pl.pallas_call(
        functools.partial(_gmlp_mix_kernel, groups=groups),
        grid=(t // TM,),
        in_specs=[
            pl.BlockSpec((TM, d), lambda i: (i, 0)),
            pl.BlockSpec((TM, d_a), lambda i: (i, 0)),
            pl.BlockSpec((TM, d_a), lambda i: (i, 0)),
            pl.BlockSpec((None, groups, CHUNK_A, CHUNK_A), lambda i: (i // n_prompt_tiles, 0, 0, 0)),
            pl.BlockSpec((None, CHUNK_A, d_a), lambda i: (i // n_prompt_tiles, 0, 0)),
            _resident((None, d_a, d), lambda i: (j_layer, 0, 0)),
        ],
        out_specs=pl.BlockSpec((TM, d), lambda i: (i, 0)),
        out_shape=jax.ShapeDtypeStruct((t, d), F32),
        scratch_shapes=[pltpu.VMEM((TM, d_a), BF16)],
        compiler_params=_cparams(("parallel",)),
        name="gmlp_mix",
    )(h, u, v, wmix, bias, w_out)


def _hgrn_proj_kernel(x_hbm_ref, g_ref, wq_ref, wf_ref, wi_ref, wg_ref, lbl_ref,
                      q_ref, f_ref, i_ref, gt_ref, xbuf_ref, xn_ref, sem, *, li, n_tiles):
    tm = xn_ref.shape[0]

    def row_copy(tile):
        return pltpu.make_async_copy(x_hbm_ref.at[pl.ds(tile * tm, tm)], xbuf_ref, sem.at[0])

    @pl.when(pl.program_id(1) == 0)
    def _():
        i = pl.program_id(0)

        @pl.when(i == 0)
        def _():
            row_copy(i).start()

        row_copy(i).wait()
        xn_ref[...] = _rms(xbuf_ref[...], g_ref[...]).astype(BF16)

        @pl.when(i + 1 < n_tiles)
        def _():
            row_copy(i + 1).start()

    xn = xn_ref[...]
    zq = _dot(xn, wq_ref[...].astype(BF16))
    q_ref[...] = (zq * jax.nn.sigmoid(zq)).astype(BF16)

    logits = lbl_ref[...]
    ex = jnp.exp(logits - jnp.max(logits, axis=0, keepdims=True))
    sm = ex / jnp.sum(ex, axis=0, keepdims=True)
    lb = jnp.sum(sm[: li + 1], axis=0, keepdims=True) - sm[0:1]
    zf = _dot(xn, wf_ref[...].astype(BF16))
    f_ref[...] = lb + (1.0 - lb) * jax.nn.sigmoid(zf)

    i_ref[...] = _dot(xn, wi_ref[...].astype(BF16)).astype(BF16)

    zg = _dot(xn, wg_ref[...].astype(BF16))
    gt_ref[...] = (zg * jax.nn.sigmoid(zg)).astype(BF16)


def _hgrn_proj(h, norm_g, w_in, lb_logits, li, j_layer):
    t, d = h.shape
    d_b = w_in.shape[-1] // 4
    per_part = d_b // TNQ
    w_spec = lambda p: pl.BlockSpec((None, d, TNQ), lambda i, j, p=p: (j_layer, 0, p * per_part + j))
    o_spec = pl.BlockSpec((TM_STREAM, TNQ), lambda i, j: (i, j))
    bf = jax.ShapeDtypeStruct((t, d_b), BF16)
    return pl.pallas_call(
        functools.partial(_hgrn_proj_kernel, li=li, n_tiles=t // TM_STREAM),
        grid=(t // TM_STREAM, per_part),
        in_specs=[
            pl.BlockSpec(memory_space=pl.ANY),
            pl.BlockSpec((None, 1, d), lambda i, j: (li, 0, 0)),
            w_spec(0), w_spec(1), w_spec(2), w_spec(3),
            pl.BlockSpec((lb_logits.shape[0], TNQ), lambda i, j: (0, j)),
        ],
        out_specs=[o_spec, o_spec, o_spec, o_spec],
        out_shape=[bf, jax.ShapeDtypeStruct((t, d_b), F32), bf, bf],
        scratch_shapes=[pltpu.VMEM((TM_STREAM, d), F32), pltpu.VMEM((TM_STREAM, d), BF16),
                        pltpu.SemaphoreType.DMA((1,))],
        compiler_params=_cparams(("arbitrary", "arbitrary")),
        name="hgrn_proj",
    )(h, norm_g, w_in, w_in, w_in, w_in, lb_logits)


def _hgrn_halves():
    halves, half = [], HGRN_BLOCK // 2
    while half >= 1:
        halves.append(half)
        half //= 2
    return tuple(halves)


def _hgrn_norm_gate(o, ng, gate):
    ms = jnp.mean(o * o, axis=-1, keepdims=True)
    return (o * lax.rsqrt(ms + EPS) * ng * gate).astype(BF16)


def _hgrn_prompt_kernel(f_ref, q_ref, i_ref, g_ref, ng_ref, og_ref, sfin_ref,
                        st_ref, tril_ref, sgn_ref, mask_ref, *, nblk, hps):
    c = HGRN_BLOCK
    w = hps * LANES
    halves = _hgrn_halves()
    nl = len(halves)

    @pl.when((pl.program_id(0) == 0) & (pl.program_id(1) == 0))
    def _():
        r = lax.broadcasted_iota(jnp.int32, (c, c), 0)
        s = lax.broadcasted_iota(jnp.int32, (c, c), 1)
        tril_ref[...] = jnp.where(s <= r, 1.0, 0.0).astype(BF16)
        for lvl, half in enumerate(halves):
            par = 2 * half
            second = (r & half) != 0
            sgn_ref[lvl] = jnp.where(second, 1.0, -1.0)
            valid = second & ((s & half) == 0) & ((r // par) == (s // par))
            mask_ref[lvl] = jnp.where(valid, 1.0, 0.0).astype(BF16)
        mask_ref[nl] = jnp.where(r == s, 1.0, 0.0).astype(BF16)

    st_ref[...] = jnp.zeros_like(st_ref)
    low_sub = lax.broadcasted_iota(jnp.int32, (c // SUBLANES, SUBLANES, w), 1) < SUBLANES // 2

    def body(blk, carry):
        rows = pl.ds(pl.multiple_of(blk * c, c), c)
        f = f_ref[rows, :]
        qb = q_ref[rows, :]
        vb = i_ref[rows, :]
        q = qb.astype(F32)
        k = 1.0 - f
        kb = k.astype(BF16)

        x = jnp.log(f)
        hi = x.astype(BF16)
        r1 = x - hi.astype(F32)
        mid = r1.astype(BF16)
        lo = (r1 - mid.astype(F32)).astype(BF16)
        tril = tril_ref[...]
        b = _dot(tril, hi) + _dot(tril, mid) + _dot(tril, lo)
        b3 = b.reshape(c // SUBLANES, SUBLANES, w)

        att = [None] * hps

        def add_level(lvl, qt, kt):
            for hd in range(hps):
                cols = slice(hd * LANES, (hd + 1) * LANES)
                a = _dot_nt(qt[:, cols], kt[:, cols]).astype(BF16) * mask_ref[lvl]
                att[hd] = a if att[hd] is None else att[hd] + a

        for lvl, half in enumerate(halves):
            par = 2 * half
            if half == 1:
                add_level(lvl, (q * f).astype(BF16), kb)
                continue
            if half >= SUBLANES:
                pieces = [jnp.broadcast_to(b[p * par + half - 1:p * par + half, :], (par, w))
                          for p in range(c // par)]
                beta = pieces[0] if len(pieces) == 1 else jnp.concatenate(pieces, axis=0)
            elif half == SUBLANES // 2:
                beta = jnp.broadcast_to(b3[:, half - 1:half, :], b3.shape).reshape(c, w)
            else:
                beta = jnp.where(low_sub, b3[:, half - 1:half, :],
                                 b3[:, par + half - 1:par + half, :]).reshape(c, w)
            sgn = jnp.concatenate([sgn_ref[lvl]] * hps, axis=1)
            e = jnp.exp((b - beta) * sgn).astype(BF16)
            add_level(lvl, qb * e, kb * e)
        add_level(nl, qb, kb)

        qe = (q * jnp.exp(b)).astype(BF16)
        b_last = b[c - 1:c, :]
        kd = (k * jnp.exp(b_last - b)).astype(BF16)
        e_last = jnp.exp(b_last)
        gate = g_ref[rows, :].astype(F32)
        ng = ng_ref[...]
        for hd in range(hps):
            cols = slice(hd * LANES, (hd + 1) * LANES)
            st = st_ref[hd]
            o = _dot_nt(qe[:, cols], st.astype(BF16)) + _dot(att[hd], vb[:, cols])
            og_ref[rows, cols] = _hgrn_norm_gate(o, ng[:, cols], gate[:, cols])
            st_ref[hd] = st * e_last[:, cols] + _dot_tn(vb[:, cols], kd[:, cols])
        return carry

    lax.fori_loop(0, nblk, body, 0, unroll=2)
    for hd in range(hps):
        sfin_ref[hd] = st_ref[hd].T


def _hgrn_prompt(f, q, i, g, norm_g, batch, seq, heads):
    hps = HGRN_HEADS_PER_STEP
    w = hps * LANES
    ngrp = heads // hps
    nblk = seq // HGRN_BLOCK
    nl = len(_hgrn_halves())
    part = pl.BlockSpec((seq, w), lambda s, h: (s, h))
    return pl.pallas_call(
        functools.partial(_hgrn_prompt_kernel, nblk=nblk, hps=hps),
        grid=(batch, ngrp),
        in_specs=[part, part, part, part,
                  pl.BlockSpec((1, w), lambda s, h: (0, h))],
        out_specs=[pl.BlockSpec((seq, w), lambda s, h: (s, h)),
                   pl.BlockSpec((None, hps, LANES, LANES), lambda s, h: (s, h, 0, 0))],
        out_shape=[jax.ShapeDtypeStruct((batch * seq, heads * LANES), BF16),
                   jax.ShapeDtypeStruct((batch, heads, LANES, LANES), F32)],
        scratch_shapes=[pltpu.VMEM((hps, LANES, LANES), F32),
                        pltpu.VMEM((HGRN_BLOCK, HGRN_BLOCK), BF16),
                        pltpu.VMEM((nl, HGRN_BLOCK, LANES), F32),
                        pltpu.VMEM((nl + 1, HGRN_BLOCK, HGRN_BLOCK), BF16)],
        compiler_params=_cparams(("arbitrary", "arbitrary")),
        name="hgrn_prompt",
    )(f, q, i, g, norm_g)


def _hgrn_sample_kernel(f_ref, q_ref, i_ref, g_ref, ng_ref, s0_ref, og_ref, s1_ref, *, heads, dec_seq):
    rows_per_tile = SUBLANES
    per_tile = rows_per_tile // dec_seq
    row = lax.broadcasted_iota(jnp.int32, (rows_per_tile, LANES), 0)
    pos = row % dec_seq
    first = row < dec_seq
    zpad = jnp.zeros((rows_per_tile, LANES), F32)
    prow = lax.broadcasted_iota(jnp.int32, (4 * rows_per_tile, LANES), 0)
    in_pieces = prow < 3 * rows_per_tile
    sel_a = jnp.where(in_pieces & (prow % rows_per_tile == dec_seq - 1), 1.0, 0.0).astype(BF16)
    sel_b = jnp.where(in_pieces & (prow % rows_per_tile == 2 * dec_seq - 1), 1.0, 0.0).astype(BF16)
    f_all = f_ref[...]
    q_all = q_ref[...].astype(F32)
    v_all = i_ref[...].astype(F32)
    g_all = g_ref[...].astype(F32)
    ng = ng_ref[...]

    def pick(x, s):
        return jnp.where(first, x[s:s + 1, :], x[dec_seq + s:dec_seq + s + 1, :])

    for tile in range(SAMPLE_BATCH_TILE // per_tile):
        rows = slice(tile * rows_per_tile, (tile + 1) * rows_per_tile)
        for h in range(heads):
            cols = slice(h * LANES, (h + 1) * LANES)
            q = q_all[rows, cols]
            f = f_all[rows, cols]
            v = v_all[rows, cols]
            k = 1.0 - f
            b = jnp.log(f)
            shift = 1
            while shift < dec_seq:
                b = b + jnp.where(pos >= shift, pltpu.roll(b, shift, 0), 0.0)
                shift *= 2
            eb = jnp.exp(b)
            qe = (q * eb).astype(BF16)
            s_a = s0_ref[tile * per_tile, h]
            s_b = s0_ref[tile * per_tile + 1, h]
            o = jnp.where(first, _dot(qe, s_a.astype(BF16)), _dot(qe, s_b.astype(BF16)))
            for s in range(dec_seq):
                e = jnp.exp(jnp.minimum(b - pick(b, s), 0.0))
                wgt = jnp.sum(q * e * pick(k, s), axis=-1, keepdims=True)
                o = o + jnp.where(pos >= s, wgt, 0.0) * pick(v, s)
            og_ref[rows, cols] = _hgrn_norm_gate(o, ng[:, cols], g_all[rows, cols])

            b_last = pick(b, dec_seq - 1)
            kd = jnp.concatenate([k * jnp.exp(b_last - b), zpad], axis=0).astype(BF16)
            v_a = jnp.concatenate([jnp.where(first, v, 0.0), zpad], axis=0).astype(BF16)
            v_b = jnp.concatenate([jnp.where(first, 0.0, v), zpad], axis=0).astype(BF16)
            hi = eb.astype(BF16).astype(F32)
            mid = (eb - hi).astype(BF16).astype(F32)
            lo = eb - hi - mid
            pieces = jnp.concatenate([hi, mid, lo, zpad], axis=0).astype(BF16)
            s1_ref[tile * per_tile, h] = s_a * _dot_tn(pieces, sel_a) + _dot_tn(kd, v_a)
            s1_ref[tile * per_tile + 1, h] = s_b * _dot_tn(pieces, sel_b) + _dot_tn(kd, v_b)


def _hgrn_sample(f, q, i, g, norm_g, s0, n_prompt, dec_batch, dec_seq, heads):
    rows = SAMPLE_BATCH_TILE * dec_seq
    first_blk = n_prompt // rows
    d_b = heads * LANES
    part = pl.BlockSpec((rows, d_b), lambda i: (first_blk + i, 0))
    return pl.pallas_call(
        functools.partial(_hgrn_sample_kernel, heads=heads, dec_seq=dec_seq),
        grid=(dec_batch // SAMPLE_BATCH_TILE,),
        in_specs=[part, part, part, part,
                  _resident((1, d_b), lambda i: (0, 0)),
                  pl.BlockSpec((SAMPLE_BATCH_TILE, heads, LANES, LANES), lambda i: (i, 0, 0, 0))],
        out_specs=[pl.BlockSpec((rows, d_b), lambda i: (i, 0)),
                   pl.BlockSpec((SAMPLE_BATCH_TILE, heads, LANES, LANES), lambda i: (i, 0, 0, 0))],
        out_shape=[jax.ShapeDtypeStruct((dec_batch * dec_seq, d_b), BF16),
                   jax.ShapeDtypeStruct(s0.shape, F32)],
        compiler_params=_cparams(("parallel",)),
        name="hgrn_sample",
    )(f, q, i, g, norm_g, s0)


def _out_proj_kernel(h_ref, xa_ref, xb_ref, w_ref, o_ref, *, na):
    x = jnp.where(pl.program_id(0) < na, xa_ref[...], xb_ref[...])
    o_ref[...] = h_ref[...] + _dot(x, w_ref[...])


def _out_proj(h, xa, xb, w, j_layer):
    t, d = h.shape
    kdim = xa.shape[1]
    na = xa.shape[0] // TM
    return pl.pallas_call(
        functools.partial(_out_proj_kernel, na=na),
        grid=(t // TM,),
        in_specs=[
            pl.BlockSpec((TM, d), lambda i: (i, 0)),
            pl.BlockSpec((TM, kdim), lambda i: (jnp.minimum(i, na - 1), 0)),
            pl.BlockSpec((TM, kdim), lambda i: (jnp.maximum(i - na, 0), 0)),
            _resident((None, kdim, d), lambda i: (j_layer, 0, 0)),
        ],
        out_specs=pl.BlockSpec((TM, d), lambda i: (i, 0)),
        out_shape=jax.ShapeDtypeStruct((t, d), F32),
        compiler_params=_cparams(("parallel",)),
        name="hgrn_out_proj",
    )(h, xa, xb, w)


def kernel(x_prompt, x_sample, state_hgrn, p_prompt, p_sample, ffn1_w_in, ffn1_w_out, ffn2_w_in, ffn2_w_out, norm_ffn1, norm_mix, norm_ffn2, norm_ple, a_w_in, a_ln_g, a_ln_b, a_w_s, a_b_s, a_w_out, b_w_in, b_lb_logits, b_norm_g, b_w_out, ple_w_proj, ple_w_gate, final_norm):
    batch, seq, d = x_prompt.shape
    dec_batch, dec_seq, _ = x_sample.shape
    depth = ffn1_w_in.shape[0]
    n_prompt = batch * seq
    n_sample = dec_batch * dec_seq
    t = n_prompt + n_sample
    heads = b_norm_g.shape[-1] // LANES
    assert n_prompt % TM == 0 and n_sample % TM == 0 and t % TM_STREAM == 0
    assert seq % HGRN_BLOCK == 0 and heads % HGRN_HEADS_PER_STEP == 0
    assert CHUNK_A % dec_seq == 0 and SUBLANES % dec_seq == 0 and dec_batch % SAMPLE_BATCH_TILE == 0
    assert a_w_s.shape[-1] == CHUNK_A and seq % CHUNK_A == 0

    p_a = p_prompt.reshape(depth, n_prompt, -1)
    p_b = p_sample.reshape(depth, n_sample, -1)

    bf = lambda w: w.astype(BF16)
    vec = lambda g: g.reshape(g.shape[0], 1, g.shape[-1])
    a_w_in_b, a_w_out_b, b_w_out_b = bf(a_w_in), bf(a_w_out), bf(b_w_out)
    ple_w_proj_b, ple_w_gate_b = bf(ple_w_proj), bf(ple_w_gate)
    n_ffn1, n_mix, n_ffn2, n_ple = vec(norm_ffn1), vec(norm_mix), vec(norm_ffn2), vec(norm_ple)
    ln_g, ln_b = vec(a_ln_g), vec(a_ln_b)
    final_g = final_norm.reshape(1, d)

    reps = CHUNK_A // dec_seq
    tri = jnp.tril(jnp.ones((CHUNK_A, CHUNK_A), bool))
    pos_c = jnp.arange(CHUNK_A) // dec_seq
    blockdiag = pos_c[:, None] == pos_c[None, :]
    w_prompt = jnp.where(tri, a_w_s, 0.0)
    w_sample = jnp.where(tri & blockdiag, jnp.tile(a_w_s[:, :, :dec_seq, :dec_seq], (1, 1, reps, reps)), 0.0)
    wmix = jnp.stack([w_prompt, w_sample], axis=1).astype(BF16)
    b_prompt = jnp.swapaxes(a_b_s, 1, 2)
    b_sample = jnp.tile(b_prompt[:, :dec_seq], (1, reps, 1))
    bias = jnp.repeat(jnp.stack([b_prompt, b_sample], axis=1), LANES, axis=-1)

    new_v, s_prompt, s_sample = [], [], []
    h = None
    for li in range(depth):
        if li == 0:
            h = _ffn(x_prompt.reshape(n_prompt, d), x_sample.reshape(n_sample, d), n_ffn1, ffn1_w_in, ffn1_w_out, li)
        else:
            h = _ffn(h, None, n_ffn1, ffn1_w_in, ffn1_w_out, li)
        j = li // 2
        if li % 2 == 0:
            u, v = _gmlp_proj(h, n_mix, a_w_in_b, ln_g, ln_b, li, j)
            new_v.append(v[n_prompt:].reshape(dec_batch, dec_seq, -1))
            h = _gmlp_mix(h, u, v, wmix[j], bias[j], a_w_out_b, j, n_prompt // TM)
        else:
            q, f, iv, g = _hgrn_proj(h, n_mix, b_w_in, b_lb_logits, li, j)
            ng = b_norm_g[j].reshape(1, heads * LANES)
            og_p, s_fin = _hgrn_prompt(f, q, iv, g, ng, batch, seq, heads)
            og_s, s_new = _hgrn_sample(f, q, iv, g, ng, state_hgrn[j], n_prompt, dec_batch, dec_seq, heads)
            s_prompt.append(s_fin)
            s_sample.append(s_new)
            h = _out_proj(h, og_p, og_s, b_w_out_b, j)
        h = _ffn(h, None, n_ffn2, ffn2_w_in, ffn2_w_out, li)
        h = _ple(h, p_a, p_b, n_ple, ple_w_gate_b, ple_w_proj_b, final_g, li, final=(li == depth - 1))

    y_prompt = h[0].reshape(batch, seq, d)
    y_sample = h[1].reshape(dec_batch, dec_seq, d)
    return (y_prompt, y_sample, jnp.stack(s_prompt), jnp.stack(s_sample), jnp.stack(new_v))
```

```python
import functools

import jax
import jax.numpy as jnp
from jax import lax
from jax.experimental import pallas as pl
from jax.experimental.pallas import tpu as pltpu

F32 = jnp.float32
BF16 = jnp.bfloat16
EPS = 1e-6

LANES = 128
SUBLANES = 8
VMEM_LIMIT_BYTES = 63 * 1024 * 1024

TM = 512
TM_STREAM = 1088
TF = 512
TNQ = 512
HGRN_BLOCK = 128
HGRN_HEADS_PER_STEP = 8
CHUNK_A = 128
SAMPLE_BATCH_TILE = 4


def _cparams(sem):
    return pltpu.CompilerParams(dimension_semantics=sem, vmem_limit_bytes=VMEM_LIMIT_BYTES)


def _rms(x, g):
    ms = jnp.mean(x * x, axis=-1, keepdims=True)
    return x * lax.rsqrt(ms + EPS) * g


def _resident(shape, index_map):
    return pl.BlockSpec(shape, index_map, pipeline_mode=pl.Buffered(1))


def _dot(a, b):
    return jnp.dot(a, b, preferred_element_type=F32)


def _dot_nt(a, b):
    return lax.dot_general(a, b, (((1,), (1,)), ((), ())), preferred_element_type=F32)


def _dot_tn(a, b):
    return lax.dot_general(a, b, (((0,), (0,)), ((), ())), preferred_element_type=F32)


def _ffn_kernel(xa_ref, xb_ref, g_ref, wg_ref, wu_ref, wo_ref, o_ref, xbuf_ref, xn_ref, sem, *, n_a, n_b):
    tm = o_ref.shape[0]
    full_a, rem_a = divmod(n_a, tm)
    n_tiles = (n_a + n_b) // tm

    def tile_copies(tile, act):
        @pl.when(tile < full_a)
        def _():
            act(pltpu.make_async_copy(xa_ref.at[pl.ds(tile * tm, tm)], xbuf_ref, sem.at[0]))

        if n_b:
            @pl.when(tile == full_a)
            def _():
                if rem_a:
                    act(pltpu.make_async_copy(xa_ref.at[pl.ds(full_a * tm, rem_a)],
                                              xbuf_ref.at[pl.ds(0, rem_a)], sem.at[0]))
                act(pltpu.make_async_copy(xb_ref, xbuf_ref.at[pl.ds(rem_a, n_b)], sem.at[1]))

    @pl.when(pl.program_id(1) == 0)
    def _():
        i = pl.program_id(0)

        @pl.when(i == 0)
        def _():
            tile_copies(i, lambda cp: cp.start())

        tile_copies(i, lambda cp: cp.wait())
        x = xbuf_ref[...]
        o_ref[...] = x
        xn_ref[...] = _rms(x, g_ref[...]).astype(BF16)

        @pl.when(i + 1 < n_tiles)
        def _():
            tile_copies(i + 1, lambda cp: cp.start())

    xn = xn_ref[...]
    gate = _dot(xn, wg_ref[...].astype(BF16))
    up = _dot(xn, wu_ref[...].astype(BF16))
    act = (0.5 * gate * jax.nn.sigmoid(gate) * up).astype(BF16)
    o_ref[...] += _dot(act, wo_ref[...].astype(BF16))


def _ffn(xa, xb, norm_g, w_in, w_out, li):
    n_a, d = xa.shape
    n_b = 0 if xb is None else xb.shape[0]
    t = n_a + n_b
    assert t % TM_STREAM == 0 and (n_b == 0 or n_a % TM_STREAM + n_b == TM_STREAM)
    d_ff = w_out.shape[1]
    nf = d_ff // TF
    return pl.pallas_call(
        functools.partial(_ffn_kernel, n_a=n_a, n_b=n_b),
        grid=(t // TM_STREAM, nf),
        in_specs=[
            pl.BlockSpec(memory_space=pl.ANY),
            pl.BlockSpec(memory_space=pl.ANY),
            pl.BlockSpec((None, 1, d), lambda i, j: (li, 0, 0)),
            pl.BlockSpec((None, d, TF), lambda i, j: (li, 0, j)),
            pl.BlockSpec((None, d, TF), lambda i, j: (li, 0, j + nf)),
            pl.BlockSpec((None, TF, d), lambda i, j: (li, j, 0)),
        ],
        out_specs=pl.BlockSpec((TM_STREAM, d), lambda i, j: (i, 0)),
        out_shape=jax.ShapeDtypeStruct((t, d), F32),
        scratch_shapes=[pltpu.VMEM((TM_STREAM, d), F32), pltpu.VMEM((TM_STREAM, d), BF16),
                        pltpu.SemaphoreType.DMA((2,))],
        compiler_params=_cparams(("arbitrary", "arbitrary")),
        name=f"ffn_l{li}",
    )(xa, xa if xb is None else xb, norm_g, w_in, w_in, w_out)


def _ple_kernel(h_ref, pa_ref, pb_ref, g_ref, wgate_ref, wproj_ref, fn_ref, *o_refs, final, na):
    i = pl.program_id(0)
    h = h_ref[...]
    hn = _rms(h, g_ref[...]).astype(BF16)
    gate = jax.nn.sigmoid(_dot(hn, wgate_ref[...]))
    p = jnp.where(i < na, pa_ref[...], pb_ref[...])
    proj = _dot(p.astype(BF16), wproj_ref[...])
    out = h + proj * gate
    if not final:
        o_refs[0][...] = out
        return
    out = _rms(out, fn_ref[...])

    @pl.when(i < na)
    def _():
        o_refs[0][...] = out

    @pl.when(i >= na)
    def _():
        o_refs[1][...] = out


def _ple(h, pa, pb, norm_g, w_gate, w_proj, final_g, li, final):
    t, d = h.shape
    dp = pa.shape[-1]
    na = pa.shape[1] // TM
    in_a = lambda i: jnp.minimum(i, na - 1)
    in_b = lambda i: jnp.maximum(i - na, 0)
    if final:
        out_specs = [pl.BlockSpec((TM, d), lambda i: (in_a(i), 0)), pl.BlockSpec((TM, d), lambda i: (in_b(i), 0))]
        out_shape = [jax.ShapeDtypeStruct((pa.shape[1], d), F32), jax.ShapeDtypeStruct((pb.shape[1], d), F32)]
    else:
        out_specs = pl.BlockSpec((TM, d), lambda i: (i, 0))
        out_shape = jax.ShapeDtypeStruct((t, d), F32)
    return pl.pallas_call(
        functools.partial(_ple_kernel, final=final, na=na),
        grid=(t // TM,),
        in_specs=[
            pl.BlockSpec((TM, d), lambda i: (i, 0)),
            pl.BlockSpec((None, TM, dp), lambda i: (li, in_a(i), 0)),
            pl.BlockSpec((None, TM, dp), lambda i: (li, in_b(i), 0)),
            _resident((None, 1, d), lambda i: (li, 0, 0)),
            _resident((None, d, d), lambda i: (li, 0, 0)),
            _resident((None, dp, d), lambda i: (li, 0, 0)),
            _resident((1, d), lambda i: (0, 0)),
        ],
        out_specs=out_specs,
        out_shape=out_shape,
        compiler_params=_cparams(("arbitrary",)),
        name=f"ple_l{li}",
    )(h, pa, pb, norm_g, w_gate, w_proj, final_g)


def _gelu(z):
    return 0.5 * z * (1.0 + lax.erf(z * (2.0 ** -0.5)))


def _gmlp_proj_kernel(x_ref, g_ref, w_ref, lng_ref, lnb_ref, u_ref, v_ref, xn_ref):
    j = pl.program_id(1)

    @pl.when(j == 0)
    def _():
        xn_ref[...] = _rms(x_ref[...], g_ref[...]).astype(BF16)
        u_ref[...] = _gelu(_dot(xn_ref[...], w_ref[...])).astype(BF16)

    @pl.when(j == 1)
    def _():
        z = _gelu(_dot(xn_ref[...], w_ref[...]))
        mu = jnp.mean(z, axis=-1, keepdims=True)
        zc = z - mu
        var = jnp.mean(zc * zc, axis=-1, keepdims=True)
        v_ref[...] = zc * lax.rsqrt(var + EPS) * lng_ref[...] + lnb_ref[...]


def _gmlp_proj(h, norm_g, w_in, ln_g, ln_b, li, j_layer):
    t, d = h.shape
    d_a = w_in.shape[-1] // 2
    o_spec = pl.BlockSpec((TM, d_a), lambda i, j: (i, 0))
    return pl.pallas_call(
        _gmlp_proj_kernel,
        grid=(t // TM, 2),
        in_specs=[
            pl.BlockSpec((TM, d), lambda i, j: (i, 0)),
            pl.BlockSpec((None, 1, d), lambda i, j: (li, 0, 0)),
            pl.BlockSpec((None, d, d_a), lambda i, j: (j_layer, 0, j)),
            pl.BlockSpec((None, 1, d_a), lambda i, j: (j_layer, 0, 0)),
            pl.BlockSpec((None, 1, d_a), lambda i, j: (j_layer, 0, 0)),
        ],
        out_specs=[o_spec, o_spec],
        out_shape=[jax.ShapeDtypeStruct((t, d_a), BF16), jax.ShapeDtypeStruct((t, d_a), F32)],
        scratch_shapes=[pltpu.VMEM((TM, d), BF16)],
        compiler_params=_cparams(("parallel", "arbitrary")),
        name="gmlp_proj",
    )(h, norm_g, w_in, ln_g, ln_b)


def _gmlp_mix_kernel(h_ref, u_ref, v_ref, wmix_ref, bias_ref, wo_ref, o_ref, t_ref, *, groups):
    for c in range(TM // CHUNK_A):
        rows = slice(c * CHUNK_A, (c + 1) * CHUNK_A)
        for g in range(groups):
            cols = slice(g * LANES, (g + 1) * LANES)
            s = _dot(wmix_ref[g], v_ref[rows, cols].astype(BF16)) + bias_ref[:, cols]
            t_ref[rows, cols] = (u_ref[rows, cols].astype(F32) * s).astype(BF16)
    o_ref[...] = h_ref[...] + _dot(t_ref[...], wo_ref[...])


def _gmlp_mix(h, u, v, wmix, bias, w_out, j_layer, n_prompt_tiles):
    t, d = h.shape
    d_a = u.shape[1]
    groups = d_a // LANES
    return pl.pallas_call(
        functools.partial(_gmlp_mix_kernel, groups=groups),
        grid=(t // TM,),
        in_specs=[
            pl.BlockSpec((TM, d), lambda i: (i, 0)),
            pl.BlockSpec((TM, d_a), lambda i: (i, 0)),
            pl.BlockSpec((TM, d_a), lambda i: (i, 0)),
            pl.BlockSpec((None, groups, CHUNK_A, CHUNK_A), lambda i: (i // n_prompt_tiles, 0, 0, 0)),
            pl.BlockSpec((None, CHUNK_A, d_a), lambda i: (i // n_prompt_tiles, 0, 0)),
            _resident((None, d_a, d), lambda i: (j_layer, 0, 0)),
        ],
        out_specs=pl.BlockSpec((TM, d), lambda i: (i, 0)),
        out_shape=jax.ShapeDtypeStruct((t, d), F32),
        scratch_shapes=[pltpu.VMEM((TM, d_a), BF16)],
        compiler_params=_cparams(("parallel",)),
        name="gmlp_mix",
    )(h, u, v, wmix, bias, w_out)


def _hgrn_proj_kernel(x_hbm_ref, g_ref, wq_ref, wf_ref, wi_ref, wg_ref, lbl_ref,
                      q_ref, f_ref, i_ref, gt_ref, xbuf_ref, xn_ref, sem, *, li, n_tiles):
    tm = xn_ref.shape[0]

    def row_copy(tile):
        return pltpu.make_async_copy(x_hbm_ref.at[pl.ds(tile * tm, tm)], xbuf_ref, sem.at[0])

    @pl.when(pl.program_id(1) == 0)
    def _():
        i = pl.program_id(0)

        @pl.when(i == 0)
        def _():
            row_copy(i).start()

        row_copy(i).wait()
        xn_ref[...] = _rms(xbuf_ref[...], g_ref[...]).astype(BF16)

        @pl.when(i + 1 < n_tiles)
        def _():
            row_copy(i + 1).start()

    xn = xn_ref[...]
    zq = _dot(xn, wq_ref[...].astype(BF16))
    q_ref[...] = (zq * jax.nn.sigmoid(zq)).astype(BF16)

    logits = lbl_ref[...]
    ex = jnp.exp(logits - jnp.max(logits, axis=0, keepdims=True))
    sm = ex / jnp.sum(ex, axis=0, keepdims=True)
    lb = jnp.sum(sm[: li + 1], axis=0, keepdims=True) - sm[0:1]
    zf = _dot(xn, wf_ref[...].astype(BF16))
    f_ref[...] = lb + (1.0 - lb) * jax.nn.sigmoid(zf)

    i_ref[...] = _dot(xn, wi_ref[...].astype(BF16)).astype(BF16)

    zg = _dot(xn, wg_ref[...].astype(BF16))
    gt_ref[...] = (zg * jax.nn.sigmoid(zg)).astype(BF16)


def _hgrn_proj(h, norm_g, w_in, lb_logits, li, j_layer):
    t, d = h.shape
    d_b = w_in.shape[-1] // 4
    per_part = d_b // TNQ
    w_spec = lambda p: pl.BlockSpec((None, d, TNQ), lambda i, j, p=p: (j_layer, 0, p * per_part + j))
    o_spec = pl.BlockSpec((TM_STREAM, TNQ), lambda i, j: (i, j))
    bf = jax.ShapeDtypeStruct((t, d_b), BF16)
    return pl.pallas_call(
        functools.partial(_hgrn_proj_kernel, li=li, n_tiles=t // TM_STREAM),
        grid=(t // TM_STREAM, per_part),
        in_specs=[
            pl.BlockSpec(memory_space=pl.ANY),
            pl.BlockSpec((None, 1, d), lambda i, j: (li, 0, 0)),
            w_spec(0), w_spec(1), w_spec(2), w_spec(3),
            pl.BlockSpec((lb_logits.shape[0], TNQ), lambda i, j: (0, j)),
        ],
        out_specs=[o_spec, o_spec, o_spec, o_spec],
        out_shape=[bf, jax.ShapeDtypeStruct((t, d_b), F32), bf, bf],
        scratch_shapes=[pltpu.VMEM((TM_STREAM, d), F32), pltpu.VMEM((TM_STREAM, d), BF16),
                        pltpu.SemaphoreType.DMA((1,))],
        compiler_params=_cparams(("arbitrary", "arbitrary")),
        name="hgrn_proj",
    )(h, norm_g, w_in, w_in, w_in, w_in, lb_logits)


def _hgrn_halves():
    halves, half = [], HGRN_BLOCK // 2
    while half >= 1:
        halves.append(half)
        half //= 2
    return tuple(halves)


def _hgrn_norm_gate(o, ng, gate):
    ms = jnp.mean(o * o, axis=-1, keepdims=True)
    return (o * lax.rsqrt(ms + EPS) * ng * gate).astype(BF16)


def _hgrn_prompt_kernel(f_ref, q_ref, i_ref, g_ref, ng_ref, og_ref, sfin_ref,
                        st_ref, tril_ref, sgn_ref, mask_ref, *, nblk, hps):
    c = HGRN_BLOCK
    w = hps * LANES
    halves = _hgrn_halves()
    nl = len(halves)

    @pl.when((pl.program_id(0) == 0) & (pl.program_id(1) == 0))
    def _():
        r = lax.broadcasted_iota(jnp.int32, (c, c), 0)
        s = lax.broadcasted_iota(jnp.int32, (c, c), 1)
        tril_ref[...] = jnp.where(s <= r, 1.0, 0.0).astype(BF16)
        for lvl, half in enumerate(halves):
            par = 2 * half
            second = (r & half) != 0
            sgn_ref[lvl] = jnp.where(second, 1.0, -1.0)
            valid = second & ((s & half) == 0) & ((r // par) == (s // par))
            mask_ref[lvl] = jnp.where(valid, 1.0, 0.0).astype(BF16)
        mask_ref[nl] = jnp.where(r == s, 1.0, 0.0).astype(BF16)

    st_ref[...] = jnp.zeros_like(st_ref)
    low_sub = lax.broadcasted_iota(jnp.int32, (c // SUBLANES, SUBLANES, w), 1) < SUBLANES // 2

    def body(blk, carry):
        rows = pl.ds(pl.multiple_of(blk * c, c), c)
        f = f_ref[rows, :]
        qb = q_ref[rows, :]
        vb = i_ref[rows, :]
        q = qb.astype(F32)
        k = 1.0 - f
        kb = k.astype(BF16)

        x = jnp.log(f)
        hi = x.astype(BF16)
        r1 = x - hi.astype(F32)
        mid = r1.astype(BF16)
        lo = (r1 - mid.astype(F32)).astype(BF16)
        tril = tril_ref[...]
        b = _dot(tril, hi) + _dot(tril, mid) + _dot(tril, lo)
        b3 = b.reshape(c // SUBLANES, SUBLANES, w)

        att = [None] * hps

        def add_level(lvl, qt, kt):
            for hd in range(hps):
                cols = slice(hd * LANES, (hd + 1) * LANES)
                a = _dot_nt(qt[:, cols], kt[:, cols]).astype(BF16) * mask_ref[lvl]
                att[hd] = a if att[hd] is None else att[hd] + a

        for lvl, half in enumerate(halves):
            par = 2 * half
            if half == 1:
                add_level(lvl, (q * f).astype(BF16), kb)
                continue
            if half >= SUBLANES:
                pieces = [jnp.broadcast_to(b[p * par + half - 1:p * par + half, :], (par, w))
                          for p in range(c // par)]
                beta = pieces[0] if len(pieces) == 1 else jnp.concatenate(pieces, axis=0)
            elif half == SUBLANES // 2:
                beta = jnp.broadcast_to(b3[:, half - 1:half, :], b3.shape).reshape(c, w)
            else:
                beta = jnp.where(low_sub, b3[:, half - 1:half, :],
                                 b3[:, par + half - 1:par + half, :]).reshape(c, w)
            sgn = jnp.concatenate([sgn_ref[lvl]] * hps, axis=1)
            e = jnp.exp((b - beta) * sgn).astype(BF16)
            add_level(lvl, qb * e, kb * e)
        add_level(nl, qb, kb)

        qe = (q * jnp.exp(b)).astype(BF16)
        b_last = b[c - 1:c, :]
        kd = (k * jnp.exp(b_last - b)).astype(BF16)
        e_last = jnp.exp(b_last)
        gate = g_ref[rows, :].astype(F32)
        ng = ng_ref[...]
        for hd in range(hps):
            cols = slice(hd * LANES, (hd + 1) * LANES)
            st = st_ref[hd]
            o = _dot_nt(qe[:, cols], st.astype(BF16)) + _dot(att[hd], vb[:, cols])
            og_ref[rows, cols] = _hgrn_norm_gate(o, ng[:, cols], gate[:, cols])
            st_ref[hd] = st * e_last[:, cols] + _dot_tn(vb[:, cols], kd[:, cols])
        return carry

    lax.fori_loop(0, nblk, body, 0, unroll=4)
    for hd in range(hps):
        sfin_ref[hd] = st_ref[hd].T


def _hgrn_prompt(f, q, i, g, norm_g, batch, seq, heads):
    hps = HGRN_HEADS_PER_STEP
    w = hps * LANES
    ngrp = heads // hps
    nblk = seq // HGRN_BLOCK
    nl = len(_hgrn_halves())
    part = pl.BlockSpec((seq, w), lambda s, h: (s, h))
    return pl.pallas_call(
        functools.partial(_hgrn_prompt_kernel, nblk=nblk, hps=hps),
        grid=(batch, ngrp),
        in_specs=[part, part, part, part,
                  pl.BlockSpec((1, w), lambda s, h: (0, h))],
        out_specs=[pl.BlockSpec((seq, w), lambda s, h: (s, h)),
                   pl.BlockSpec((None, hps, LANES, LANES), lambda s, h: (s, h, 0, 0))],
        out_shape=[jax.ShapeDtypeStruct((batch * seq, heads * LANES), BF16),
                   jax.ShapeDtypeStruct((batch, heads, LANES, LANES), F32)],
        scratch_shapes=[pltpu.VMEM((hps, LANES, LANES), F32),
                        pltpu.VMEM((HGRN_BLOCK, HGRN_BLOCK), BF16),
                        pltpu.VMEM((nl, HGRN_BLOCK, LANES), F32),
                        pltpu.VMEM((nl + 1, HGRN_BLOCK, HGRN_BLOCK), BF16)],
        compiler_params=_cparams(("arbitrary", "arbitrary")),
        name="hgrn_prompt",
    )(f, q, i, g, norm_g)


def _hgrn_sample_kernel(f_ref, q_ref, i_ref, g_ref, ng_ref, s0_ref, og_ref, s1_ref, *, heads, dec_seq):
    rows_per_tile = SUBLANES
    per_tile = rows_per_tile // dec_seq
    row = lax.broadcasted_iota(jnp.int32, (rows_per_tile, LANES), 0)
    pos = row % dec_seq
    first = row < dec_seq
    zpad = jnp.zeros((rows_per_tile, LANES), F32)
    prow = lax.broadcasted_iota(jnp.int32, (4 * rows_per_tile, LANES), 0)
    in_pieces = prow < 3 * rows_per_tile
    sel_a = jnp.where(in_pieces & (prow % rows_per_tile == dec_seq - 1), 1.0, 0.0).astype(BF16)
    sel_b = jnp.where(in_pieces & (prow % rows_per_tile == 2 * dec_seq - 1), 1.0, 0.0).astype(BF16)
    f_all = f_ref[...]
    q_all = q_ref[...].astype(F32)
    v_all = i_ref[...].astype(F32)
    g_all = g_ref[...].astype(F32)
    ng = ng_ref[...]

    def pick(x, s):
        return jnp.where(first, x[s:s + 1, :], x[dec_seq + s:dec_seq + s + 1, :])

    for tile in range(SAMPLE_BATCH_TILE // per_tile):
        rows = slice(tile * rows_per_tile, (tile + 1) * rows_per_tile)
        for h in range(heads):
            cols = slice(h * LANES, (h + 1) * LANES)
            q = q_all[rows, cols]
            f = f_all[rows, cols]
            v = v_all[rows, cols]
            k = 1.0 - f
            b = jnp.log(f)
            shift = 1
            while shift < dec_seq:
                b = b + jnp.where(pos >= shift, pltpu.roll(b, shift, 0), 0.0)
                shift *= 2
            eb = jnp.exp(b)
            qe = (q * eb).astype(BF16)
            s_a = s0_ref[tile * per_tile, h]
            s_b = s0_ref[tile * per_tile + 1, h]
            o = jnp.where(first, _dot(qe, s_a.astype(BF16)), _dot(qe, s_b.astype(BF16)))
            for s in range(dec_seq):
                e = jnp.exp(jnp.minimum(b - pick(b, s), 0.0))
                wgt = jnp.sum(q * e * pick(k, s), axis=-1, keepdims=True)
                o = o + jnp.where(pos >= s, wgt, 0.0) * pick(v, s)
            og_ref[rows, cols] = _hgrn_norm_gate(o, ng[:, cols], g_all[rows, cols])

            b_last = pick(b, dec_seq - 1)
            kd = jnp.concatenate([k * jnp.exp(b_last - b), zpad], axis=0).astype(BF16)
            v_a = jnp.concatenate([jnp.where(first, v, 0.0), zpad], axis=0).astype(BF16)
            v_b = jnp.concatenate([jnp.where(first, 0.0, v), zpad], axis=0).astype(BF16)
            hi = eb.astype(BF16).astype(F32)
            mid = (eb - hi).astype(BF16).astype(F32)
            lo = eb - hi - mid
            pieces = jnp.concatenate([hi, mid, lo, zpad], axis=0).astype(BF16)
            s1_ref[tile * per_tile, h] = s_a * _dot_tn(pieces, sel_a) + _dot_tn(kd, v_a)
            s1_ref[tile * per_tile + 1, h] = s_b * _dot_tn(pieces, sel_b) + _dot_tn(kd, v_b)


def _hgrn_sample(f, q, i, g, norm_g, s0, n_prompt, dec_batch, dec_seq, heads):
    rows = SAMPLE_BATCH_TILE * dec_seq
    first_blk = n_prompt // rows
    d_b = heads * LANES
    part = pl.BlockSpec((rows, d_b), lambda i: (first_blk + i, 0))
    return pl.pallas_call(
        functools.partial(_hgrn_sample_kernel, heads=heads, dec_seq=dec_seq),
        grid=(dec_batch // SAMPLE_BATCH_TILE,),
        in_specs=[part, part, part, part,
                  _resident((1, d_b), lambda i: (0, 0)),
                  pl.BlockSpec((SAMPLE_BATCH_TILE, heads, LANES, LANES), lambda i: (i, 0, 0, 0))],
        out_specs=[pl.BlockSpec((rows, d_b), lambda i: (i, 0)),
                   pl.BlockSpec((SAMPLE_BATCH_TILE, heads, LANES, LANES), lambda i: (i, 0, 0, 0))],
        out_shape=[jax.ShapeDtypeStruct((dec_batch * dec_seq, d_b), BF16),
                   jax.ShapeDtypeStruct(s0.shape, F32)],
        compiler_params=_cparams(("parallel",)),
        name="hgrn_sample",
    )(f, q, i, g, norm_g, s0)


def _out_proj_kernel(h_ref, xa_ref, xb_ref, w_ref, o_ref, *, na):
    x = jnp.where(pl.program_id(0) < na, xa_ref[...], xb_ref[...])
    o_ref[...] = h_ref[...] + _dot(x, w_ref[...])


def _out_proj(h, xa, xb, w, j_layer):
    t, d = h.shape
    kdim = xa.shape[1]
    na = xa.shape[0] // TM
    return pl.pallas_call(
        functools.partial(_out_proj_kernel, na=na),
        grid=(t // TM,),
        in_specs=[
            pl.BlockSpec((TM, d), lambda i: (i, 0)),
            pl.BlockSpec((TM, kdim), lambda i: (jnp.minimum(i, na - 1), 0)),
            pl.BlockSpec((TM, kdim), lambda i: (jnp.maximum(i - na, 0), 0)),
            _resident((None, kdim, d), lambda i: (j_layer, 0, 0)),
        ],
        out_specs=pl.BlockSpec((TM, d), lambda i: (i, 0)),
        out_shape=jax.ShapeDtypeStruct((t, d), F32),
        compiler_params=_cparams(("parallel",)),
        name="hgrn_out_proj",
    )(h, xa, xb, w)


def kernel(x_prompt, x_sample, state_hgrn, p_prompt, p_sample, ffn1_w_in, ffn1_w_out, ffn2_w_in, ffn2_w_out, norm_ffn1, norm_mix, norm_ffn2, norm_ple, a_w_in, a_ln_g, a_ln_b, a_w_s, a_b_s, a_w_out, b_w_in, b_lb_logits, b_norm_g, b_w_out, ple_w_proj, ple_w_gate, final_norm):
    batch, seq, d = x_prompt.shape
    dec_batch, dec_seq, _ = x_sample.shape
    depth = ffn1_w_in.shape[0]
    n_prompt = batch * seq
    n_sample = dec_batch * dec_seq
    t = n_prompt + n_sample
    heads = b_norm_g.shape[-1] // LANES
    assert n_prompt % TM == 0 and n_sample % TM == 0 and t % TM_STREAM == 0
    assert seq % HGRN_BLOCK == 0 and heads % HGRN_HEADS_PER_STEP == 0
    assert CHUNK_A % dec_seq == 0 and SUBLANES % dec_seq == 0 and dec_batch % SAMPLE_BATCH_TILE == 0
    assert a_w_s.shape[-1] == CHUNK_A and seq % CHUNK_A == 0

    p_a = p_prompt.reshape(depth, n_prompt, -1)
    p_b = p_sample.reshape(depth, n_sample, -1)

    bf = lambda w: w.astype(BF16)
    vec = lambda g: g.reshape(g.shape[0], 1, g.shape[-1])
    a_w_in_b, a_w_out_b, b_w_out_b = bf(a_w_in), bf(a_w_out), bf(b_w_out)
    ple_w_proj_b, ple_w_gate_b = bf(ple_w_proj), bf(ple_w_gate)
    n_ffn1, n_mix, n_ffn2, n_ple = vec(norm_ffn1), vec(norm_mix), vec(norm_ffn2), vec(norm_ple)
    ln_g, ln_b = vec(a_ln_g), vec(a_ln_b)
    final_g = final_norm.reshape(1, d)

    reps = CHUNK_A // dec_seq
    tri = jnp.tril(jnp.ones((CHUNK_A, CHUNK_A), bool))
    pos_c = jnp.arange(CHUNK_A) // dec_seq
    blockdiag = pos_c[:, None] == pos_c[None, :]
    w_prompt = jnp.where(tri, a_w_s, 0.0)
    w_sample = jnp.where(tri & blockdiag, jnp.tile(a_w_s[:, :, :dec_seq, :dec_seq], (1, 1, reps, reps)), 0.0)
    wmix = jnp.stack([w_prompt, w_sample], axis=1).astype(BF16)
    b_prompt = jnp.swapaxes(a_b_s, 1, 2)
    b_sample = jnp.tile(b_prompt[:, :dec_seq], (1, reps, 1))
    bias = jnp.repeat(jnp.stack([b_prompt, b_sample], axis=1), LANES, axis=-1)

    new_v, s_prompt, s_sample = [], [], []
    h = None
    for li in range(depth):
        if li == 0:
            h = _ffn(x_prompt.reshape(n_prompt, d), x_sample.reshape(n_sample, d), n_ffn1, ffn1_w_in, ffn1_w_out, li)
        else:
            h = _ffn(h, None, n_ffn1, ffn1_w_in, ffn1_w_out, li)
        j = li // 2
        if li % 2 == 0:
            u, v = _gmlp_proj(h, n_mix, a_w_in_b, ln_g, ln_b, li, j)
            new_v.append(v[n_prompt:].reshape(dec_batch, dec_seq, -1))
            h = _gmlp_mix(h, u, v, wmix[j], bias[j], a_w_out_b, j, n_prompt // TM)
        else:
            q, f, iv, g = _hgrn_proj(h, n_mix, b_w_in, b_lb_logits, li, j)
            ng = b_norm_g[j].reshape(1, heads * LANES)
            og_p, s_fin = _hgrn_prompt(f, q, iv, g, ng, batch, seq, heads)
            og_s, s_new = _hgrn_sample(f, q, iv, g, ng, state_hgrn[j], n_prompt, dec_batch, dec_seq, heads)
            s_prompt.append(s_fin)
            s_sample.append(s_new)
            h = _out_proj(h, og_p, og_s, b_w_out_b, j)
        h = _ffn(h, None, n_ffn2, ffn2_w_in, ffn2_w_out, li)
        h = _ple(h, p_a, p_b, n_ple, ple_w_gate_b, ple_w_proj_b, final_g, li, final=(li == depth - 1))

    y_prompt = h[0].reshape(batch, seq, d)
    y_sample = h[1].reshape(dec_batch, dec_seq, d)
    return (y_prompt, y_sample, jnp.stack(s_prompt), jnp.stack(s_sample), jnp.stack(new_v))
```

```python
import functools

import jax
import jax.numpy as jnp
from jax import lax
from jax.experimental import pallas as pl
from jax.experimental.pallas import tpu as pltpu

F32 = jnp.float32
BF16 = jnp.bfloat16
EPS = 1e-6

LANES = 128
SUBLANES = 8
VMEM_LIMIT_BYTES = 63 * 1024 * 1024

TM = 512
TM_STREAM = 1088
TF = 512
TNQ = 512
HGRN_BLOCK = 128
HGRN_HEADS_PER_STEP = 8
CHUNK_A = 128
SAMPLE_BATCH_TILE = 4


def _cparams(sem):
    return pltpu.CompilerParams(dimension_semantics=sem, vmem_limit_bytes=VMEM_LIMIT_BYTES)


def _rms(x, g):
    ms = jnp.mean(x * x, axis=-1, keepdims=True)
    return x * lax.rsqrt(ms + EPS) * g


def _resident(shape, index_map):
    return pl.BlockSpec(shape, index_map, pipeline_mode=pl.Buffered(1))


def _dot(a, b):
    return jnp.dot(a, b, preferred_element_type=F32)


def _dot_nt(a, b):
    return lax.dot_general(a, b, (((1,), (1,)), ((), ())), preferred_element_type=F32)


def _dot_tn(a, b):
    return lax.dot_general(a, b, (((0,), (0,)), ((), ())), preferred_element_type=F32)


def _ffn_kernel(xa_ref, xb_ref, g_ref, wg_ref, wu_ref, wo_ref, o_ref, xbuf_ref, xn_ref, sem, *, n_a, n_b):
    tm = o_ref.shape[0]
    full_a, rem_a = divmod(n_a, tm)
    n_tiles = (n_a + n_b) // tm

    def tile_copies(tile, act):
        @pl.when(tile < full_a)
        def _():
            act(pltpu.make_async_copy(xa_ref.at[pl.ds(tile * tm, tm)], xbuf_ref, sem.at[0]))

        if n_b:
            @pl.when(tile == full_a)
            def _():
                if rem_a:
                    act(pltpu.make_async_copy(xa_ref.at[pl.ds(full_a * tm, rem_a)],
                                              xbuf_ref.at[pl.ds(0, rem_a)], sem.at[0]))
                act(pltpu.make_async_copy(xb_ref, xbuf_ref.at[pl.ds(rem_a, n_b)], sem.at[1]))

    @pl.when(pl.program_id(1) == 0)
    def _():
        i = pl.program_id(0)

        @pl.when(i == 0)
        def _():
            tile_copies(i, lambda cp: cp.start())

        tile_copies(i, lambda cp: cp.wait())
        x = xbuf_ref[...]
        o_ref[...] = x
        xn_ref[...] = _rms(x, g_ref[...]).astype(BF16)

        @pl.when(i + 1 < n_tiles)
        def _():
            tile_copies(i + 1, lambda cp: cp.start())

    xn = xn_ref[...]
    gate = _dot(xn, wg_ref[...].astype(BF16))
    up = _dot(xn, wu_ref[...].astype(BF16))
    act = (0.5 * gate * jax.nn.sigmoid(gate) * up).astype(BF16)
    o_ref[...] += _dot(act, wo_ref[...].astype(BF16))


def _ffn(xa, xb, norm_g, w_in, w_out, li):
    n_a, d = xa.shape
    n_b = 0 if xb is None else xb.shape[0]
    t = n_a + n_b
    assert t % TM_STREAM == 0 and (n_b == 0 or n_a % TM_STREAM + n_b == TM_STREAM)
    d_ff = w_out.shape[1]
    nf = d_ff // TF
    return pl.pallas_call(
        functools.partial(_ffn_kernel, n_a=n_a, n_b=n_b),
        grid=(t // TM_STREAM, nf),
        in_specs=[
            pl.BlockSpec(memory_space=pl.ANY),
            pl.BlockSpec(memory_space=pl.ANY),
            pl.BlockSpec((None, 1, d), lambda i, j: (li, 0, 0)),
            pl.BlockSpec((None, d, TF), lambda i, j: (li, 0, j)),
            pl.BlockSpec((None, d, TF), lambda i, j: (li, 0, j + nf)),
            pl.BlockSpec((None, TF, d), lambda i, j: (li, j, 0)),
        ],
        out_specs=pl.BlockSpec((TM_STREAM, d), lambda i, j: (i, 0)),
        out_shape=jax.ShapeDtypeStruct((t, d), F32),
        scratch_shapes=[pltpu.VMEM((TM_STREAM, d), F32), pltpu.VMEM((TM_STREAM, d), BF16),
                        pltpu.SemaphoreType.DMA((2,))],
        compiler_params=_cparams(("arbitrary", "arbitrary")),
        name=f"ffn_l{li}",
    )(xa, xa if xb is None else xb, norm_g, w_in, w_in, w_out)


def _ple_kernel(h_ref, pa_ref, pb_ref, g_ref, wgate_ref, wproj_ref, fn_ref, *o_refs, final, na):
    i = pl.program_id(0)
    h = h_ref[...]
    hn = _rms(h, g_ref[...]).astype(BF16)
    gate = jax.nn.sigmoid(_dot(hn, wgate_ref[...]))
    p = jnp.where(i < na, pa_ref[...], pb_ref[...])
    proj = _dot(p.astype(BF16), wproj_ref[...])
    out = h + proj * gate
    if not final:
        o_refs[0][...] = out
        return
    out = _rms(out, fn_ref[...])

    @pl.when(i < na)
    def _():
        o_refs[0][...] = out

    @pl.when(i >= na)
    def _():
        o_refs[1][...] = out


def _ple(h, pa, pb, norm_g, w_gate, w_proj, final_g, li, final):
    t, d = h.shape
    dp = pa.shape[-1]
    na = pa.shape[1] // TM
    in_a = lambda i: jnp.minimum(i, na - 1)
    in_b = lambda i: jnp.maximum(i - na, 0)
    if final:
        out_specs = [pl.BlockSpec((TM, d), lambda i: (in_a(i), 0)), pl.BlockSpec((TM, d), lambda i: (in_b(i), 0))]
        out_shape = [jax.ShapeDtypeStruct((pa.shape[1], d), F32), jax.ShapeDtypeStruct((pb.shape[1], d), F32)]
    else:
        out_specs = pl.BlockSpec((TM, d), lambda i: (i, 0))
        out_shape = jax.ShapeDtypeStruct((t, d), F32)
    return pl.pallas_call(
        functools.partial(_ple_kernel, final=final, na=na),
        grid=(t // TM,),
        in_specs=[
            pl.BlockSpec((TM, d), lambda i: (i, 0)),
            pl.BlockSpec((None, TM, dp), lambda i: (li, in_a(i), 0)),
            pl.BlockSpec((None, TM, dp), lambda i: (li, in_b(i), 0)),
            _resident((None, 1, d), lambda i: (li, 0, 0)),
            _resident((None, d, d), lambda i: (li, 0, 0)),
            _resident((None, dp, d), lambda i: (li, 0, 0)),
            _resident((1, d), lambda i: (0, 0)),
        ],
        out_specs=out_specs,
        out_shape=out_shape,
        compiler_params=_cparams(("arbitrary",)),
        name=f"ple_l{li}",
    )(h, pa, pb, norm_g, w_gate, w_proj, final_g)


def _gelu(z):
    return 0.5 * z * (1.0 + lax.erf(z * (2.0 ** -0.5)))


def _gmlp_proj_kernel(x_ref, g_ref, w_ref, lng_ref, lnb_ref, u_ref, v_ref, xn_ref):
    j = pl.program_id(1)

    @pl.when(j == 0)
    def _():
        xn_ref[...] = _rms(x_ref[...], g_ref[...]).astype(BF16)
        u_ref[...] = _gelu(_dot(xn_ref[...], w_ref[...])).astype(BF16)

    @pl.when(j == 1)
    def _():
        z = _gelu(_dot(xn_ref[...], w_ref[...]))
        mu = jnp.mean(z, axis=-1, keepdims=True)
        zc = z - mu
        var = jnp.mean(zc * zc, axis=-1, keepdims=True)
        v_ref[...] = zc * lax.rsqrt(var + EPS) * lng_ref[...] + lnb_ref[...]


def _gmlp_proj(h, norm_g, w_in, ln_g, ln_b, li, j_layer):
    t, d = h.shape
    d_a = w_in.shape[-1] // 2
    o_spec = pl.BlockSpec((TM, d_a), lambda i, j: (i, 0))
    return pl.pallas_call(
        _gmlp_proj_kernel,
        grid=(t // TM, 2),
        in_specs=[
            pl.BlockSpec((TM, d), lambda i, j: (i, 0)),
            pl.BlockSpec((None, 1, d), lambda i, j: (li, 0, 0)),
            pl.BlockSpec((None, d, d_a), lambda i, j: (j_layer, 0, j)),
            pl.BlockSpec((None, 1, d_a), lambda i, j: (j_layer, 0, 0)),
            pl.BlockSpec((None, 1, d_a), lambda i, j: (j_layer, 0, 0)),
        ],
        out_specs=[o_spec, o_spec],
        out_shape=[jax.ShapeDtypeStruct((t, d_a), BF16), jax.ShapeDtypeStruct((t, d_a), F32)],
        scratch_shapes=[pltpu.VMEM((TM, d), BF16)],
        compiler_params=_cparams(("parallel", "arbitrary")),
        name="gmlp_proj",
    )(h, norm_g, w_in, ln_g, ln_b)


def _gmlp_mix_kernel(h_ref, u_ref, v_ref, wmix_ref, bias_ref, wo_ref, o_ref, t_ref, *, groups):
    for c in range(TM // CHUNK_A):
        rows = slice(c * CHUNK_A, (c + 1) * CHUNK_A)
        for g in range(groups):
            cols = slice(g * LANES, (g + 1) * LANES)
            s = _dot(wmix_ref[g], v_ref[rows, cols].astype(BF16)) + bias_ref[:, cols]
            t_ref[rows, cols] = (u_ref[rows, cols].astype(F32) * s).astype(BF16)
    o_ref[...] = h_ref[...] + _dot(t_ref[...], wo_ref[...])


def _gmlp_mix(h, u, v, wmix, bias, w_out, j_layer, n_prompt_tiles):
    t, d = h.shape
    d_a = u.shape[1]
    groups = d_a // LANES
    return pl.pallas_call(
        functools.partial(_gmlp_mix_kernel, groups=groups),
        grid=(t // TM,),
        in_specs=[
            pl.BlockSpec((TM, d), lambda i: (i, 0)),
            pl.BlockSpec((TM, d_a), lambda i: (i, 0)),
            pl.BlockSpec((TM, d_a), lambda i: (i, 0)),
            pl.BlockSpec((None, groups, CHUNK_A, CHUNK_A), lambda i: (i // n_prompt_tiles, 0, 0, 0)),
            pl.BlockSpec((None, CHUNK_A, d_a), lambda i: (i // n_prompt_tiles, 0, 0)),
            _resident((None, d_a, d), lambda i: (j_layer, 0, 0)),
        ],
        out_specs=pl.BlockSpec((TM, d), lambda i: (i, 0)),
        out_shape=jax.ShapeDtypeStruct((t, d), F32),
        scratch_shapes=[pltpu.VMEM((TM, d_a), BF16)],
        compiler_params=_cparams(("parallel",)),
        name="gmlp_mix",
    )(h, u, v, wmix, bias, w_out)


def _hgrn_proj_kernel(x_hbm_ref, g_ref, wq_ref, wf_ref, wi_ref, wg_ref, lbl_ref,
                      q_ref, f_ref, i_ref, gt_ref, xbuf_ref, xn_ref, sem, *, li, n_tiles):
    tm = xn_ref.shape[0]

    def row_copy(tile):
        return pltpu.make_async_copy(x_hbm_ref.at[pl.ds(tile * tm, tm)], xbuf_ref, sem.at[0])

    @pl.when(pl.program_id(1) == 0)
    def _():
        i = pl.program_id(0)

        @pl.when(i == 0)
        def _():
            row_copy(i).start()

        row_copy(i).wait()
        xn_ref[...] = _rms(xbuf_ref[...], g_ref[...]).astype(BF16)

        @pl.when(i + 1 < n_tiles)
        def _():
            row_copy(i + 1).start()

    xn = xn_ref[...]
    zq = _dot(xn, wq_ref[...].astype(BF16))
    q_ref[...] = (zq * jax.nn.sigmoid(zq)).astype(BF16)

    logits = lbl_ref[...]
    ex = jnp.exp(logits - jnp.max(logits, axis=0, keepdims=True))
    sm = ex / jnp.sum(ex, axis=0, keepdims=True)
    lb = jnp.sum(sm[: li + 1], axis=0, keepdims=True) - sm[0:1]
    zf = _dot(xn, wf_ref[...].astype(BF16))
    f_ref[...] = lb + (1.0 - lb) * jax.nn.sigmoid(zf)

    i_ref[...] = _dot(xn, wi_ref[...].astype(BF16)).astype(BF16)

    zg = _dot(xn, wg_ref[...].astype(BF16))
    gt_ref[...] = (zg * jax.nn.sigmoid(zg)).astype(BF16)


def _hgrn_proj(h, norm_g, w_in, lb_logits, li, j_layer):
    t, d = h.shape
    d_b = w_in.shape[-1] // 4
    per_part = d_b // TNQ
    w_spec = lambda p: pl.BlockSpec((None, d, TNQ), lambda i, j, p=p: (j_layer, 0, p * per_part + j))
    o_spec = pl.BlockSpec((TM_STREAM, TNQ), lambda i, j: (i, j))
    bf = jax.ShapeDtypeStruct((t, d_b), BF16)
    return pl.pallas_call(
        functools.partial(_hgrn_proj_kernel, li=li, n_tiles=t // TM_STREAM),
        grid=(t // TM_STREAM, per_part),
        in_specs=[
            pl.BlockSpec(memory_space=pl.ANY),
            pl.BlockSpec((None, 1, d), lambda i, j: (li, 0, 0)),
            w_spec(0), w_spec(1), w_spec(2), w_spec(3),
            pl.BlockSpec((lb_logits.shape[0], TNQ), lambda i, j: (0, j)),
        ],
        out_specs=[o_spec, o_spec, o_spec, o_spec],
        out_shape=[bf, jax.ShapeDtypeStruct((t, d_b), F32), bf, bf],
        scratch_shapes=[pltpu.VMEM((TM_STREAM, d), F32), pltpu.VMEM((TM_STREAM, d), BF16),
                        pltpu.SemaphoreType.DMA((1,))],
        compiler_params=_cparams(("arbitrary", "arbitrary")),
        name="hgrn_proj",
    )(h, norm_g, w_in, w_in, w_in, w_in, lb_logits)


def _hgrn_halves():
    halves, half = [], HGRN_BLOCK // 2
    while half >= 1:
        halves.append(half)
        half //= 2
    return tuple(halves)


def _hgrn_norm_gate(o, ng, gate):
    ms = jnp.mean(o * o, axis=-1, keepdims=True)
    return (o * lax.rsqrt(ms + EPS) * ng * gate).astype(BF16)


def _hgrn_prompt_kernel(f_ref, q_ref, i_ref, g_ref, ng_ref, og_ref, sfin_ref,
                        st_ref, tril_ref, sgn_ref, mask_ref, *, nblk, hps):
    c = HGRN_BLOCK
    w = hps * LANES
    halves = _hgrn_halves()
    nl = len(halves)

    @pl.when((pl.program_id(0) == 0) & (pl.program_id(1) == 0))
    def _():
        r = lax.broadcasted_iota(jnp.int32, (c, c), 0)
        s = lax.broadcasted_iota(jnp.int32, (c, c), 1)
        tril_ref[...] = jnp.where(s <= r, 1.0, 0.0).astype(BF16)
        for lvl, half in enumerate(halves):
            par = 2 * half
            second = (r & half) != 0
            sgn_ref[lvl] = jnp.where(second, 1.0, -1.0)
            valid = second & ((s & half) == 0) & ((r // par) == (s // par))
            mask_ref[lvl] = jnp.where(valid, 1.0, 0.0).astype(BF16)
        mask_ref[nl] = jnp.where(r == s, 1.0, 0.0).astype(BF16)

    st_ref[...] = jnp.zeros_like(st_ref)
    low_sub = lax.broadcasted_iota(jnp.int32, (c // SUBLANES, SUBLANES, w), 1) < SUBLANES // 2

    def body(blk, carry):
        rows = pl.ds(pl.multiple_of(blk * c, c), c)
        f = f_ref[rows, :]
        qb = q_ref[rows, :]
        vb = i_ref[rows, :]
        q = qb.astype(F32)
        k = 1.0 - f
        kb = k.astype(BF16)

        x = jnp.log(f)
        hi = x.astype(BF16)
        r1 = x - hi.astype(F32)
        mid = r1.astype(BF16)
        lo = (r1 - mid.astype(F32)).astype(BF16)
        tril = tril_ref[...]
        b = _dot(tril, hi) + _dot(tril, mid) + _dot(tril, lo)
        b3 = b.reshape(c // SUBLANES, SUBLANES, w)

        att = [None] * hps

        def add_level(lvl, qt, kt):
            for hd in range(hps):
                cols = slice(hd * LANES, (hd + 1) * LANES)
                a = _dot_nt(qt[:, cols], kt[:, cols]).astype(BF16) * mask_ref[lvl]
                att[hd] = a if att[hd] is None else att[hd] + a

        for lvl, half in enumerate(halves):
            par = 2 * half
            if half == 1:
                add_level(lvl, (q * f).astype(BF16), kb)
                continue
            if half >= SUBLANES:
                pieces = [jnp.broadcast_to(b[p * par + half - 1:p * par + half, :], (par, w))
                          for p in range(c // par)]
                beta = pieces[0] if len(pieces) == 1 else jnp.concatenate(pieces, axis=0)
            elif half == SUBLANES // 2:
                beta = jnp.broadcast_to(b3[:, half - 1:half, :], b3.shape).reshape(c, w)
            else:
                beta = jnp.where(low_sub, b3[:, half - 1:half, :],
                                 b3[:, par + half - 1:par + half, :]).reshape(c, w)
            sgn = jnp.concatenate([sgn_ref[lvl]] * hps, axis=1)
            e = jnp.exp((b - beta) * sgn).astype(BF16)
            add_level(lvl, qb * e, kb * e)
        add_level(nl, qb, kb)

        qe = (q * jnp.exp(b)).astype(BF16)
        b_last = b[c - 1:c, :]
        kd = (k * jnp.exp(b_last - b)).astype(BF16)
        e_last = jnp.exp(b_last)
        gate = g_ref[rows, :].astype(F32)
        ng = ng_ref[...]
        for hd in range(hps):
            cols = slice(hd * LANES, (hd + 1) * LANES)
            st = st_ref[hd]
            o = _dot_nt(qe[:, cols], st.astype(BF16)) + _dot(att[hd], vb[:, cols])
            og_ref[rows, cols] = _hgrn_norm_gate(o, ng[:, cols], gate[:, cols])
            st_ref[hd] = st * e_last[:, cols] + _dot_tn(vb[:, cols], kd[:, cols])
        return carry

    lax.fori_loop(0, nblk, body, 0, unroll=4)
    for hd in range(hps):
        sfin_ref[hd] = st_ref[hd].T


def _hgrn_prompt(f, q, i, g, norm_g, batch, seq, heads):
    hps = HGRN_HEADS_PER_STEP
    w = hps * LANES
    ngrp = heads // hps
    nblk = seq // HGRN_BLOCK
    nl = len(_hgrn_halves())
    part = pl.BlockSpec((seq, w), lambda s, h: (s, h))
    return pl.pallas_call(
        functools.partial(_hgrn_prompt_kernel, nblk=nblk, hps=hps),
        grid=(batch, ngrp),
        in_specs=[part, part, part, part,
                  pl.BlockSpec((1, w), lambda s, h: (0, h))],
        out_specs=[pl.BlockSpec((seq, w), lambda s, h: (s, h)),
                   pl.BlockSpec((None, hps, LANES, LANES), lambda s, h: (s, h, 0, 0))],
        out_shape=[jax.ShapeDtypeStruct((batch * seq, heads * LANES), BF16),
                   jax.ShapeDtypeStruct((batch, heads, LANES, LANES), F32)],
        scratch_shapes=[pltpu.VMEM((hps, LANES, LANES), F32),
                        pltpu.VMEM((HGRN_BLOCK, HGRN_BLOCK), BF16),
                        pltpu.VMEM((nl, HGRN_BLOCK, LANES), F32),
                        pltpu.VMEM((nl + 1, HGRN_BLOCK, HGRN_BLOCK), BF16)],
        compiler_params=_cparams(("arbitrary", "arbitrary")),
        name="hgrn_prompt",
    )(f, q, i, g, norm_g)


def _hgrn_sample_kernel(f_ref, q_ref, i_ref, g_ref, ng_ref, s0_ref, og_ref, s1_ref, *, heads, dec_seq):
    rows_per_tile = SUBLANES
    per_tile = rows_per_tile // dec_seq
    row = lax.broadcasted_iota(jnp.int32, (rows_per_tile, LANES), 0)
    pos = row % dec_seq
    first = row < dec_seq
    zpad = jnp.zeros((rows_per_tile, LANES), F32)
    zpad2 = jnp.zeros((rows_per_tile, 2 * LANES), F32)
    prow = lax.broadcasted_iota(jnp.int32, (4 * rows_per_tile, 2 * LANES), 0)
    plane = lax.broadcasted_iota(jnp.int32, (4 * rows_per_tile, 2 * LANES), 1)
    last_row = jnp.where(plane < LANES, dec_seq - 1, 2 * dec_seq - 1)
    sel_ab = jnp.where((prow < 3 * rows_per_tile) & (prow % rows_per_tile == last_row), 1.0, 0.0).astype(BF16)
    f_all = f_ref[...]
    q_all = q_ref[...].astype(F32)
    v_all = i_ref[...].astype(F32)
    g_all = g_ref[...].astype(F32)
    ng = ng_ref[...]

    def pick(x, s):
        return jnp.where(first, x[s:s + 1, :], x[dec_seq + s:dec_seq + s + 1, :])

    for tile in range(SAMPLE_BATCH_TILE // per_tile):
        rows = slice(tile * rows_per_tile, (tile + 1) * rows_per_tile)
        for h in range(heads):
            cols = slice(h * LANES, (h + 1) * LANES)
            q = q_all[rows, cols]
            f = f_all[rows, cols]
            v = v_all[rows, cols]
            k = 1.0 - f
            b = jnp.log(f)
            shift = 1
            while shift < dec_seq:
                b = b + jnp.where(pos >= shift, pltpu.roll(b, shift, 0), 0.0)
                shift *= 2
            eb = jnp.exp(b)
            qe = (q * eb).astype(BF16)
            s_a = s0_ref[tile * per_tile, h]
            s_b = s0_ref[tile * per_tile + 1, h]
            o_ab = _dot(qe, jnp.concatenate([s_a, s_b], axis=1).astype(BF16))
            o = jnp.where(first, o_ab[:, :LANES], o_ab[:, LANES:])
            for s in range(dec_seq):
                e = jnp.exp(jnp.minimum(b - pick(b, s), 0.0))
                wgt = jnp.sum(q * e * pick(k, s), axis=-1, keepdims=True)
                o = o + jnp.where(pos >= s, wgt, 0.0) * pick(v, s)
            og_ref[rows, cols] = _hgrn_norm_gate(o, ng[:, cols], g_all[rows, cols])

            b_last = pick(b, dec_seq - 1)
            kd = jnp.concatenate([k * jnp.exp(b_last - b), zpad], axis=0).astype(BF16)
            v_ab = jnp.concatenate([jnp.where(first, v, 0.0), jnp.where(first, 0.0, v)], axis=1)
            v_ab = jnp.concatenate([v_ab, zpad2], axis=0).astype(BF16)
            hi = eb.astype(BF16).astype(F32)
            mid = (eb - hi).astype(BF16).astype(F32)
            lo = eb - hi - mid
            pieces = jnp.concatenate([hi, mid, lo, zpad], axis=0).astype(BF16)
            e_ab = _dot_tn(pieces, sel_ab)
            upd_ab = _dot_tn(kd, v_ab)
            s1_ref[tile * per_tile, h] = s_a * e_ab[:, :LANES] + upd_ab[:, :LANES]
            s1_ref[tile * per_tile + 1, h] = s_b * e_ab[:, LANES:] + upd_ab[:, LANES:]


def _hgrn_sample(f, q, i, g, norm_g, s0, n_prompt, dec_batch, dec_seq, heads):
    rows = SAMPLE_BATCH_TILE * dec_seq
    first_blk = n_prompt // rows
    d_b = heads * LANES
    part = pl.BlockSpec((rows, d_b), lambda i: (first_blk + i, 0))
    return pl.pallas_call(
        functools.partial(_hgrn_sample_kernel, heads=heads, dec_seq=dec_seq),
        grid=(dec_batch // SAMPLE_BATCH_TILE,),
        in_specs=[part, part, part, part,
                  _resident((1, d_b), lambda i: (0, 0)),
                  pl.BlockSpec((SAMPLE_BATCH_TILE, heads, LANES, LANES), lambda i: (i, 0, 0, 0))],
        out_specs=[pl.BlockSpec((rows, d_b), lambda i: (i, 0)),
                   pl.BlockSpec((SAMPLE_BATCH_TILE, heads, LANES, LANES), lambda i: (i, 0, 0, 0))],
        out_shape=[jax.ShapeDtypeStruct((dec_batch * dec_seq, d_b), BF16),
                   jax.ShapeDtypeStruct(s0.shape, F32)],
        compiler_params=_cparams(("parallel",)),
        name="hgrn_sample",
    )(f, q, i, g, norm_g, s0)


def _out_proj_kernel(h_ref, xa_ref, xb_ref, w_ref, o_ref, *, na):
    x = jnp.where(pl.program_id(0) < na, xa_ref[...], xb_ref[...])
    o_ref[...] = h_ref[...] + _dot(x, w_ref[...])


def _out_proj(h, xa, xb, w, j_layer):
    t, d = h.shape
    kdim = xa.shape[1]
    na = xa.shape[0] // TM
    return pl.pallas_call(
        functools.partial(_out_proj_kernel, na=na),
        grid=(t // TM,),
        in_specs=[
            pl.BlockSpec((TM, d), lambda i: (i, 0)),
            pl.BlockSpec((TM, kdim), lambda i: (jnp.minimum(i, na - 1), 0)),
            pl.BlockSpec((TM, kdim), lambda i: (jnp.maximum(i - na, 0), 0)),
            _resident((None, kdim, d), lambda i: (j_layer, 0, 0)),
        ],
        out_specs=pl.BlockSpec((TM, d), lambda i: (i, 0)),
        out_shape=jax.ShapeDtypeStruct((t, d), F32),
        compiler_params=_cparams(("parallel",)),
        name="hgrn_out_proj",
    )(h, xa, xb, w)


def kernel(x_prompt, x_sample, state_hgrn, p_prompt, p_sample, ffn1_w_in, ffn1_w_out, ffn2_w_in, ffn2_w_out, norm_ffn1, norm_mix, norm_ffn2, norm_ple, a_w_in, a_ln_g, a_ln_b, a_w_s, a_b_s, a_w_out, b_w_in, b_lb_logits, b_norm_g, b_w_out, ple_w_proj, ple_w_gate, final_norm):
    batch, seq, d = x_prompt.shape
    dec_batch, dec_seq, _ = x_sample.shape
    depth = ffn1_w_in.shape[0]
    n_prompt = batch * seq
    n_sample = dec_batch * dec_seq
    t = n_prompt + n_sample
    heads = b_norm_g.shape[-1] // LANES
    assert n_prompt % TM == 0 and n_sample % TM == 0 and t % TM_STREAM == 0
    assert seq % HGRN_BLOCK == 0 and heads % HGRN_HEADS_PER_STEP == 0
    assert CHUNK_A % dec_seq == 0 and SUBLANES % dec_seq == 0 and dec_batch % SAMPLE_BATCH_TILE == 0
    assert a_w_s.shape[-1] == CHUNK_A and seq % CHUNK_A == 0

    p_a = p_prompt.reshape(depth, n_prompt, -1)
    p_b = p_sample.reshape(depth, n_sample, -1)

    bf = lambda w: w.astype(BF16)
    vec = lambda g: g.reshape(g.shape[0], 1, g.shape[-1])
    a_w_in_b, a_w_out_b, b_w_out_b = bf(a_w_in), bf(a_w_out), bf(b_w_out)
    ple_w_proj_b, ple_w_gate_b = bf(ple_w_proj), bf(ple_w_gate)
    n_ffn1, n_mix, n_ffn2, n_ple = vec(norm_ffn1), vec(norm_mix), vec(norm_ffn2), vec(norm_ple)
    ln_g, ln_b = vec(a_ln_g), vec(a_ln_b)
    final_g = final_norm.reshape(1, d)

    reps = CHUNK_A // dec_seq
    tri = jnp.tril(jnp.ones((CHUNK_A, CHUNK_A), bool))
    pos_c = jnp.arange(CHUNK_A) // dec_seq
    blockdiag = pos_c[:, None] == pos_c[None, :]
    w_prompt = jnp.where(tri, a_w_s, 0.0)
    w_sample = jnp.where(tri & blockdiag, jnp.tile(a_w_s[:, :, :dec_seq, :dec_seq], (1, 1, reps, reps)), 0.0)
    wmix = jnp.stack([w_prompt, w_sample], axis=1).astype(BF16)
    b_prompt = jnp.swapaxes(a_b_s, 1, 2)
    b_sample = jnp.tile(b_prompt[:, :dec_seq], (1, reps, 1))
    bias = jnp.repeat(jnp.stack([b_prompt, b_sample], axis=1), LANES, axis=-1)

    new_v, s_prompt, s_sample = [], [], []
    h = None
    for li in range(depth):
        if li == 0:
            h = _ffn(x_prompt.reshape(n_prompt, d), x_sample.reshape(n_sample, d), n_ffn1, ffn1_w_in, ffn1_w_out, li)
        else:
            h = _ffn(h, None, n_ffn1, ffn1_w_in, ffn1_w_out, li)
        j = li // 2
        if li % 2 == 0:
            u, v = _gmlp_proj(h, n_mix, a_w_in_b, ln_g, ln_b, li, j)
            new_v.append(v[n_prompt:].reshape(dec_batch, dec_seq, -1))
            h = _gmlp_mix(h, u, v, wmix[j], bias[j], a_w_out_b, j, n_prompt // TM)
        else:
            q, f, iv, g = _hgrn_proj(h, n_mix, b_w_in, b_lb_logits, li, j)
            ng = b_norm_g[j].reshape(1, heads * LANES)
            og_p, s_fin = _hgrn_prompt(f, q, iv, g, ng, batch, seq, heads)
            og_s, s_new = _hgrn_sample(f, q, iv, g, ng, state_hgrn[j], n_prompt, dec_batch, dec_seq, heads)
            s_prompt.append(s_fin)
            s_sample.append(s_new)
            h = _out_proj(h, og_p, og_s, b_w_out_b, j)
        h = _ffn(h, None, n_ffn2, ffn2_w_in, ffn2_w_out, li)
        h = _ple(h, p_a, p_b, n_ple, ple_w_gate_b, ple_w_proj_b, final_g, li, final=(li == depth - 1))

    y_prompt = h[0].reshape(batch, seq, d)
    y_sample = h[1].reshape(dec_batch, dec_seq, d)
    return (y_prompt, y_sample, jnp.stack(s_prompt), jnp.stack(s_sample), jnp.stack(new_v))
```

```python
import functools

import jax
import jax.numpy as jnp
from jax import lax
from jax.experimental import pallas as pl
from jax.experimental.pallas import tpu as pltpu

F32 = jnp.float32
BF16 = jnp.bfloat16
EPS = 1e-6

LANES = 128
SUBLANES = 8
VMEM_LIMIT_BYTES = 63 * 1024 * 1024

TM = 512
TM_STREAM = 1088
TF = 512
TNQ = 512
HGRN_BLOCK = 128
HGRN_HEADS_PER_STEP = 8
CHUNK_A = 128
SAMPLE_BATCH_TILE = 4


def _cparams(sem):
    return pltpu.CompilerParams(dimension_semantics=sem, vmem_limit_bytes=VMEM_LIMIT_BYTES)


def _rms(x, g):
    ms = jnp.mean(x * x, axis=-1, keepdims=True)
    return x * lax.rsqrt(ms + EPS) * g


def _resident(shape, index_map):
    return pl.BlockSpec(shape, index_map, pipeline_mode=pl.Buffered(1))


def _dot(a, b):
    return jnp.dot(a, b, preferred_element_type=F32)


def _dot_nt(a, b):
    return lax.dot_general(a, b, (((1,), (1,)), ((), ())), preferred_element_type=F32)


def _dot_tn(a, b):
    return lax.dot_general(a, b, (((0,), (0,)), ((), ())), preferred_element_type=F32)


def _ffn_kernel(xa_ref, xb_ref, g_ref, wg_ref, wu_ref, wo_ref, o_ref, xbuf_ref, xn_ref, sem, *, n_a, n_b):
    tm = o_ref.shape[0]
    full_a, rem_a = divmod(n_a, tm)
    n_tiles = (n_a + n_b) // tm

    def tile_copies(tile, act):
        @pl.when(tile < full_a)
        def _():
            act(pltpu.make_async_copy(xa_ref.at[pl.ds(tile * tm, tm)], xbuf_ref, sem.at[0]))

        if n_b:
            @pl.when(tile == full_a)
            def _():
                if rem_a:
                    act(pltpu.make_async_copy(xa_ref.at[pl.ds(full_a * tm, rem_a)],
                                              xbuf_ref.at[pl.ds(0, rem_a)], sem.at[0]))
                act(pltpu.make_async_copy(xb_ref, xbuf_ref.at[pl.ds(rem_a, n_b)], sem.at[1]))

    @pl.when(pl.program_id(1) == 0)
    def _():
        i = pl.program_id(0)

        @pl.when(i == 0)
        def _():
            tile_copies(i, lambda cp: cp.start())

        tile_copies(i, lambda cp: cp.wait())
        x = xbuf_ref[...]
        o_ref[...] = x
        xn_ref[...] = _rms(x, g_ref[...]).astype(BF16)

        @pl.when(i + 1 < n_tiles)
        def _():
            tile_copies(i + 1, lambda cp: cp.start())

    xn = xn_ref[...]
    gate = _dot(xn, wg_ref[...].astype(BF16))
    up = _dot(xn, wu_ref[...].astype(BF16))
    act = (0.5 * gate * jax.nn.sigmoid(gate) * up).astype(BF16)
    o_ref[...] += _dot(act, wo_ref[...].astype(BF16))


def _ffn(xa, xb, norm_g, w_in, w_out, li):
    n_a, d = xa.shape
    n_b = 0 if xb is None else xb.shape[0]
    t = n_a + n_b
    assert t % TM_STREAM == 0 and (n_b == 0 or n_a % TM_STREAM + n_b == TM_STREAM)
    d_ff = w_out.shape[1]
    nf = d_ff // TF
    return pl.pallas_call(
        functools.partial(_ffn_kernel, n_a=n_a, n_b=n_b),
        grid=(t // TM_STREAM, nf),
        in_specs=[
            pl.BlockSpec(memory_space=pl.ANY),
            pl.BlockSpec(memory_space=pl.ANY),
            pl.BlockSpec((None, 1, d), lambda i, j: (li, 0, 0)),
            pl.BlockSpec((None, d, TF), lambda i, j: (li, 0, j)),
            pl.BlockSpec((None, d, TF), lambda i, j: (li, 0, j + nf)),
            pl.BlockSpec((None, TF, d), lambda i, j: (li, j, 0)),
        ],
        out_specs=pl.BlockSpec((TM_STREAM, d), lambda i, j: (i, 0)),
        out_shape=jax.ShapeDtypeStruct((t, d), F32),
        scratch_shapes=[pltpu.VMEM((TM_STREAM, d), F32), pltpu.VMEM((TM_STREAM, d), BF16),
                        pltpu.SemaphoreType.DMA((2,))],
        compiler_params=_cparams(("arbitrary", "arbitrary")),
        name=f"ffn_l{li}",
    )(xa, xa if xb is None else xb, norm_g, w_in, w_in, w_out)


def _ple_kernel(h_ref, pa_ref, pb_ref, g_ref, wgate_ref, wproj_ref, fn_ref, *o_refs, final, na):
    i = pl.program_id(0)
    h = h_ref[...]
    hn = _rms(h, g_ref[...]).astype(BF16)
    gate = jax.nn.sigmoid(_dot(hn, wgate_ref[...]))
    p = jnp.where(i < na, pa_ref[...], pb_ref[...])
    proj = _dot(p.astype(BF16), wproj_ref[...])
    out = h + proj * gate
    if not final:
        o_refs[0][...] = out
        return
    out = _rms(out, fn_ref[...])

    @pl.when(i < na)
    def _():
        o_refs[0][...] = out

    @pl.when(i >= na)
    def _():
        o_refs[1][...] = out


def _ple(h, pa, pb, norm_g, w_gate, w_proj, final_g, li, final):
    t, d = h.shape
    dp = pa.shape[-1]
    na = pa.shape[1] // TM
    in_a = lambda i: jnp.minimum(i, na - 1)
    in_b = lambda i: jnp.maximum(i - na, 0)
    if final:
        out_specs = [pl.BlockSpec((TM, d), lambda i: (in_a(i), 0)), pl.BlockSpec((TM, d), lambda i: (in_b(i), 0))]
        out_shape = [jax.ShapeDtypeStruct((pa.shape[1], d), F32), jax.ShapeDtypeStruct((pb.shape[1], d), F32)]
    else:
        out_specs = pl.BlockSpec((TM, d), lambda i: (i, 0))
        out_shape = jax.ShapeDtypeStruct((t, d), F32)
    return pl.pallas_call(
        functools.partial(_ple_kernel, final=final, na=na),
        grid=(t // TM,),
        in_specs=[
            pl.BlockSpec((TM, d), lambda i: (i, 0)),
            pl.BlockSpec((None, TM, dp), lambda i: (li, in_a(i), 0)),
            pl.BlockSpec((None, TM, dp), lambda i: (li, in_b(i), 0)),
            _resident((None, 1, d), lambda i: (li, 0, 0)),
            _resident((None, d, d), lambda i: (li, 0, 0)),
            _resident((None, dp, d), lambda i: (li, 0, 0)),
            _resident((1, d), lambda i: (0, 0)),
        ],
        out_specs=out_specs,
        out_shape=out_shape,
        compiler_params=_cparams(("arbitrary",)),
        name=f"ple_l{li}",
    )(h, pa, pb, norm_g, w_gate, w_proj, final_g)


def _gelu(z):
    return 0.5 * z * (1.0 + lax.erf(z * (2.0 ** -0.5)))


def _gmlp_proj_kernel(x_ref, g_ref, w_ref, lng_ref, lnb_ref, u_ref, v_ref, xn_ref):
    j = pl.program_id(1)

    @pl.when(j == 0)
    def _():
        xn_ref[...] = _rms(x_ref[...], g_ref[...]).astype(BF16)
        u_ref[...] = _gelu(_dot(xn_ref[...], w_ref[...])).astype(BF16)

    @pl.when(j == 1)
    def _():
        half = xn_ref.shape[0] // 2
        for part in range(2):
            rows = slice(part * half, (part + 1) * half)
            z = _gelu(_dot(xn_ref[rows, :], w_ref[...]))
            mu = jnp.mean(z, axis=-1, keepdims=True)
            zc = z - mu
            var = jnp.mean(zc * zc, axis=-1, keepdims=True)
            v_ref[rows, :] = zc * lax.rsqrt(var + EPS) * lng_ref[...] + lnb_ref[...]


def _gmlp_proj(h, norm_g, w_in, ln_g, ln_b, li, j_layer):
    t, d = h.shape
    d_a = w_in.shape[-1] // 2
    o_spec = pl.BlockSpec((TM, d_a), lambda i, j: (i, 0))
    return pl.pallas_call(
        _gmlp_proj_kernel,
        grid=(t // TM, 2),
        in_specs=[
            pl.BlockSpec((TM, d), lambda i, j: (i, 0)),
            pl.BlockSpec((None, 1, d), lambda i, j: (li, 0, 0)),
            pl.BlockSpec((None, d, d_a), lambda i, j: (j_layer, 0, j)),
            pl.BlockSpec((None, 1, d_a), lambda i, j: (j_layer, 0, 0)),
            pl.BlockSpec((None, 1, d_a), lambda i, j: (j_layer, 0, 0)),
        ],
        out_specs=[o_spec, o_spec],
        out_shape=[jax.ShapeDtypeStruct((t, d_a), BF16), jax.ShapeDtypeStruct((t, d_a), F32)],
        scratch_shapes=[pltpu.VMEM((TM, d), BF16)],
        compiler_params=_cparams(("parallel", "arbitrary")),
        name="gmlp_proj",
    )(h, norm_g, w_in, ln_g, ln_b)


def _gmlp_mix_kernel(h_ref, u_ref, v_ref, wmix_ref, bias_ref, wo_ref, o_ref, t_ref, *, groups):
    for c in range(TM // CHUNK_A):
        rows = slice(c * CHUNK_A, (c + 1) * CHUNK_A)
        for g in range(groups):
            cols = slice(g * LANES, (g + 1) * LANES)
            s = _dot(wmix_ref[g], v_ref[rows, cols].astype(BF16)) + bias_ref[:, cols]
            t_ref[rows, cols] = (u_ref[rows, cols].astype(F32) * s).astype(BF16)
    o_ref[...] = h_ref[...] + _dot(t_ref[...], wo_ref[...])


def _gmlp_mix(h, u, v, wmix, bias, w_out, j_layer, n_prompt_tiles):
    t, d = h.shape
    d_a = u.shape[1]
    groups = d_a // LANES
    return pl.pallas_call(
        functools.partial(_gmlp_mix_kernel, groups=groups),
        grid=(t // TM,),
        in_specs=[
            pl.BlockSpec((TM, d), lambda i: (i, 0)),
            pl.BlockSpec((TM, d_a), lambda i: (i, 0)),
            pl.BlockSpec((TM, d_a), lambda i: (i, 0)),
            pl.BlockSpec((None, groups, CHUNK_A, CHUNK_A), lambda i: (i // n_prompt_tiles, 0, 0, 0)),
            pl.BlockSpec((None, CHUNK_A, d_a), lambda i: (i // n_prompt_tiles, 0, 0)),
            _resident((None, d_a, d), lambda i: (j_layer, 0, 0)),
        ],
        out_specs=pl.BlockSpec((TM, d), lambda i: (i, 0)),
        out_shape=jax.ShapeDtypeStruct((t, d), F32),
        scratch_shapes=[pltpu.VMEM((TM, d_a), BF16)],
        compiler_params=_cparams(("parallel",)),
        name="gmlp_mix",
    )(h, u, v, wmix, bias, w_out)


def _hgrn_proj_kernel(x_hbm_ref, g_ref, wq_ref, wf_ref, wi_ref, wg_ref, lbl_ref,
                      q_ref, f_ref, i_ref, gt_ref, xbuf_ref, xn_ref, sem, *, li, n_tiles):
    tm = xn_ref.shape[0]

    def row_copy(tile):
        return pltpu.make_async_copy(x_hbm_ref.at[pl.ds(tile * tm, tm)], xbuf_ref, sem.at[0])

    @pl.when(pl.program_id(1) == 0)
    def _():
        i = pl.program_id(0)

        @pl.when(i == 0)
        def _():
            row_copy(i).start()

        row_copy(i).wait()
        xn_ref[...] = _rms(xbuf_ref[...], g_ref[...]).astype(BF16)

        @pl.when(i + 1 < n_tiles)
        def _():
            row_copy(i + 1).start()

    xn = xn_ref[...]
    zq = _dot(xn, wq_ref[...].astype(BF16))
    q_ref[...] = (zq * jax.nn.sigmoid(zq)).astype(BF16)

    logits = lbl_ref[...]
    ex = jnp.exp(logits - jnp.max(logits, axis=0, keepdims=True))
    sm = ex / jnp.sum(ex, axis=0, keepdims=True)
    lb = jnp.sum(sm[: li + 1], axis=0, keepdims=True) - sm[0:1]
    zf = _dot(xn, wf_ref[...].astype(BF16))
    f_ref[...] = lb + (1.0 - lb) * jax.nn.sigmoid(zf)

    zg = _dot(xn, wg_ref[...].astype(BF16))
    gt_ref[...] = (zg * jax.nn.sigmoid(zg)).astype(BF16)

    i_ref[...] = _dot(xn, wi_ref[...].astype(BF16)).astype(BF16)


def _hgrn_proj(h, norm_g, w_in, lb_logits, li, j_layer):
    t, d = h.shape
    d_b = w_in.shape[-1] // 4
    per_part = d_b // TNQ
    w_spec = lambda p: pl.BlockSpec((None, d, TNQ), lambda i, j, p=p: (j_layer, 0, p * per_part + j))
    o_spec = pl.BlockSpec((TM_STREAM, TNQ), lambda i, j: (i, j))
    bf = jax.ShapeDtypeStruct((t, d_b), BF16)
    return pl.pallas_call(
        functools.partial(_hgrn_proj_kernel, li=li, n_tiles=t // TM_STREAM),
        grid=(t // TM_STREAM, per_part),
        in_specs=[
            pl.BlockSpec(memory_space=pl.ANY),
            pl.BlockSpec((None, 1, d), lambda i, j: (li, 0, 0)),
            w_spec(0), w_spec(1), w_spec(2), w_spec(3),
            pl.BlockSpec((lb_logits.shape[0], TNQ), lambda i, j: (0, j)),
        ],
        out_specs=[o_spec, o_spec, o_spec, o_spec],
        out_shape=[bf, jax.ShapeDtypeStruct((t, d_b), F32), bf, bf],
        scratch_shapes=[pltpu.VMEM((TM_STREAM, d), F32), pltpu.VMEM((TM_STREAM, d), BF16),
                        pltpu.SemaphoreType.DMA((1,))],
        compiler_params=_cparams(("arbitrary", "arbitrary")),
        name="hgrn_proj",
    )(h, norm_g, w_in, w_in, w_in, w_in, lb_logits)


def _hgrn_halves():
    halves, half = [], HGRN_BLOCK // 2
    while half >= 1:
        halves.append(half)
        half //= 2
    return tuple(halves)


def _hgrn_norm_gate(o, ng, gate):
    ms = jnp.mean(o * o, axis=-1, keepdims=True)
    return (o * lax.rsqrt(ms + EPS) * ng * gate).astype(BF16)


def _hgrn_prompt_kernel(f_ref, q_ref, i_ref, g_ref, ng_ref, og_ref, sfin_ref,
                        st_ref, tril_ref, sgn_ref, mask_ref, *, nblk, hps):
    c = HGRN_BLOCK
    w = hps * LANES
    halves = _hgrn_halves()
    nl = len(halves)

    @pl.when((pl.program_id(0) == 0) & (pl.program_id(1) == 0))
    def _():
        r = lax.broadcasted_iota(jnp.int32, (c, c), 0)
        s = lax.broadcasted_iota(jnp.int32, (c, c), 1)
        tril_ref[...] = jnp.where(s <= r, 1.0, 0.0).astype(BF16)
        for lvl, half in enumerate(halves):
            par = 2 * half
            second = (r & half) != 0
            sgn_ref[lvl] = jnp.where(second, 1.0, -1.0)
            valid = second & ((s & half) == 0) & ((r // par) == (s // par))
            mask_ref[lvl] = jnp.where(valid, 1.0, 0.0).astype(BF16)
        mask_ref[nl] = jnp.where(r == s, 1.0, 0.0).astype(BF16)

    st_ref[...] = jnp.zeros_like(st_ref)
    low_sub = lax.broadcasted_iota(jnp.int32, (c // SUBLANES, SUBLANES, w), 1) < SUBLANES // 2

    def body(blk, carry):
        rows = pl.ds(pl.multiple_of(blk * c, c), c)
        f = f_ref[rows, :]
        qb = q_ref[rows, :]
        vb = i_ref[rows, :]
        q = qb.astype(F32)
        k = 1.0 - f
        kb = k.astype(BF16)

        x = jnp.log(f)
        hi = x.astype(BF16)
        r1 = x - hi.astype(F32)
        mid = r1.astype(BF16)
        lo = (r1 - mid.astype(F32)).astype(BF16)
        tril = tril_ref[...]
        b = _dot(tril, hi) + _dot(tril, mid) + _dot(tril, lo)
        b3 = b.reshape(c // SUBLANES, SUBLANES, w)

        att = [None] * hps

        def add_level(lvl, qt, kt):
            for hd in range(hps):
                cols = slice(hd * LANES, (hd + 1) * LANES)
                a = _dot_nt(qt[:, cols], kt[:, cols]).astype(BF16) * mask_ref[lvl]
                att[hd] = a if att[hd] is None else att[hd] + a

        for lvl, half in enumerate(halves):
            par = 2 * half
            if half == 1:
                add_level(lvl, (q * f).astype(BF16), kb)
                continue
            if half >= SUBLANES:
                pieces = [jnp.broadcast_to(b[p * par + half - 1:p * par + half, :], (par, w))
                          for p in range(c // par)]
                beta = pieces[0] if len(pieces) == 1 else jnp.concatenate(pieces, axis=0)
            elif half == SUBLANES // 2:
                beta = jnp.broadcast_to(b3[:, half - 1:half, :], b3.shape).reshape(c, w)
            else:
                beta = jnp.where(low_sub, b3[:, half - 1:half, :],
                                 b3[:, par + half - 1:par + half, :]).reshape(c, w)
            sgn = jnp.concatenate([sgn_ref[lvl]] * hps, axis=1)
            e = jnp.exp((b - beta) * sgn).astype(BF16)
            add_level(lvl, qb * e, kb * e)
        add_level(nl, qb, kb)

        qe = (q * jnp.exp(b)).astype(BF16)
        b_last = b[c - 1:c, :]
        kd = (k * jnp.exp(b_last - b)).astype(BF16)
        e_last = jnp.exp(b_last)
        gate = g_ref[rows, :].astype(F32)
        ng = ng_ref[...]
        for hd in range(hps):
            cols = slice(hd * LANES, (hd + 1) * LANES)
            st = st_ref[hd]
            o = _dot_nt(qe[:, cols], st.astype(BF16)) + _dot(att[hd], vb[:, cols])
            og_ref[rows, cols] = _hgrn_norm_gate(o, ng[:, cols], gate[:, cols])
            st_ref[hd] = st * e_last[:, cols] + _dot_tn(vb[:, cols], kd[:, cols])
        return carry

    lax.fori_loop(0, nblk, body, 0, unroll=4)
    for hd in range(hps):
        sfin_ref[hd] = st_ref[hd].T


def _hgrn_prompt(f, q, i, g, norm_g, batch, seq, heads):
    hps = HGRN_HEADS_PER_STEP
    w = hps * LANES
    ngrp = heads // hps
    nblk = seq // HGRN_BLOCK
    nl = len(_hgrn_halves())
    part = pl.BlockSpec((seq, w), lambda s, h: (s, h))
    return pl.pallas_call(
        functools.partial(_hgrn_prompt_kernel, nblk=nblk, hps=hps),
        grid=(batch, ngrp),
        in_specs=[part, part, part, part,
                  pl.BlockSpec((1, w), lambda s, h: (0, h))],
        out_specs=[pl.BlockSpec((seq, w), lambda s, h: (s, h)),
                   pl.BlockSpec((None, hps, LANES, LANES), lambda s, h: (s, h, 0, 0))],
        out_shape=[jax.ShapeDtypeStruct((batch * seq, heads * LANES), BF16),
                   jax.ShapeDtypeStruct((batch, heads, LANES, LANES), F32)],
        scratch_shapes=[pltpu.VMEM((hps, LANES, LANES), F32),
                        pltpu.VMEM((HGRN_BLOCK, HGRN_BLOCK), BF16),
                        pltpu.VMEM((nl, HGRN_BLOCK, LANES), F32),
                        pltpu.VMEM((nl + 1, HGRN_BLOCK, HGRN_BLOCK), BF16)],
        compiler_params=_cparams(("arbitrary", "arbitrary")),
        name="hgrn_prompt",
    )(f, q, i, g, norm_g)


def _hgrn_sample_kernel(f_ref, q_ref, i_ref, g_ref, ng_ref, s0_ref, og_ref, s1_ref, *, heads, dec_seq):
    rows_per_tile = SUBLANES
    per_tile = rows_per_tile // dec_seq
    row = lax.broadcasted_iota(jnp.int32, (rows_per_tile, LANES), 0)
    pos = row % dec_seq
    first = row < dec_seq
    zpad = jnp.zeros((rows_per_tile, LANES), F32)
    zpad2 = jnp.zeros((rows_per_tile, 2 * LANES), F32)
    prow = lax.broadcasted_iota(jnp.int32, (4 * rows_per_tile, 2 * LANES), 0)
    plane = lax.broadcasted_iota(jnp.int32, (4 * rows_per_tile, 2 * LANES), 1)
    last_row = jnp.where(plane < LANES, dec_seq - 1, 2 * dec_seq - 1)
    sel_ab = jnp.where((prow < 3 * rows_per_tile) & (prow % rows_per_tile == last_row), 1.0, 0.0).astype(BF16)
    f_all = f_ref[...]
    q_all = q_ref[...].astype(F32)
    v_all = i_ref[...].astype(F32)
    g_all = g_ref[...].astype(F32)
    ng = ng_ref[...]

    def pick(x, s):
        return jnp.where(first, x[s:s + 1, :], x[dec_seq + s:dec_seq + s + 1, :])

    for tile in range(SAMPLE_BATCH_TILE // per_tile):
        rows = slice(tile * rows_per_tile, (tile + 1) * rows_per_tile)
        for h in range(heads):
            cols = slice(h * LANES, (h + 1) * LANES)
            q = q_all[rows, cols]
            f = f_all[rows, cols]
            v = v_all[rows, cols]
            k = 1.0 - f
            b = jnp.log(f)
            shift = 1
            while shift < dec_seq:
                b = b + jnp.where(pos >= shift, pltpu.roll(b, shift, 0), 0.0)
                shift *= 2
            eb = jnp.exp(b)
            qe = (q * eb).astype(BF16)
            s_a = s0_ref[tile * per_tile, h]
            s_b = s0_ref[tile * per_tile + 1, h]
            o_ab = _dot(qe, jnp.concatenate([s_a, s_b], axis=1).astype(BF16))
            o = jnp.where(first, o_ab[:, :LANES], o_ab[:, LANES:])
            for s in range(dec_seq):
                e = jnp.exp(jnp.minimum(b - pick(b, s), 0.0))
                wgt = jnp.sum(q * e * pick(k, s), axis=-1, keepdims=True)
                o = o + jnp.where(pos >= s, wgt, 0.0) * pick(v, s)
            og_ref[rows, cols] = _hgrn_norm_gate(o, ng[:, cols], g_all[rows, cols])

            b_last = pick(b, dec_seq - 1)
            kd = jnp.concatenate([k * jnp.exp(b_last - b), zpad], axis=0).astype(BF16)
            v_ab = jnp.concatenate([jnp.where(first, v, 0.0), jnp.where(first, 0.0, v)], axis=1)
            v_ab = jnp.concatenate([v_ab, zpad2], axis=0).astype(BF16)
            hi = eb.astype(BF16).astype(F32)
            mid = (eb - hi).astype(BF16).astype(F32)
            lo = eb - hi - mid
            pieces = jnp.concatenate([hi, mid, lo, zpad], axis=0).astype(BF16)
            e_ab = _dot_tn(pieces, sel_ab)
            upd_ab = _dot_tn(kd, v_ab)
            s1_ref[tile * per_tile, h] = s_a * e_ab[:, :LANES] + upd_ab[:, :LANES]
            s1_ref[tile * per_tile + 1, h] = s_b * e_ab[:, LANES:] + upd_ab[:, LANES:]


def _hgrn_sample(f, q, i, g, norm_g, s0, n_prompt, dec_batch, dec_seq, heads):
    rows = SAMPLE_BATCH_TILE * dec_seq
    first_blk = n_prompt // rows
    d_b = heads * LANES
    part = pl.BlockSpec((rows, d_b), lambda i: (first_blk + i, 0))
    return pl.pallas_call(
        functools.partial(_hgrn_sample_kernel, heads=heads, dec_seq=dec_seq),
        grid=(dec_batch // SAMPLE_BATCH_TILE,),
        in_specs=[part, part, part, part,
                  _resident((1, d_b), lambda i: (0, 0)),
                  pl.BlockSpec((SAMPLE_BATCH_TILE, heads, LANES, LANES), lambda i: (i, 0, 0, 0))],
        out_specs=[pl.BlockSpec((rows, d_b), lambda i: (i, 0)),
                   pl.BlockSpec((SAMPLE_BATCH_TILE, heads, LANES, LANES), lambda i: (i, 0, 0, 0))],
        out_shape=[jax.ShapeDtypeStruct((dec_batch * dec_seq, d_b), BF16),
                   jax.ShapeDtypeStruct(s0.shape, F32)],
        compiler_params=_cparams(("parallel",)),
        name="hgrn_sample",
    )(f, q, i, g, norm_g, s0)


def _out_proj_kernel(h_ref, xa_ref, xb_ref, w_ref, o_ref, *, na):
    x = jnp.where(pl.program_id(0) < na, xa_ref[...], xb_ref[...])
    o_ref[...] = h_ref[...] + _dot(x, w_ref[...])


def _out_proj(h, xa, xb, w, j_layer):
    t, d = h.shape
    kdim = xa.shape[1]
    na = xa.shape[0] // TM
    return pl.pallas_call(
        functools.partial(_out_proj_kernel, na=na),
        grid=(t // TM,),
        in_specs=[
            pl.BlockSpec((TM, d), lambda i: (i, 0)),
            pl.BlockSpec((TM, kdim), lambda i: (jnp.minimum(i, na - 1), 0)),
            pl.BlockSpec((TM, kdim), lambda i: (jnp.maximum(i - na, 0), 0)),
            _resident((None, kdim, d), lambda i: (j_layer, 0, 0)),
        ],
        out_specs=pl.BlockSpec((TM, d), lambda i: (i, 0)),
        out_shape=jax.ShapeDtypeStruct((t, d), F32),
        compiler_params=_cparams(("parallel",)),
        name="hgrn_out_proj",
    )(h, xa, xb, w)


def kernel(x_prompt, x_sample, state_hgrn, p_prompt, p_sample, ffn1_w_in, ffn1_w_out, ffn2_w_in, ffn2_w_out, norm_ffn1, norm_mix, norm_ffn2, norm_ple, a_w_in, a_ln_g, a_ln_b, a_w_s, a_b_s, a_w_out, b_w_in, b_lb_logits, b_norm_g, b_w_out, ple_w_proj, ple_w_gate, final_norm):
    batch, seq, d = x_prompt.shape
    dec_batch, dec_seq, _ = x_sample.shape
    depth = ffn1_w_in.shape[0]
    n_prompt = batch * seq
    n_sample = dec_batch * dec_seq
    t = n_prompt + n_sample
    heads = b_norm_g.shape[-1] // LANES
    assert n_prompt % TM == 0 and n_sample % TM == 0 and t % TM_STREAM == 0
    assert seq % HGRN_BLOCK == 0 and heads % HGRN_HEADS_PER_STEP == 0
    assert CHUNK_A % dec_seq == 0 and SUBLANES % dec_seq == 0 and dec_batch % SAMPLE_BATCH_TILE == 0
    assert a_w_s.shape[-1] == CHUNK_A and seq % CHUNK_A == 0

    p_a = p_prompt.reshape(depth, n_prompt, -1)
    p_b = p_sample.reshape(depth, n_sample, -1)

    bf = lambda w: w.astype(BF16)
    vec = lambda g: g.reshape(g.shape[0], 1, g.shape[-1])
    a_w_in_b, a_w_out_b, b_w_out_b = bf(a_w_in), bf(a_w_out), bf(b_w_out)
    ple_w_proj_b, ple_w_gate_b = bf(ple_w_proj), bf(ple_w_gate)
    n_ffn1, n_mix, n_ffn2, n_ple = vec(norm_ffn1), vec(norm_mix), vec(norm_ffn2), vec(norm_ple)
    ln_g, ln_b = vec(a_ln_g), vec(a_ln_b)
    final_g = final_norm.reshape(1, d)

    reps = CHUNK_A // dec_seq
    tri = jnp.tril(jnp.ones((CHUNK_A, CHUNK_A), bool))
    pos_c = jnp.arange(CHUNK_A) // dec_seq
    blockdiag = pos_c[:, None] == pos_c[None, :]
    w_prompt = jnp.where(tri, a_w_s, 0.0)
    w_sample = jnp.where(tri & blockdiag, jnp.tile(a_w_s[:, :, :dec_seq, :dec_seq], (1, 1, reps, reps)), 0.0)
    wmix = jnp.stack([w_prompt, w_sample], axis=1).astype(BF16)
    b_prompt = jnp.swapaxes(a_b_s, 1, 2)
    b_sample = jnp.tile(b_prompt[:, :dec_seq], (1, reps, 1))
    bias = jnp.repeat(jnp.stack([b_prompt, b_sample], axis=1), LANES, axis=-1)

    new_v, s_prompt, s_sample = [], [], []
    h = None
    for li in range(depth):
        if li == 0:
            h = _ffn(x_prompt.reshape(n_prompt, d), x_sample.reshape(n_sample, d), n_ffn1, ffn1_w_in, ffn1_w_out, li)
        else:
            h = _ffn(h, None, n_ffn1, ffn1_w_in, ffn1_w_out, li)
        j = li // 2
        if li % 2 == 0:
            u, v = _gmlp_proj(h, n_mix, a_w_in_b, ln_g, ln_b, li, j)
            new_v.append(v[n_prompt:].reshape(dec_batch, dec_seq, -1))
            h = _gmlp_mix(h, u, v, wmix[j], bias[j], a_w_out_b, j, n_prompt // TM)
        else:
            q, f, iv, g = _hgrn_proj(h, n_mix, b_w_in, b_lb_logits, li, j)
            ng = b_norm_g[j].reshape(1, heads * LANES)
            og_p, s_fin = _hgrn_prompt(f, q, iv, g, ng, batch, seq, heads)
            og_s, s_new = _hgrn_sample(f, q, iv, g, ng, state_hgrn[j], n_prompt, dec_batch, dec_seq, heads)
            s_prompt.append(s_fin)
            s_sample.append(s_new)
            h = _out_proj(h, og_p, og_s, b_w_out_b, j)
        h = _ffn(h, None, n_ffn2, ffn2_w_in, ffn2_w_out, li)
        h = _ple(h, p_a, p_b, n_ple, ple_w_gate_b, ple_w_proj_b, final_g, li, final=(li == depth - 1))

    y_prompt = h[0].reshape(batch, seq, d)
    y_sample = h[1].reshape(dec_batch, dec_seq, d)
    return (y_prompt, y_sample, jnp.stack(s_prompt), jnp.stack(s_sample), jnp.stack(new_v))
```

```python
import functools

import jax
import jax.numpy as jnp
from jax import lax
from jax.experimental import pallas as pl
from jax.experimental.pallas import tpu as pltpu

F32 = jnp.float32
BF16 = jnp.bfloat16
EPS = 1e-6

LANES = 128
SUBLANES = 8
VMEM_LIMIT_BYTES = 63 * 1024 * 1024

TM = 512
TM_STREAM = 1088
TF = 512
TNQ = 512
HGRN_BLOCK = 128
HGRN_HEADS_PER_STEP = 8
CHUNK_A = 128
SAMPLE_BATCH_TILE = 4


def _cparams(sem, fused_inputs=None):
    return pltpu.CompilerParams(dimension_semantics=sem, vmem_limit_bytes=VMEM_LIMIT_BYTES,
                                allow_input_fusion=fused_inputs)


def _rms(x, g):
    ms = jnp.mean(x * x, axis=-1, keepdims=True)
    return x * lax.rsqrt(ms + EPS) * g


def _resident(shape, index_map):
    return pl.BlockSpec(shape, index_map, pipeline_mode=pl.Buffered(1))


def _dot(a, b):
    return jnp.dot(a, b, preferred_element_type=F32)


def _dot_nt(a, b):
    return lax.dot_general(a, b, (((1,), (1,)), ((), ())), preferred_element_type=F32)


def _dot_tn(a, b):
    return lax.dot_general(a, b, (((0,), (0,)), ((), ())), preferred_element_type=F32)


def _ffn_kernel(xa_ref, xb_ref, g_ref, wg_ref, wu_ref, wo_ref, o_ref, xbuf_ref, xn_ref, sem, *, n_a, n_b):
    tm = o_ref.shape[0]
    full_a, rem_a = divmod(n_a, tm)
    n_tiles = (n_a + n_b) // tm

    def tile_copies(tile, act):
        @pl.when(tile < full_a)
        def _():
            act(pltpu.make_async_copy(xa_ref.at[pl.ds(tile * tm, tm)], xbuf_ref, sem.at[0]))

        if n_b:
            @pl.when(tile == full_a)
            def _():
                if rem_a:
                    act(pltpu.make_async_copy(xa_ref.at[pl.ds(full_a * tm, rem_a)],
                                              xbuf_ref.at[pl.ds(0, rem_a)], sem.at[0]))
                act(pltpu.make_async_copy(xb_ref, xbuf_ref.at[pl.ds(rem_a, n_b)], sem.at[1]))

    @pl.when(pl.program_id(1) == 0)
    def _():
        i = pl.program_id(0)

        @pl.when(i == 0)
        def _():
            tile_copies(i, lambda cp: cp.start())

        tile_copies(i, lambda cp: cp.wait())
        x = xbuf_ref[...]
        o_ref[...] = x
        xn_ref[...] = _rms(x, g_ref[...]).astype(BF16)

        @pl.when(i + 1 < n_tiles)
        def _():
            tile_copies(i + 1, lambda cp: cp.start())

    xn = xn_ref[...]
    gate = _dot(xn, wg_ref[...].astype(BF16))
    up = _dot(xn, wu_ref[...].astype(BF16))
    act = (0.5 * gate * jax.nn.sigmoid(gate) * up).astype(BF16)
    o_ref[...] += _dot(act, wo_ref[...].astype(BF16))


def _ffn(xa, xb, norm_g, w_in, w_out, li):
    n_a, d = xa.shape
    n_b = 0 if xb is None else xb.shape[0]
    t = n_a + n_b
    assert t % TM_STREAM == 0 and (n_b == 0 or n_a % TM_STREAM + n_b == TM_STREAM)
    d_ff = w_out.shape[1]
    nf = d_ff // TF
    return pl.pallas_call(
        functools.partial(_ffn_kernel, n_a=n_a, n_b=n_b),
        grid=(t // TM_STREAM, nf),
        in_specs=[
            pl.BlockSpec(memory_space=pl.ANY),
            pl.BlockSpec(memory_space=pl.ANY),
            pl.BlockSpec((None, 1, d), lambda i, j: (li, 0, 0)),
            pl.BlockSpec((None, d, TF), lambda i, j: (li, 0, j)),
            pl.BlockSpec((None, d, TF), lambda i, j: (li, 0, j + nf)),
            pl.BlockSpec((None, TF, d), lambda i, j: (li, j, 0)),
        ],
        out_specs=pl.BlockSpec((TM_STREAM, d), lambda i, j: (i, 0)),
        out_shape=jax.ShapeDtypeStruct((t, d), F32),
        scratch_shapes=[pltpu.VMEM((TM_STREAM, d), F32), pltpu.VMEM((TM_STREAM, d), BF16),
                        pltpu.SemaphoreType.DMA((2,))],
        compiler_params=_cparams(("arbitrary", "arbitrary")),
        name=f"ffn_l{li}",
    )(xa, xa if xb is None else xb, norm_g, w_in, w_in, w_out)


def _ple_kernel(h_ref, pa_ref, pb_ref, g_ref, wgate_ref, wproj_ref, fn_ref, *o_refs, final, na):
    i = pl.program_id(0)
    h = h_ref[...]
    hn = _rms(h, g_ref[...]).astype(BF16)
    gate = jax.nn.sigmoid(_dot(hn, wgate_ref[...]))
    p = jnp.where(i < na, pa_ref[...], pb_ref[...])
    proj = _dot(p.astype(BF16), wproj_ref[...])
    out = h + proj * gate
    if not final:
        o_refs[0][...] = out
        return
    out = _rms(out, fn_ref[...])

    @pl.when(i < na)
    def _():
        o_refs[0][...] = out

    @pl.when(i >= na)
    def _():
        o_refs[1][...] = out


def _ple(h, pa, pb, norm_g, w_gate, w_proj, final_g, li, final):
    t, d = h.shape
    dp = pa.shape[-1]
    na = pa.shape[1] // TM
    in_a = lambda i: jnp.minimum(i, na - 1)
    in_b = lambda i: jnp.maximum(i - na, 0)
    if final:
        out_specs = [pl.BlockSpec((TM, d), lambda i: (in_a(i), 0)), pl.BlockSpec((TM, d), lambda i: (in_b(i), 0))]
        out_shape = [jax.ShapeDtypeStruct((pa.shape[1], d), F32), jax.ShapeDtypeStruct((pb.shape[1], d), F32)]
    else:
        out_specs = pl.BlockSpec((TM, d), lambda i: (i, 0))
        out_shape = jax.ShapeDtypeStruct((t, d), F32)
    return pl.pallas_call(
        functools.partial(_ple_kernel, final=final, na=na),
        grid=(t // TM,),
        in_specs=[
            pl.BlockSpec((TM, d), lambda i: (i, 0)),
            pl.BlockSpec((None, TM, dp), lambda i: (li, in_a(i), 0)),
            pl.BlockSpec((None, TM, dp), lambda i: (li, in_b(i), 0)),
            _resident((None, 1, d), lambda i: (li, 0, 0)),
            _resident((None, d, d), lambda i: (li, 0, 0)),
            _resident((None, dp, d), lambda i: (li, 0, 0)),
            _resident((1, d), lambda i: (0, 0)),
        ],
        out_specs=out_specs,
        out_shape=out_shape,
        compiler_params=_cparams(("arbitrary",), [False, False, False, False, True, True, False]),
        name=f"ple_l{li}",
    )(h, pa, pb, norm_g, w_gate, w_proj, final_g)


def _gelu(z):
    return 0.5 * z * (1.0 + lax.erf(z * (2.0 ** -0.5)))


def _gmlp_proj_kernel(x_ref, g_ref, w_ref, lng_ref, lnb_ref, u_ref, v_ref, xn_ref):
    j = pl.program_id(1)

    @pl.when(j == 0)
    def _():
        xn_ref[...] = _rms(x_ref[...], g_ref[...]).astype(BF16)
        u_ref[...] = _gelu(_dot(xn_ref[...], w_ref[...])).astype(BF16)

    @pl.when(j == 1)
    def _():
        z = _gelu(_dot(xn_ref[...], w_ref[...]))
        mu = jnp.mean(z, axis=-1, keepdims=True)
        zc = z - mu
        var = jnp.mean(zc * zc, axis=-1, keepdims=True)
        v_ref[...] = zc * lax.rsqrt(var + EPS) * lng_ref[...] + lnb_ref[...]


def _gmlp_proj(h, norm_g, w_in, ln_g, ln_b, li, j_layer):
    t, d = h.shape
    d_a = w_in.shape[-1] // 2
    o_spec = pl.BlockSpec((TM, d_a), lambda i, j: (i, 0))
    return pl.pallas_call(
        _gmlp_proj_kernel,
        grid=(t // TM, 2),
        in_specs=[
            pl.BlockSpec((TM, d), lambda i, j: (i, 0)),
            pl.BlockSpec((None, 1, d), lambda i, j: (li, 0, 0)),
            pl.BlockSpec((None, d, d_a), lambda i, j: (j_layer, 0, j)),
            pl.BlockSpec((None, 1, d_a), lambda i, j: (j_layer, 0, 0)),
            pl.BlockSpec((None, 1, d_a), lambda i, j: (j_layer, 0, 0)),
        ],
        out_specs=[o_spec, o_spec],
        out_shape=[jax.ShapeDtypeStruct((t, d_a), BF16), jax.ShapeDtypeStruct((t, d_a), F32)],
        scratch_shapes=[pltpu.VMEM((TM, d), BF16)],
        compiler_params=_cparams(("parallel", "arbitrary")),
        name="gmlp_proj",
    )(h, norm_g, w_in, ln_g, ln_b)


def _gmlp_mix_kernel(h_ref, u_ref, v_ref, wmix_ref, bias_ref, wo_ref, o_ref, t_ref, *, groups):
    for c in range(TM // CHUNK_A):
        rows = slice(c * CHUNK_A, (c + 1) * CHUNK_A)
        for g in range(groups):
            cols = slice(g * LANES, (g + 1) * LANES)
            s = _dot(wmix_ref[g], v_ref[rows, cols].astype(BF16)) + bias_ref[:, cols]
            t_ref[rows, cols] = (u_ref[rows, cols].astype(F32) * s).astype(BF16)
    o_ref[...] = h_ref[...] + _dot(t_ref[...], wo_ref[...])


def _gmlp_mix(h, u, v, wmix, bias, w_out, j_layer, n_prompt_tiles):
    t, d = h.shape
    d_a = u.shape[1]
    groups = d_a // LANES
    return pl.pallas_call(
        functools.partial(_gmlp_mix_kernel, groups=groups),
        grid=(t // TM,),
        in_specs=[
            pl.BlockSpec((TM, d), lambda i: (i, 0)),
            pl.BlockSpec((TM, d_a), lambda i: (i, 0)),
            pl.BlockSpec((TM, d_a), lambda i: (i, 0)),
            pl.BlockSpec((None, groups, CHUNK_A, CHUNK_A), lambda i: (i // n_prompt_tiles, 0, 0, 0)),
            pl.BlockSpec((None, CHUNK_A, d_a), lambda i: (i // n_prompt_tiles, 0, 0)),
            _resident((None, d_a, d), lambda i: (j_layer, 0, 0)),
        ],
        out_specs=pl.BlockSpec((TM, d), lambda i: (i, 0)),
        out_shape=jax.ShapeDtypeStruct((t, d), F32),
        scratch_shapes=[pltpu.VMEM((TM, d_a), BF16)],
        compiler_params=_cparams(("parallel",), [False, False, False, False, False, True]),
        name="gmlp_mix",
    )(h, u, v, wmix, bias, w_out)


def _hgrn_proj_kernel(x_hbm_ref, g_ref, wq_ref, wf_ref, wi_ref, wg_ref, lbl_ref,
                      q_ref, f_ref, i_ref, gt_ref, xbuf_ref, xn_ref, sem, *, li, n_tiles):
    tm = xn_ref.shape[0]

    def row_copy(tile):
        return pltpu.make_async_copy(x_hbm_ref.at[pl.ds(tile * tm, tm)], xbuf_ref, sem.at[0])

    @pl.when(pl.program_id(1) == 0)
    def _():
        i = pl.program_id(0)

        @pl.when(i == 0)
        def _():
            row_copy(i).start()

        row_copy(i).wait()
        xn_ref[...] = _rms(xbuf_ref[...], g_ref[...]).astype(BF16)

        @pl.when(i + 1 < n_tiles)
        def _():
            row_copy(i + 1).start()

    xn = xn_ref[...]
    zq = _dot(xn, wq_ref[...].astype(BF16))
    q_ref[...] = (zq * jax.nn.sigmoid(zq)).astype(BF16)

    logits = lbl_ref[...]
    ex = jnp.exp(logits - jnp.max(logits, axis=0, keepdims=True))
    sm = ex / jnp.sum(ex, axis=0, keepdims=True)
    lb = jnp.sum(sm[: li + 1], axis=0, keepdims=True) - sm[0:1]
    zf = _dot(xn, wf_ref[...].astype(BF16))
    f_ref[...] = lb + (1.0 - lb) * jax.nn.sigmoid(zf)

    i_ref[...] = _dot(xn, wi_ref[...].astype(BF16)).astype(BF16)

    zg = _dot(xn, wg_ref[...].astype(BF16))
    gt_ref[...] = (zg * jax.nn.sigmoid(zg)).astype(BF16)


def _hgrn_proj(h, norm_g, w_in, lb_logits, li, j_layer):
    t, d = h.shape
    d_b = w_in.shape[-1] // 4
    per_part = d_b // TNQ
    w_spec = lambda p: pl.BlockSpec((None, d, TNQ), lambda i, j, p=p: (j_layer, 0, p * per_part + j))
    o_spec = pl.BlockSpec((TM_STREAM, TNQ), lambda i, j: (i, j))
    bf = jax.ShapeDtypeStruct((t, d_b), BF16)
    return pl.pallas_call(
        functools.partial(_hgrn_proj_kernel, li=li, n_tiles=t // TM_STREAM),
        grid=(t // TM_STREAM, per_part),
        in_specs=[
            pl.BlockSpec(memory_space=pl.ANY),
            pl.BlockSpec((None, 1, d), lambda i, j: (li, 0, 0)),
            w_spec(0), w_spec(1), w_spec(2), w_spec(3),
            pl.BlockSpec((lb_logits.shape[0], TNQ), lambda i, j: (0, j)),
        ],
        out_specs=[o_spec, o_spec, o_spec, o_spec],
        out_shape=[bf, jax.ShapeDtypeStruct((t, d_b), F32), bf, bf],
        scratch_shapes=[pltpu.VMEM((TM_STREAM, d), F32), pltpu.VMEM((TM_STREAM, d), BF16),
                        pltpu.SemaphoreType.DMA((1,))],
        compiler_params=_cparams(("arbitrary", "arbitrary")),
        name="hgrn_proj",
    )(h, norm_g, w_in, w_in, w_in, w_in, lb_logits)


def _hgrn_halves():
    halves, half = [], HGRN_BLOCK // 2
    while half >= 1:
        halves.append(half)
        half //= 2
    return tuple(halves)


def _hgrn_norm_gate(o, ng, gate):
    ms = jnp.mean(o * o, axis=-1, keepdims=True)
    return (o * lax.rsqrt(ms + EPS) * ng * gate).astype(BF16)


def _hgrn_prompt_kernel(f_ref, q_ref, i_ref, g_ref, ng_ref, og_ref, sfin_ref,
                        st_ref, tril_ref, sgn_ref, mask_ref, *, nblk, hps):
    c = HGRN_BLOCK
    w = hps * LANES
    halves = _hgrn_halves()
    nl = len(halves)

    @pl.when((pl.program_id(0) == 0) & (pl.program_id(1) == 0))
    def _():
        r = lax.broadcasted_iota(jnp.int32, (c, c), 0)
        s = lax.broadcasted_iota(jnp.int32, (c, c), 1)
        tril_ref[...] = jnp.where(s <= r, 1.0, 0.0).astype(BF16)
        for lvl, half in enumerate(halves):
            par = 2 * half
            second = (r & half) != 0
            sgn_ref[lvl] = jnp.where(second, 1.0, -1.0)
            valid = second & ((s & half) == 0) & ((r // par) == (s // par))
            mask_ref[lvl] = jnp.where(valid, 1.0, 0.0).astype(BF16)
        mask_ref[nl] = jnp.where(r == s, 1.0, 0.0).astype(BF16)

    st_ref[...] = jnp.zeros_like(st_ref)
    low_sub = lax.broadcasted_iota(jnp.int32, (c // SUBLANES, SUBLANES, w), 1) < SUBLANES // 2

    def body(blk, carry):
        rows = pl.ds(pl.multiple_of(blk * c, c), c)
        f = f_ref[rows, :]
        qb = q_ref[rows, :]
        vb = i_ref[rows, :]
        q = qb.astype(F32)
        k = 1.0 - f
        kb = k.astype(BF16)

        x = jnp.log(f)
        hi = x.astype(BF16)
        r1 = x - hi.astype(F32)
        mid = r1.astype(BF16)
        lo = (r1 - mid.astype(F32)).astype(BF16)
        tril = tril_ref[...]
        b = _dot(tril, hi) + _dot(tril, mid) + _dot(tril, lo)
        b3 = b.reshape(c // SUBLANES, SUBLANES, w)

        att = [None] * hps

        def add_level(lvl, qt, kt):
            for hd in range(hps):
                cols = slice(hd * LANES, (hd + 1) * LANES)
                a = _dot_nt(qt[:, cols], kt[:, cols]).astype(BF16) * mask_ref[lvl]
                att[hd] = a if att[hd] is None else att[hd] + a

        for lvl, half in enumerate(halves):
            par = 2 * half
            if half == 1:
                add_level(lvl, (q * f).astype(BF16), kb)
                continue
            if half >= SUBLANES:
                pieces = [jnp.broadcast_to(b[p * par + half - 1:p * par + half, :], (par, w))
                          for p in range(c // par)]
                beta = pieces[0] if len(pieces) == 1 else jnp.concatenate(pieces, axis=0)
            elif half == SUBLANES // 2:
                beta = jnp.broadcast_to(b3[:, half - 1:half, :], b3.shape).reshape(c, w)
            else:
                beta = jnp.where(low_sub, b3[:, half - 1:half, :],
                                 b3[:, par + half - 1:par + half, :]).reshape(c, w)
            sgn = jnp.concatenate([sgn_ref[lvl]] * hps, axis=1)
            e = jnp.exp((b - beta) * sgn).astype(BF16)
            add_level(lvl, qb * e, kb * e)
        add_level(nl, qb, kb)

        qe = (q * jnp.exp(b)).astype(BF16)
        b_last = b[c - 1:c, :]
        kd = (k * jnp.exp(b_last - b)).astype(BF16)
        e_last = jnp.exp(b_last)
        gate = g_ref[rows, :].astype(F32)
        ng = ng_ref[...]
        for hd in range(hps):
            cols = slice(hd * LANES, (hd + 1) * LANES)
            st = st_ref[hd]
            o = _dot_nt(qe[:, cols], st.astype(BF16)) + _dot(att[hd], vb[:, cols])
            og_ref[rows, cols] = _hgrn_norm_gate(o, ng[:, cols], gate[:, cols])
            st_ref[hd] = st * e_last[:, cols] + _dot_tn(vb[:, cols], kd[:, cols])
        return carry

    lax.fori_loop(0, nblk, body, 0, unroll=4)
    for hd in range(hps):
        sfin_ref[hd] = st_ref[hd].T


def _hgrn_prompt(f, q, i, g, norm_g, batch, seq, heads):
    hps = HGRN_HEADS_PER_STEP
    w = hps * LANES
    ngrp = heads // hps
    nblk = seq // HGRN_BLOCK
    nl = len(_hgrn_halves())
    part = pl.BlockSpec((seq, w), lambda s, h: (s, h))
    return pl.pallas_call(
        functools.partial(_hgrn_prompt_kernel, nblk=nblk, hps=hps),
        grid=(batch, ngrp),
        in_specs=[part, part, part, part,
                  pl.BlockSpec((1, w), lambda s, h: (0, h))],
        out_specs=[pl.BlockSpec((seq, w), lambda s, h: (s, h)),
                   pl.BlockSpec((None, hps, LANES, LANES), lambda s, h: (s, h, 0, 0))],
        out_shape=[jax.ShapeDtypeStruct((batch * seq, heads * LANES), BF16),
                   jax.ShapeDtypeStruct((batch, heads, LANES, LANES), F32)],
        scratch_shapes=[pltpu.VMEM((hps, LANES, LANES), F32),
                        pltpu.VMEM((HGRN_BLOCK, HGRN_BLOCK), BF16),
                        pltpu.VMEM((nl, HGRN_BLOCK, LANES), F32),
                        pltpu.VMEM((nl + 1, HGRN_BLOCK, HGRN_BLOCK), BF16)],
        compiler_params=_cparams(("arbitrary", "arbitrary")),
        name="hgrn_prompt",
    )(f, q, i, g, norm_g)


def _hgrn_sample_kernel(f_ref, q_ref, i_ref, g_ref, ng_ref, s0_ref, og_ref, s1_ref, *, heads, dec_seq):
    rows_per_tile = SUBLANES
    per_tile = rows_per_tile // dec_seq
    row = lax.broadcasted_iota(jnp.int32, (rows_per_tile, LANES), 0)
    pos = row % dec_seq
    first = row < dec_seq
    zpad = jnp.zeros((rows_per_tile, LANES), F32)
    zpad2 = jnp.zeros((rows_per_tile, 2 * LANES), F32)
    prow = lax.broadcasted_iota(jnp.int32, (4 * rows_per_tile, 2 * LANES), 0)
    plane = lax.broadcasted_iota(jnp.int32, (4 * rows_per_tile, 2 * LANES), 1)
    last_row = jnp.where(plane < LANES, dec_seq - 1, 2 * dec_seq - 1)
    sel_ab = jnp.where((prow < 3 * rows_per_tile) & (prow % rows_per_tile == last_row), 1.0, 0.0).astype(BF16)
    f_all = f_ref[...]
    q_all = q_ref[...].astype(F32)
    v_all = i_ref[...].astype(F32)
    g_all = g_ref[...].astype(F32)
    ng = ng_ref[...]

    def pick(x, s):
        return jnp.where(first, x[s:s + 1, :], x[dec_seq + s:dec_seq + s + 1, :])

    for tile in range(SAMPLE_BATCH_TILE // per_tile):
        rows = slice(tile * rows_per_tile, (tile + 1) * rows_per_tile)
        for h in range(heads):
            cols = slice(h * LANES, (h + 1) * LANES)
            q = q_all[rows, cols]
            f = f_all[rows, cols]
            v = v_all[rows, cols]
            k = 1.0 - f
            b = jnp.log(f)
            shift = 1
            while shift < dec_seq:
                b = b + jnp.where(pos >= shift, pltpu.roll(b, shift, 0), 0.0)
                shift *= 2
            eb = jnp.exp(b)
            qe = (q * eb).astype(BF16)
            s_a = s0_ref[tile * per_tile, h]
            s_b = s0_ref[tile * per_tile + 1, h]
            o_ab = _dot(qe, jnp.concatenate([s_a, s_b], axis=1).astype(BF16))
            o = jnp.where(first, o_ab[:, :LANES], o_ab[:, LANES:])
            for s in range(dec_seq):
                e = jnp.exp(jnp.minimum(b - pick(b, s), 0.0))
                wgt = jnp.sum(q * e * pick(k, s), axis=-1, keepdims=True)
                o = o + jnp.where(pos >= s, wgt, 0.0) * pick(v, s)
            og_ref[rows, cols] = _hgrn_norm_gate(o, ng[:, cols], g_all[rows, cols])

            b_last = pick(b, dec_seq - 1)
            kd = jnp.concatenate([k * jnp.exp(b_last - b), zpad], axis=0).astype(BF16)
            v_ab = jnp.concatenate([jnp.where(first, v, 0.0), jnp.where(first, 0.0, v)], axis=1)
            v_ab = jnp.concatenate([v_ab, zpad2], axis=0).astype(BF16)
            hi = eb.astype(BF16).astype(F32)
            mid = (eb - hi).astype(BF16).astype(F32)
            lo = eb - hi - mid
            pieces = jnp.concatenate([hi, mid, lo, zpad], axis=0).astype(BF16)
            e_ab = _dot_tn(pieces, sel_ab)
            upd_ab = _dot_tn(kd, v_ab)
            s1_ref[tile * per_tile, h] = s_a * e_ab[:, :LANES] + upd_ab[:, :LANES]
            s1_ref[tile * per_tile + 1, h] = s_b * e_ab[:, LANES:] + upd_ab[:, LANES:]


def _hgrn_sample(f, q, i, g, norm_g, s0, n_prompt, dec_batch, dec_seq, heads):
    rows = SAMPLE_BATCH_TILE * dec_seq
    first_blk = n_prompt // rows
    d_b = heads * LANES
    part = pl.BlockSpec((rows, d_b), lambda i: (first_blk + i, 0))
    return pl.pallas_call(
        functools.partial(_hgrn_sample_kernel, heads=heads, dec_seq=dec_seq),
        grid=(dec_batch // SAMPLE_BATCH_TILE,),
        in_specs=[part, part, part, part,
                  _resident((1, d_b), lambda i: (0, 0)),
                  pl.BlockSpec((SAMPLE_BATCH_TILE, heads, LANES, LANES), lambda i: (i, 0, 0, 0))],
        out_specs=[pl.BlockSpec((rows, d_b), lambda i: (i, 0)),
                   pl.BlockSpec((SAMPLE_BATCH_TILE, heads, LANES, LANES), lambda i: (i, 0, 0, 0))],
        out_shape=[jax.ShapeDtypeStruct((dec_batch * dec_seq, d_b), BF16),
                   jax.ShapeDtypeStruct(s0.shape, F32)],
        compiler_params=_cparams(("parallel",)),
        name="hgrn_sample",
    )(f, q, i, g, norm_g, s0)


def _out_proj_kernel(h_ref, xa_ref, xb_ref, w_ref, o_ref, *, na):
    x = jnp.where(pl.program_id(0) < na, xa_ref[...], xb_ref[...])
    o_ref[...] = h_ref[...] + _dot(x, w_ref[...])


def _out_proj(h, xa, xb, w, j_layer):
    t, d = h.shape
    kdim = xa.shape[1]
    na = xa.shape[0] // TM
    return pl.pallas_call(
        functools.partial(_out_proj_kernel, na=na),
        grid=(t // TM,),
        in_specs=[
            pl.BlockSpec((TM, d), lambda i: (i, 0)),
            pl.BlockSpec((TM, kdim), lambda i: (jnp.minimum(i, na - 1), 0)),
            pl.BlockSpec((TM, kdim), lambda i: (jnp.maximum(i - na, 0), 0)),
            _resident((None, kdim, d), lambda i: (j_layer, 0, 0)),
        ],
        out_specs=pl.BlockSpec((TM, d), lambda i: (i, 0)),
        out_shape=jax.ShapeDtypeStruct((t, d), F32),
        compiler_params=_cparams(("parallel",), [False, False, False, True]),
        name="hgrn_out_proj",
    )(h, xa, xb, w)


def kernel(x_prompt, x_sample, state_hgrn, p_prompt, p_sample, ffn1_w_in, ffn1_w_out, ffn2_w_in, ffn2_w_out, norm_ffn1, norm_mix, norm_ffn2, norm_ple, a_w_in, a_ln_g, a_ln_b, a_w_s, a_b_s, a_w_out, b_w_in, b_lb_logits, b_norm_g, b_w_out, ple_w_proj, ple_w_gate, final_norm):
    batch, seq, d = x_prompt.shape
    dec_batch, dec_seq, _ = x_sample.shape
    depth = ffn1_w_in.shape[0]
    n_prompt = batch * seq
    n_sample = dec_batch * dec_seq
    t = n_prompt + n_sample
    heads = b_norm_g.shape[-1] // LANES
    assert n_prompt % TM == 0 and n_sample % TM == 0 and t % TM_STREAM == 0
    assert seq % HGRN_BLOCK == 0 and heads % HGRN_HEADS_PER_STEP == 0
    assert CHUNK_A % dec_seq == 0 and SUBLANES % dec_seq == 0 and dec_batch % SAMPLE_BATCH_TILE == 0
    assert a_w_s.shape[-1] == CHUNK_A and seq % CHUNK_A == 0

    p_a = p_prompt.reshape(depth, n_prompt, -1)
    p_b = p_sample.reshape(depth, n_sample, -1)

    bf = lambda w: w.astype(BF16)
    vec = lambda g: g.reshape(g.shape[0], 1, g.shape[-1])
    a_w_in_b, a_w_out_b, b_w_out_b = bf(a_w_in), bf(a_w_out), bf(b_w_out)
    ple_w_proj_b, ple_w_gate_b = bf(ple_w_proj), bf(ple_w_gate)
    n_ffn1, n_mix, n_ffn2, n_ple = vec(norm_ffn1), vec(norm_mix), vec(norm_ffn2), vec(norm_ple)
    ln_g, ln_b = vec(a_ln_g), vec(a_ln_b)
    final_g = final_norm.reshape(1, d)

    reps = CHUNK_A // dec_seq
    tri = jnp.tril(jnp.ones((CHUNK_A, CHUNK_A), bool))
    pos_c = jnp.arange(CHUNK_A) // dec_seq
    blockdiag = pos_c[:, None] == pos_c[None, :]
    w_prompt = jnp.where(tri, a_w_s, 0.0)
    w_sample = jnp.where(tri & blockdiag, jnp.tile(a_w_s[:, :, :dec_seq, :dec_seq], (1, 1, reps, reps)), 0.0)
    wmix = jnp.stack([w_prompt, w_sample], axis=1).astype(BF16)
    b_prompt = jnp.swapaxes(a_b_s, 1, 2)
    b_sample = jnp.tile(b_prompt[:, :dec_seq], (1, reps, 1))
    bias = jnp.repeat(jnp.stack([b_prompt, b_sample], axis=1), LANES, axis=-1)

    new_v, s_prompt, s_sample = [], [], []
    h = None
    for li in range(depth):
        if li == 0:
            h = _ffn(x_prompt.reshape(n_prompt, d), x_sample.reshape(n_sample, d), n_ffn1, ffn1_w_in, ffn1_w_out, li)
        else:
            h = _ffn(h, None, n_ffn1, ffn1_w_in, ffn1_w_out, li)
        j = li // 2
        if li % 2 == 0:
            u, v = _gmlp_proj(h, n_mix, a_w_in_b, ln_g, ln_b, li, j)
            new_v.append(v[n_prompt:].reshape(dec_batch, dec_seq, -1))
            h = _gmlp_mix(h, u, v, wmix[j], bias[j], a_w_out_b, j, n_prompt // TM)
        else:
            q, f, iv, g = _hgrn_proj(h, n_mix, b_w_in, b_lb_logits, li, j)
            ng = b_norm_g[j].reshape(1, heads * LANES)
            og_p, s_fin = _hgrn_prompt(f, q, iv, g, ng, batch, seq, heads)
            og_s, s_new = _hgrn_sample(f, q, iv, g, ng, state_hgrn[j], n_prompt, dec_batch, dec_seq, heads)
            s_prompt.append(s_fin)
            s_sample.append(s_new)
            h = _out_proj(h, og_p, og_s, b_w_out_b, j)
        h = _ffn(h, None, n_ffn2, ffn2_w_in, ffn2_w_out, li)
        h = _ple(h, p_a, p_b, n_ple, ple_w_gate_b, ple_w_proj_b, final_g, li, final=(li == depth - 1))

    y_prompt = h[0].reshape(batch, seq, d)
    y_sample = h[1].reshape(dec_batch, dec_seq, d)
    return (y_prompt, y_sample, jnp.stack(s_prompt), jnp.stack(s_sample), jnp.stack(new_v))
```
